```python
import jax, jax.numpy as jnp
from jax import lax
import numpy as np


D_MODEL = 1024
BATCH = 2
SEQ = 8192
DEPTH = 2

N_MIXERS = 2
N_LAYERS_A = (DEPTH + 1) // 2
N_LAYERS_B = DEPTH // 2
MLA_HEADS = 8
QK_NOPE = 128
QK_ROPE = 64
V_DIM = 128
Q_LORA = 384
KV_LORA = 256
ROPE_THETA = 10000.0
SWA_HEADS = 16
SWA_KV_HEADS = 4
SWA_HEAD_DIM = 64
WINDOW = 128
D_FF = 4 * D_MODEL
BLOCK_Q = 128
EPS = 1e-6

kernel_name = 'hybrid_mla_swa_sink_alibi_adaln'


def rmsnorm(x, g):
    xf = x.astype(jnp.float32)
    y = xf * lax.rsqrt(jnp.mean(xf * xf, axis=-1, keepdims=True) + EPS)
    return (y * g.astype(jnp.float32)).astype(x.dtype)


def modulate(x, g, shift, scale):
    return rmsnorm(x, g) * (1.0 + scale[:, None, :]) + shift[:, None, :]


def rope(x, positions):
    half = QK_ROPE // 2
    inv_freq = ROPE_THETA ** (-jnp.arange(half, dtype=jnp.float32) / half)
    ang = positions.astype(jnp.float32)[..., None] * inv_freq
    shape = ang.shape[:2] + (1,) * (x.ndim - 3) + (half,)
    cos = jnp.cos(ang).reshape(shape)
    sin = jnp.sin(ang).reshape(shape)
    xf = x.astype(jnp.float32)
    x1, x2 = xf[..., :half], xf[..., half:]
    out = jnp.concatenate([x1 * cos - x2 * sin, x1 * sin + x2 * cos], axis=-1)
    return out.astype(x.dtype)


def alibi_slopes(n_heads):
    return jnp.asarray(2.0 ** (-8.0 * np.arange(1, n_heads + 1) / n_heads), dtype=jnp.float32)


def mla(h, positions, w_dq, g_q, w_uq, w_dkv, g_kv, w_ukv, w_o):
    B, S, _ = h.shape
    H = MLA_HEADS
    cq = rmsnorm(h @ w_dq, g_q)
    q = (cq @ w_uq).reshape(B, S, H, QK_NOPE + QK_ROPE)
    q_nope = q[..., :QK_NOPE]
    q_rope = rope(q[..., QK_NOPE:], positions)
    ckv_kr = h @ w_dkv
    ckv = rmsnorm(ckv_kr[..., :KV_LORA], g_kv)
    k_rope = rope(ckv_kr[..., KV_LORA:], positions)
    kv = (ckv @ w_ukv).reshape(B, S, H, QK_NOPE + V_DIM)
    k_nope, v = kv[..., :QK_NOPE], kv[..., QK_NOPE:]
    scale = (QK_NOPE + QK_ROPE) ** -0.5
    n_blk = S // BLOCK_Q
    qn_blk = q_nope.reshape(B, n_blk, BLOCK_Q, H, QK_NOPE).transpose(1, 0, 2, 3, 4)
    qr_blk = q_rope.reshape(B, n_blk, BLOCK_Q, H, QK_ROPE).transpose(1, 0, 2, 3, 4)
    key_idx = jnp.arange(S)

    def one_block(args):
        i, qn, qr = args
        s = (jnp.einsum('bqhd,bkhd->bhqk', qn, k_nope)
             + jnp.einsum('bqhr,bkr->bhqk', qr, k_rope)).astype(jnp.float32) * scale
        q_idx = i * BLOCK_Q + jnp.arange(BLOCK_Q)
        causal = key_idx[None, :] <= q_idx[:, None]
        s = jnp.where(causal[None, None], s, -jnp.inf)
        p = jax.nn.softmax(s, axis=-1).astype(v.dtype)
        return jnp.einsum('bhqk,bkhd->bqhd', p, v)

    o = lax.map(one_block, (jnp.arange(n_blk), qn_blk, qr_blk))
    o = o.transpose(1, 0, 2, 3, 4).reshape(B, S, H * V_DIM)
    return o @ w_o


def swa(h, w_qkv, b_qkv, sinks, w_o, b_o):
    B, S, _ = h.shape
    Hq, Hk, Dh, W = SWA_HEADS, SWA_KV_HEADS, SWA_HEAD_DIM, WINDOW
    G = Hq // Hk
    qkv = h @ w_qkv + b_qkv
    q = qkv[..., :Hq * Dh]
    k = qkv[..., Hq * Dh:(Hq + Hk) * Dh].reshape(B, S, Hk, Dh)
    v = qkv[..., (Hq + Hk) * Dh:].reshape(B, S, Hk, Dh)
    n_blk = S // W
    qb = q.reshape(B, n_blk, W, Hk, G, Dh)

    def band(t):
        tb = t.reshape(B, n_blk, W, Hk, Dh)
        prev = jnp.pad(tb[:, :-1], ((0, 0), (1, 0), (0, 0), (0, 0), (0, 0)))
        return jnp.concatenate([prev, tb], axis=2)

    kb, vb = band(k), band(v)
    s = jnp.einsum('bnqkgd,bnjkd->bnkgqj', qb, kb).astype(jnp.float32) * (Dh ** -0.5)
    dist = W + jnp.arange(W)[:, None] - jnp.arange(2 * W)[None, :]
    in_window = (dist >= 0) & (dist < W)
    real_key = (jnp.arange(n_blk)[:, None] > 0) | (jnp.arange(2 * W)[None, :] >= W)
    mask = in_window[None] & real_key[:, None, :]
    slopes = alibi_slopes(Hq).reshape(Hk, G)
    s = s - slopes[:, :, None, None] * dist.astype(jnp.float32)
    s = jnp.where(mask[None, :, None, None], s, -jnp.inf)
    sink = sinks.astype(jnp.float32).reshape(Hk, G)[:, :, None]
    m = jnp.maximum(s.max(axis=-1), sink)
    p = jnp.exp(s - m[..., None])
    denom = p.sum(axis=-1) + jnp.exp(sink - m)
    p = (p / denom[..., None]).astype(vb.dtype)
    o = jnp.einsum('bnkgqj,bnjkd->bnqkgd', p, vb).reshape(B, S, Hq * Dh)
    return o @ w_o + b_o


def setup_inputs(seed: int = 0) -> dict:
    key = jax.random.key(seed)
    ks = jax.random.split(key, 24)
    f32 = jnp.float32

    def w(k, shape, fan_in, gain=1.0):
        return jax.random.normal(k, shape, f32) * (gain * fan_in ** -0.5)

    def g(k, shape):
        return 1.0 + 0.05 * jax.random.normal(k, shape, f32)

    A, Bn = N_LAYERS_A, N_LAYERS_B
    x = jax.random.normal(ks[0], (BATCH, SEQ, D_MODEL), f32)
    c = jax.random.normal(ks[1], (BATCH, D_MODEL), f32)
    positions = (jnp.arange(SEQ, dtype=jnp.int32)[None, :]
                 + jax.random.randint(ks[2], (BATCH, 1), 0, 1024, dtype=jnp.int32))
    return {
        'x': x,
        'c': c,
        'positions': positions,
        'w_ada': w(ks[3], (DEPTH, D_MODEL, 6 * D_MODEL), D_MODEL, 0.5),
        'b_ada': 0.02 * jax.random.normal(ks[4], (DEPTH, 6 * D_MODEL), f32),
        'g_mix': g(ks[5], (DEPTH, D_MODEL)),
        'g_mlp': g(ks[6], (DEPTH, D_MODEL)),
        'mla_w_dq': w(ks[7], (A, D_MODEL, Q_LORA), D_MODEL),
        'mla_g_q': g(ks[8], (A, Q_LORA)),
        'mla_w_uq': w(ks[9], (A, Q_LORA, MLA_HEADS * (QK_NOPE + QK_ROPE)), Q_LORA),
        'mla_w_dkv': w(ks[10], (A, D_MODEL, KV_LORA + QK_ROPE), D_MODEL),
        'mla_g_kv': g(ks[11], (A, KV_LORA)),
        'mla_w_ukv': w(ks[12], (A, KV_LORA, MLA_HEADS * (QK_NOPE + V_DIM)), KV_LORA),
        'mla_w_o': w(ks[13], (A, MLA_HEADS * V_DIM, D_MODEL), MLA_HEADS * V_DIM),
        'swa_w_qkv': w(ks[14], (Bn, D_MODEL, (SWA_HEADS + 2 * SWA_KV_HEADS) * SWA_HEAD_DIM), D_MODEL),
        'swa_b_qkv': 0.02 * jax.random.normal(ks[15], (Bn, (SWA_HEADS + 2 * SWA_KV_HEADS) * SWA_HEAD_DIM), f32),
        'swa_sinks': 0.5 * jax.random.normal(ks[16], (Bn, SWA_HEADS), f32),
        'swa_w_o': w(ks[17], (Bn, SWA_HEADS * SWA_HEAD_DIM, D_MODEL), SWA_HEADS * SWA_HEAD_DIM),
        'swa_b_o': 0.02 * jax.random.normal(ks[18], (Bn, D_MODEL), f32),
        'w_ff1': w(ks[19], (DEPTH, D_MODEL, D_FF), D_MODEL),
        'w_ff2': w(ks[20], (DEPTH, D_FF, D_MODEL), D_FF),
        'g_final': g(ks[21], (D_MODEL,)),
    }


def reference(x, c, positions, w_ada, b_ada, g_mix, g_mlp,
              mla_w_dq, mla_g_q, mla_w_uq, mla_w_dkv, mla_g_kv, mla_w_ukv, mla_w_o,
              swa_w_qkv, swa_b_qkv, swa_sinks, swa_w_o, swa_b_o,
              w_ff1, w_ff2, g_final):
    cond = jax.nn.silu(c)
    for i in range(DEPTH):
        mod = cond @ w_ada[i] + b_ada[i]
        sh1, sc1, gt1, sh2, sc2, gt2 = jnp.split(mod, 6, axis=-1)
        h = modulate(x, g_mix[i], sh1, sc1)
        j = i // N_MIXERS
        if i % N_MIXERS == 0:
            y = mla(h, positions, mla_w_dq[j], mla_g_q[j], mla_w_uq[j], mla_w_dkv[j],
                    mla_g_kv[j], mla_w_ukv[j], mla_w_o[j])
        else:
            y = swa(h, swa_w_qkv[j], swa_b_qkv[j], swa_sinks[j], swa_w_o[j], swa_b_o[j])
        x = x + gt1[:, None, :] * y
        h = modulate(x, g_mlp[i], sh2, sc2)
        y = jnp.square(jax.nn.relu(h @ w_ff1[i])) @ w_ff2[i]
        x = x + gt2[:, None, :] * y
    return rmsnorm(x, g_final)
```

```python
import functools

import numpy as np
import jax
import jax.numpy as jnp
from jax import lax
from jax.experimental import pallas as pl
from jax.experimental.pallas import tpu as pltpu

F32 = jnp.float32
BF16 = jnp.bfloat16

EPS = 1e-6
ROPE_THETA = 10000.0
MLA_HEADS = 8
QK_NOPE = 128
QK_ROPE = 64
V_DIM = 128
KV_LORA = 256
SWA_HEADS = 16
SWA_KV_HEADS = 4
SWA_HEAD_DIM = 64
WINDOW = 128

LANES = 128
QK_PAD = 2 * LANES
VMEM_LIMIT = 56 * 1024 * 1024

NEG_INF = float("-inf")


def _params(*sem):
    return pltpu.CompilerParams(dimension_semantics=sem, vmem_limit_bytes=VMEM_LIMIT)


def _rmsnorm(x, g):
    y = x * lax.rsqrt(jnp.mean(x * x, axis=-1, keepdims=True) + EPS)
    return y * g


def _dot(a, b):
    return jnp.dot(a, b, preferred_element_type=F32)


def _dot_nt(a, b):
    return lax.dot_general(a, b, (((1,), (1,)), ((), ())), preferred_element_type=F32)


def _adaln_kernel(ct_ref, w_ref, b_ref, o_ref):
    ct = ct_ref[...]
    cond = ct * jax.nn.sigmoid(ct)
    w = w_ref[...]
    rows = [jnp.sum(w * cond[:, b:b + 1], axis=0, keepdims=True)
            for b in range(ct.shape[1])]
    o_ref[...] = jnp.concatenate(rows, axis=0) + b_ref[...]


def _adaln(c, w_ada, b_ada):
    depth, d, n = w_ada.shape
    bsz = c.shape[0]
    tn = 1024
    return pl.pallas_call(
        _adaln_kernel,
        grid=(depth, n // tn),
        in_specs=[
            pl.BlockSpec((d, bsz), lambda l, j: (0, 0)),
            pl.BlockSpec((None, d, tn), lambda l, j: (l, 0, j)),
            pl.BlockSpec((None, 1, tn), lambda l, j: (l, 0, j)),
        ],
        out_specs=pl.BlockSpec((None, bsz, tn), lambda l, j: (l, 0, j)),
        out_shape=jax.ShapeDtypeStruct((depth, bsz, n), F32),
        compiler_params=_params("parallel", "parallel"),
        name="adaln",
    )(c.T, w_ada, b_ada.reshape(depth, 1, n))


def _rope_table_kernel(pos_ref, inv_ref, cos_ref, sin_ref):
    ang = pos_ref[...].astype(F32) * inv_ref[...]
    cos_ref[...] = jnp.cos(ang)
    sin_ref[...] = jnp.sin(ang)


def _rope_tables(positions):
    n = positions.size
    half = QK_ROPE // 2
    inv_freq = ROPE_THETA ** (-jnp.arange(half, dtype=F32) / half)
    inv = jnp.tile(inv_freq, LANES // half).reshape(1, LANES)
    tm = 2048
    return pl.pallas_call(
        _rope_table_kernel,
        grid=(n // tm,),
        in_specs=[pl.BlockSpec((tm, 1), lambda i: (i, 0)),
                  pl.BlockSpec((1, LANES), lambda i: (0, 0))],
        out_specs=[pl.BlockSpec((tm, LANES), lambda i: (i, 0)),
                   pl.BlockSpec((tm, LANES), lambda i: (i, 0))],
        out_shape=[jax.ShapeDtypeStruct((n, LANES), F32)] * 2,
        compiler_params=_params("parallel"),
        name="rope_table",
    )(positions.reshape(n, 1), inv)


def _mla_proj_kernel(x_ref, mod_ref, g_ref, cos_ref, sin_ref, wdq_ref, gq_ref, wuq_ref,
                     wdkv_ref, gkv_ref, wukv_ref, q_ref, k_ref, v_ref, *, scale):
    d = x_ref.shape[-1]
    mod = mod_ref[...]
    sh1, sc1 = mod[:, 0:d], mod[:, d:2 * d]
    h = (_rmsnorm(x_ref[...], g_ref[...]) * (1.0 + sc1) + sh1).astype(BF16)
    cos, sin = cos_ref[...], sin_ref[...]
    lane = lax.broadcasted_iota(jnp.int32, cos.shape, 1)
    low = lane < QK_ROPE

    cq = _rmsnorm(_dot(h, wdq_ref[...]), gq_ref[...]).astype(BF16)
    qe = _dot(cq, wuq_ref[...])
    n_nope = MLA_HEADS * QK_NOPE
    n_rope = MLA_HEADS * QK_ROPE
    for hd in range(MLA_HEADS):
        q_ref[hd, :, 0:LANES] = (qe[:, hd * LANES:(hd + 1) * LANES] * scale).astype(BF16)
        j = hd // 2
        a = qe[:, n_nope + j * LANES:n_nope + (j + 1) * LANES]
        b = qe[:, n_nope + n_rope + j * LANES:n_nope + n_rope + (j + 1) * LANES]
        r = (a * cos + b * sin) * scale
        keep = low if hd % 2 == 0 else jnp.logical_not(low)
        q_ref[hd, :, LANES:QK_PAD] = jnp.where(keep, r, 0.0).astype(BF16)

    dkv = _dot(h, wdkv_ref[...])
    ckv = _rmsnorm(dkv[:, 0:KV_LORA], gkv_ref[...]).astype(BF16)
    kr = dkv[:, KV_LORA:KV_LORA + LANES] * cos + dkv[:, KV_LORA + LANES:KV_LORA + 2 * LANES] * sin
    kr_even = jnp.where(low, kr, 0.0).astype(BF16)
    kr_odd = jnp.where(low, 0.0, kr).astype(BF16)
    kv = _dot(ckv, wukv_ref[...])
    for hd in range(MLA_HEADS):
        base = hd * (QK_NOPE + V_DIM)
        k_ref[hd, :, 0:LANES] = kv[:, base:base + QK_NOPE].astype(BF16)
        k_ref[hd, :, LANES:QK_PAD] = kr_even if hd % 2 == 0 else kr_odd
        v_ref[hd] = kv[:, base + QK_NOPE:base + QK_NOPE + V_DIM].astype(BF16)


def _rot_half(w):
    half = w.shape[-1] // 2
    return jnp.concatenate([-w[..., half:], w[..., :half]], axis=-1)


def _mla_proj(x, mods, layer, g, cos_t, sin_t, w_dq, g_q, w_uq, w_dkv, g_kv, w_ukv):
    bsz, s, d = x.shape
    hh = MLA_HEADS
    q_lora = w_dq.shape[1]
    tm = 512
    wq3 = w_uq.reshape(q_lora, hh, QK_NOPE + QK_ROPE)
    wq_rope = wq3[:, :, QK_NOPE:]
    wuq_ext = jnp.concatenate([
        wq3[:, :, :QK_NOPE].reshape(q_lora, hh * QK_NOPE),
        wq_rope.reshape(q_lora, hh * QK_ROPE),
        _rot_half(wq_rope).reshape(q_lora, hh * QK_ROPE)], axis=1).astype(BF16)
    wkr = w_dkv[:, KV_LORA:]
    wkr_rot = _rot_half(wkr)
    wdkv_ext = jnp.concatenate([w_dkv[:, :KV_LORA], wkr, wkr, wkr_rot, wkr_rot], axis=1).astype(BF16)
    scale = float((QK_NOPE + QK_ROPE) ** -0.5)

    const = lambda b, i: (0, 0)
    tok = lambda b, i: (b * (s // tm) + i, 0)
    out_map = lambda b, i: (b, 0, i, 0)
    return pl.pallas_call(
        functools.partial(_mla_proj_kernel, scale=scale),
        grid=(bsz, s // tm),
        in_specs=[
            pl.BlockSpec((None, tm, d), lambda b, i: (b, i, 0)),
            pl.BlockSpec((None, None, 1, mods.shape[-1]), lambda b, i: (layer, b, 0, 0)),
            pl.BlockSpec((1, d), const),
            pl.BlockSpec((tm, LANES), tok),
            pl.BlockSpec((tm, LANES), tok),
            pl.BlockSpec(w_dq.shape, const),
            pl.BlockSpec((1, q_lora), const),
            pl.BlockSpec(wuq_ext.shape, const),
            pl.BlockSpec(wdkv_ext.shape, const),
            pl.BlockSpec((1, KV_LORA), const),
            pl.BlockSpec(w_ukv.shape, const),
        ],
        out_specs=[
            pl.BlockSpec((None, hh, tm, QK_PAD), out_map),
            pl.BlockSpec((None, hh, tm, QK_PAD), out_map),
            pl.BlockSpec((None, hh, tm, V_DIM), out_map),
        ],
        out_shape=[
            jax.ShapeDtypeStruct((bsz, hh, s, QK_PAD), BF16),
            jax.ShapeDtypeStruct((bsz, hh, s, QK_PAD), BF16),
            jax.ShapeDtypeStruct((bsz, hh, s, V_DIM), BF16),
        ],
        compiler_params=_params("parallel", "parallel"),
        name="mla_proj",
    )(x, mods, g.reshape(1, d), cos_t, sin_t, w_dq.astype(BF16), g_q.reshape(1, q_lora),
      wuq_ext, wdkv_ext, g_kv.reshape(1, KV_LORA), w_ukv.astype(BF16))


def _mla_flash_kernel(q_ref, k_ref, v_ref, o_ref, m_sc, l_sc, acc_sc, *, tq, tk):
    qi = pl.program_id(2)
    q = q_ref[...]
    m_sc[...] = jnp.full(m_sc.shape, NEG_INF, F32)
    l_sc[...] = jnp.zeros(l_sc.shape, F32)
    acc_sc[...] = jnp.zeros(acc_sc.shape, F32)

    def step(kstart, masked):
        k = k_ref[pl.ds(kstart, tk), :]
        v = v_ref[pl.ds(kstart, tk), :]
        s = _dot_nt(q, k)
        if masked:
            row = qi * tq + lax.broadcasted_iota(jnp.int32, (tq, tk), 0)
            col = kstart + lax.broadcasted_iota(jnp.int32, (tq, tk), 1)
            s = jnp.where(col <= row, s, NEG_INF)
        m_prev = m_sc[...]
        m_new = jnp.maximum(m_prev, jnp.max(s, axis=-1, keepdims=True))
        alpha = jnp.exp(m_prev - m_new)
        p = jnp.exp(s - m_new)
        l_sc[...] = alpha * l_sc[...] + jnp.sum(p, axis=-1, keepdims=True)
        acc_sc[...] = alpha * acc_sc[...] + _dot(p.astype(BF16), v)
        m_sc[...] = m_new

    def body(i, carry):
        step(pl.multiple_of(i * tk, tk), False)
        return carry

    lax.fori_loop(0, qi * (tq // tk), body, 0)
    for j in range(tq // tk):
        step(pl.multiple_of(qi * tq + j * tk, tk), True)
    o_ref[...] = (acc_sc[...] / l_sc[...]).astype(o_ref.dtype)


def _mla_flash(q, k, v):
    bsz, hh, s, _ = q.shape
    tq, tk = 512, 512
    return pl.pallas_call(
        functools.partial(_mla_flash_kernel, tq=tq, tk=tk),
        grid=(bsz, hh, s // tq),
        in_specs=[
            pl.BlockSpec((None, None, tq, QK_PAD), lambda b, h, i: (b, h, i, 0)),
            pl.BlockSpec((None, None, s, QK_PAD), lambda b, h, i: (b, h, 0, 0)),
            pl.BlockSpec((None, None, s, V_DIM), lambda b, h, i: (b, h, 0, 0)),
        ],
        out_specs=pl.BlockSpec((None, tq, V_DIM), lambda b, h, i: (b, i, h)),
        out_shape=jax.ShapeDtypeStruct((bsz, s, hh * V_DIM), BF16),
        scratch_shapes=[pltpu.VMEM((tq, 1), F32), pltpu.VMEM((tq, 1), F32),
                        pltpu.VMEM((tq, V_DIM), F32)],
        compiler_params=_params("parallel", "parallel", "arbitrary"),
        name="mla_flash",
    )(q, k, v)


def _post_mlp_kernel(x_ref, a_ref, mod_ref, wo_ref, bo_ref, g_ref, w1_ref, w2_ref, gf_ref,
                     out_ref, x1_sc, h_sc, acc_sc, *, final_norm):
    j = pl.program_id(2)
    d = x_ref.shape[-1]

    @pl.when(j == 0)
    def _():
        mod = mod_ref[...]
        gt1, sh2, sc2 = mod[:, 2 * d:3 * d], mod[:, 3 * d:4 * d], mod[:, 4 * d:5 * d]
        y = _dot(a_ref[...], wo_ref[...]) + bo_ref[...]
        x1 = x_ref[...] + gt1 * y
        x1_sc[...] = x1
        h_sc[...] = (_rmsnorm(x1, g_ref[...]) * (1.0 + sc2) + sh2).astype(BF16)
        acc_sc[...] = jnp.zeros(acc_sc.shape, F32)

    u = jnp.square(jnp.maximum(_dot(h_sc[...], w1_ref[...]), 0.0))
    acc_sc[...] += _dot(u.astype(BF16), w2_ref[...])

    @pl.when(j == pl.num_programs(2) - 1)
    def _():
        gt2 = mod_ref[...][:, 5 * d:6 * d]
        x2 = x1_sc[...] + gt2 * acc_sc[...]
        if final_norm:
            x2 = _rmsnorm(x2, gf_ref[...])
        out_ref[...] = x2


def _post_mlp(x, attn, mods, layer, w_o, b_o, g_mlp, w_ff1, w_ff2, g_final, final_norm):
    bsz, s, d = x.shape
    dff = w_ff1.shape[1]
    tm, tf = 512, 1024
    const = lambda b, i, j: (0, 0)
    tok = lambda b, i, j: (b, i, 0)
    return pl.pallas_call(
        functools.partial(_post_mlp_kernel, final_norm=final_norm),
        grid=(bsz, s // tm, dff // tf),
        in_specs=[
            pl.BlockSpec((None, tm, d), tok),
            pl.BlockSpec((None, tm, attn.shape[-1]), tok),
            pl.BlockSpec((None, None, 1, mods.shape[-1]), lambda b, i, j: (layer, b, 0, 0)),
            pl.BlockSpec(w_o.shape, const),
            pl.BlockSpec((1, d), const),
            pl.BlockSpec((1, d), const),
            pl.BlockSpec((d, tf), lambda b, i, j: (0, j)),
            pl.BlockSpec((tf, d), lambda b, i, j: (j, 0)),
            pl.BlockSpec((1, d), const),
        ],
        out_specs=pl.BlockSpec((None, tm, d), tok),
        out_shape=jax.ShapeDtypeStruct((bsz, s, d), F32),
        scratch_shapes=[pltpu.VMEM((tm, d), F32), pltpu.VMEM((tm, d), BF16),
                        pltpu.VMEM((tm, d), F32)],
        compiler_params=_params("parallel", "parallel", "arbitrary"),
        name="post_mlp_final" if final_norm else "post_mlp",
    )(x, attn, mods, w_o.astype(BF16), b_o.reshape(1, d), g_mlp.reshape(1, d),
      w_ff1.astype(BF16), w_ff2.astype(BF16), g_final.reshape(1, d))


def _swa_proj_kernel(x_ref, mod_ref, g_ref, w_ref, b_ref, o_ref, *, scale):
    d = x_ref.shape[-1]
    mod = mod_ref[...]
    sh1, sc1 = mod[:, 0:d], mod[:, d:2 * d]
    h = (_rmsnorm(x_ref[...], g_ref[...]) * (1.0 + sc1) + sh1).astype(BF16)
    qkv = _dot(h, w_ref[...]) + b_ref[...]
    nq = SWA_HEADS * SWA_HEAD_DIM
    o_ref[:, 0:nq] = (qkv[:, 0:nq] * scale).astype(BF16)
    o_ref[:, nq:] = qkv[:, nq:].astype(BF16)


def _swa_proj(x, mods, layer, g, w_qkv, b_qkv):
    bsz, s, d = x.shape
    n = w_qkv.shape[1]
    tm = 512
    const = lambda b, i: (0, 0)
    return pl.pallas_call(
        functools.partial(_swa_proj_kernel, scale=float(SWA_HEAD_DIM ** -0.5)),
        grid=(bsz, s // tm),
        in_specs=[
            pl.BlockSpec((None, tm, d), lambda b, i: (b, i, 0)),
            pl.BlockSpec((None, None, 1, mods.shape[-1]), lambda b, i: (layer, b, 0, 0)),
            pl.BlockSpec((1, d), const),
            pl.BlockSpec(w_qkv.shape, const),
            pl.BlockSpec((1, n), const),
        ],
        out_specs=pl.BlockSpec((None, tm, n), lambda b, i: (b, i, 0)),
        out_shape=jax.ShapeDtypeStruct((bsz, s, n), BF16),
        compiler_params=_params("parallel", "parallel"),
        name="swa_proj",
    )(x, mods, g.reshape(1, d), w_qkv.astype(BF16), b_qkv.reshape(1, n))


def _swa_attn_kernel(sink_ref, q_ref, kc_ref, vc_ref, kp_ref, vp_ref, o_ref, *, slopes):
    i = pl.program_id(1)
    w = WINDOW
    dh = SWA_HEAD_DIM
    group = SWA_HEADS // SWA_KV_HEADS
    qpos = lax.broadcasted_iota(jnp.int32, (w, 2 * w), 0)
    kpos = lax.broadcasted_iota(jnp.int32, (w, 2 * w), 1)
    dist = w + qpos - kpos
    in_window = (dist >= 0) & (dist < w)
    dist_f = dist.astype(F32)
    for n in range(q_ref.shape[0] // w):
        rows = slice(n * w, (n + 1) * w)
        if n == 0:
            kband = jnp.concatenate([kp_ref[...], kc_ref[rows, :]], axis=0)
            vband = jnp.concatenate([vp_ref[...], vc_ref[rows, :]], axis=0)
            first_real = jnp.where(i > 0, 0, w)
            mask = in_window & (kpos >= first_real)
        else:
            kband = kc_ref[(n - 1) * w:(n + 1) * w, :]
            vband = vc_ref[(n - 1) * w:(n + 1) * w, :]
            mask = in_window
        qb = q_ref[rows, :]
        outs = []
        for hd in range(SWA_HEADS):
            kh = hd // group
            s = _dot_nt(qb[:, hd * dh:(hd + 1) * dh], kband[:, kh * dh:(kh + 1) * dh])
            s = jnp.where(mask, s - slopes[hd] * dist_f, NEG_INF)
            sink = sink_ref[hd]
            m = jnp.maximum(jnp.max(s, axis=-1, keepdims=True), sink)
            p = jnp.exp(s - m)
            denom = jnp.sum(p, axis=-1, keepdims=True) + jnp.exp(sink - m)
            o = _dot(p.astype(BF16), vband[:, kh * dh:(kh + 1) * dh])
            outs.append(o / denom)
        o_ref[rows, :] = jnp.concatenate(outs, axis=-1).astype(o_ref.dtype)


def _swa_attn(qkv, sinks):
    bsz, s, _ = qkv.shape
    nq = SWA_HEADS * SWA_HEAD_DIM
    nkv = SWA_KV_HEADS * SWA_HEAD_DIM
    tm = 512
    per = tm // WINDOW
    slopes = tuple(float(v) for v in 2.0 ** (-8.0 * np.arange(1, SWA_HEADS + 1) / SWA_HEADS))
    kcol, vcol = nq // nkv, nq // nkv + 1
    prev = lambda b, i: jnp.maximum(i * per - 1, 0)
    return pl.pallas_call(
        functools.partial(_swa_attn_kernel, slopes=slopes),
        grid=(bsz, s // tm),
        in_specs=[
            pl.BlockSpec(memory_space=pltpu.SMEM),
            pl.BlockSpec((None, tm, nq), lambda b, i: (b, i, 0)),
            pl.BlockSpec((None, tm, nkv), lambda b, i: (b, i, kcol)),
            pl.BlockSpec((None, tm, nkv), lambda b, i: (b, i, vcol)),
            pl.BlockSpec((None, WINDOW, nkv), lambda b, i: (b, prev(b, i), kcol)),
            pl.BlockSpec((None, WINDOW, nkv), lambda b, i: (b, prev(b, i), vcol)),
        ],
        out_specs=pl.BlockSpec((None, tm, nq), lambda b, i: (b, i, 0)),
        out_shape=jax.ShapeDtypeStruct((bsz, s, nq), BF16),
        compiler_params=_params("parallel", "parallel"),
        name="swa_attn",
    )(sinks.astype(F32), qkv, qkv, qkv, qkv, qkv)


def kernel(x, c, positions, w_ada, b_ada, g_mix, g_mlp, mla_w_dq, mla_g_q, mla_w_uq, mla_w_dkv,
           mla_g_kv, mla_w_ukv, mla_w_o, swa_w_qkv, swa_b_qkv, swa_sinks, swa_w_o, swa_b_o,
           w_ff1, w_ff2, g_final):
    depth = w_ada.shape[0]
    bsz, s, d = x.shape
    mods = _adaln(c, w_ada, b_ada).reshape(depth, bsz, 1, w_ada.shape[-1])
    cos_t, sin_t = _rope_tables(positions)
    zero_bias = jnp.zeros((d,), F32)
    for i in range(depth):
        j = i // 2
        last = i == depth - 1
        if i % 2 == 0:
            q, k, v = _mla_proj(x, mods, i, g_mix[i], cos_t, sin_t, mla_w_dq[j], mla_g_q[j],
                                mla_w_uq[j], mla_w_dkv[j], mla_g_kv[j], mla_w_ukv[j])
            attn = _mla_flash(q, k, v)
            w_o, b_o = mla_w_o[j], zero_bias
        else:
            qkv = _swa_proj(x, mods, i, g_mix[i], swa_w_qkv[j], swa_b_qkv[j])
            attn = _swa_attn(qkv, swa_sinks[j])
            w_o, b_o = swa_w_o[j], swa_b_o[j]
        x = _post_mlp(x, attn, mods, i, w_o, b_o, g_mlp[i], w_ff1[i], w_ff2[i], g_final, last)
    return x
```

```python
import functools

import numpy as np
import jax
import jax.numpy as jnp
from jax import lax
from jax.experimental import pallas as pl
from jax.experimental.pallas import tpu as pltpu

F32 = jnp.float32
BF16 = jnp.bfloat16

EPS = 1e-6
ROPE_THETA = 10000.0
MLA_HEADS = 8
QK_NOPE = 128
QK_ROPE = 64
V_DIM = 128
KV_LORA = 256
SWA_HEADS = 16
SWA_KV_HEADS = 4
SWA_HEAD_DIM = 64
WINDOW = 128

LANES = 128
QK_PAD = 2 * LANES
VMEM_LIMIT = 56 * 1024 * 1024

NEG_INF = float("-inf")


def _params(*sem):
    return pltpu.CompilerParams(dimension_semantics=sem, vmem_limit_bytes=VMEM_LIMIT)


def _rmsnorm(x, g):
    y = x * lax.rsqrt(jnp.mean(x * x, axis=-1, keepdims=True) + EPS)
    return y * g


def _dot(a, b):
    return jnp.dot(a, b, preferred_element_type=F32)


def _dot_nt(a, b):
    return lax.dot_general(a, b, (((1,), (1,)), ((), ())), preferred_element_type=F32)


def _adaln_kernel(ct_ref, w_ref, b_ref, o_ref):
    ct = ct_ref[...]
    cond = ct * jax.nn.sigmoid(ct)
    w = w_ref[...]
    rows = [jnp.sum(w * cond[:, b:b + 1], axis=0, keepdims=True)
            for b in range(ct.shape[1])]
    o_ref[...] = jnp.concatenate(rows, axis=0) + b_ref[...]


def _adaln(c, w_ada, b_ada):
    depth, d, n = w_ada.shape
    bsz = c.shape[0]
    tn = 1024
    return pl.pallas_call(
        _adaln_kernel,
        grid=(depth, n // tn),
        in_specs=[
            pl.BlockSpec((d, bsz), lambda l, j: (0, 0)),
            pl.BlockSpec((None, d, tn), lambda l, j: (l, 0, j)),
            pl.BlockSpec((None, 1, tn), lambda l, j: (l, 0, j)),
        ],
        out_specs=pl.BlockSpec((None, bsz, tn), lambda l, j: (l, 0, j)),
        out_shape=jax.ShapeDtypeStruct((depth, bsz, n), F32),
        compiler_params=_params("parallel", "parallel"),
        name="adaln",
    )(c.T, w_ada, b_ada.reshape(depth, 1, n))


def _rope_table_kernel(pos_ref, inv_ref, cos_ref, sin_ref):
    ang = pos_ref[...].astype(F32) * inv_ref[...]
    cos_ref[...] = jnp.cos(ang)
    sin_ref[...] = jnp.sin(ang)


def _rope_tables(positions):
    n = positions.size
    half = QK_ROPE // 2
    inv_freq = ROPE_THETA ** (-jnp.arange(half, dtype=F32) / half)
    inv = jnp.tile(inv_freq, LANES // half).reshape(1, LANES)
    tm = 2048
    return pl.pallas_call(
        _rope_table_kernel,
        grid=(n // tm,),
        in_specs=[pl.BlockSpec((tm, 1), lambda i: (i, 0)),
                  pl.BlockSpec((1, LANES), lambda i: (0, 0))],
        out_specs=[pl.BlockSpec((tm, LANES), lambda i: (i, 0)),
                   pl.BlockSpec((tm, LANES), lambda i: (i, 0))],
        out_shape=[jax.ShapeDtypeStruct((n, LANES), F32)] * 2,
        compiler_params=_params("parallel"),
        name="rope_table",
    )(positions.reshape(n, 1), inv)


def _mla_proj_kernel(x_ref, mod_ref, g_ref, cos_ref, sin_ref, wdq_ref, gq_ref, wuq_ref,
                     wdkv_ref, gkv_ref, wukv_ref, q_ref, k_ref, v_ref, *, scale):
    d = x_ref.shape[-1]
    mod = mod_ref[...]
    sh1, sc1 = mod[:, 0:d], mod[:, d:2 * d]
    h = (_rmsnorm(x_ref[...], g_ref[...]) * (1.0 + sc1) + sh1).astype(BF16)
    cos, sin = cos_ref[...], sin_ref[...]
    lane = lax.broadcasted_iota(jnp.int32, cos.shape, 1)
    low = lane < QK_ROPE

    cq = _rmsnorm(_dot(h, wdq_ref[...]), gq_ref[...]).astype(BF16)
    qe = _dot(cq, wuq_ref[...])
    n_nope = MLA_HEADS * QK_NOPE
    n_rope = MLA_HEADS * QK_ROPE
    for hd in range(MLA_HEADS):
        q_ref[hd, :, 0:LANES] = (qe[:, hd * LANES:(hd + 1) * LANES] * scale).astype(BF16)
        j = hd // 2
        a = qe[:, n_nope + j * LANES:n_nope + (j + 1) * LANES]
        b = qe[:, n_nope + n_rope + j * LANES:n_nope + n_rope + (j + 1) * LANES]
        r = (a * cos + b * sin) * scale
        keep = low if hd % 2 == 0 else jnp.logical_not(low)
        q_ref[hd, :, LANES:QK_PAD] = jnp.where(keep, r, 0.0).astype(BF16)

    dkv = _dot(h, wdkv_ref[...])
    ckv = _rmsnorm(dkv[:, 0:KV_LORA], gkv_ref[...]).astype(BF16)
    kr = dkv[:, KV_LORA:KV_LORA + LANES] * cos + dkv[:, KV_LORA + LANES:KV_LORA + 2 * LANES] * sin
    kr_even = jnp.where(low, kr, 0.0).astype(BF16)
    kr_odd = jnp.where(low, 0.0, kr).astype(BF16)
    kv = _dot(ckv, wukv_ref[...])
    for hd in range(MLA_HEADS):
        base = hd * (QK_NOPE + V_DIM)
        k_ref[hd, :, 0:LANES] = kv[:, base:base + QK_NOPE].astype(BF16)
        k_ref[hd, :, LANES:QK_PAD] = kr_even if hd % 2 == 0 else kr_odd
        v_ref[hd] = kv[:, base + QK_NOPE:base + QK_NOPE + V_DIM].astype(BF16)


def _rot_half(w):
    half = w.shape[-1] // 2
    return jnp.concatenate([-w[..., half:], w[..., :half]], axis=-1)


def _mla_proj(x, mods, layer, g, cos_t, sin_t, w_dq, g_q, w_uq, w_dkv, g_kv, w_ukv):
    bsz, s, d = x.shape
    hh = MLA_HEADS
    q_lora = w_dq.shape[1]
    tm = 512
    wq3 = w_uq.reshape(q_lora, hh, QK_NOPE + QK_ROPE)
    wq_rope = wq3[:, :, QK_NOPE:]
    wuq_ext = jnp.concatenate([
        wq3[:, :, :QK_NOPE].reshape(q_lora, hh * QK_NOPE),
        wq_rope.reshape(q_lora, hh * QK_ROPE),
        _rot_half(wq_rope).reshape(q_lora, hh * QK_ROPE)], axis=1).astype(BF16)
    wkr = w_dkv[:, KV_LORA:]
    wkr_rot = _rot_half(wkr)
    wdkv_ext = jnp.concatenate([w_dkv[:, :KV_LORA], wkr, wkr, wkr_rot, wkr_rot], axis=1).astype(BF16)
    scale = float((QK_NOPE + QK_ROPE) ** -0.5)

    const = lambda b, i: (0, 0)
    tok = lambda b, i: (b * (s // tm) + i, 0)
    out_map = lambda b, i: (b, 0, i, 0)
    return pl.pallas_call(
        functools.partial(_mla_proj_kernel, scale=scale),
        grid=(bsz, s // tm),
        in_specs=[
            pl.BlockSpec((None, tm, d), lambda b, i: (b, i, 0)),
            pl.BlockSpec((None, None, 1, mods.shape[-1]), lambda b, i: (layer, b, 0, 0)),
            pl.BlockSpec((1, d), const),
            pl.BlockSpec((tm, LANES), tok),
            pl.BlockSpec((tm, LANES), tok),
            pl.BlockSpec(w_dq.shape, const),
            pl.BlockSpec((1, q_lora), const),
            pl.BlockSpec(wuq_ext.shape, const),
            pl.BlockSpec(wdkv_ext.shape, const),
            pl.BlockSpec((1, KV_LORA), const),
            pl.BlockSpec(w_ukv.shape, const),
        ],
        out_specs=[
            pl.BlockSpec((None, hh, tm, QK_PAD), out_map),
            pl.BlockSpec((None, hh, tm, QK_PAD), out_map),
            pl.BlockSpec((None, hh, tm, V_DIM), out_map),
        ],
        out_shape=[
            jax.ShapeDtypeStruct((bsz, hh, s, QK_PAD), BF16),
            jax.ShapeDtypeStruct((bsz, hh, s, QK_PAD), BF16),
            jax.ShapeDtypeStruct((bsz, hh, s, V_DIM), BF16),
        ],
        compiler_params=_params("parallel", "parallel"),
        name="mla_proj",
    )(x, mods, g.reshape(1, d), cos_t, sin_t, w_dq.astype(BF16), g_q.reshape(1, q_lora),
      wuq_ext, wdkv_ext, g_kv.reshape(1, KV_LORA), w_ukv.astype(BF16))


def _mla_flash_kernel(q_ref, k_ref, v_ref, o_ref, vext_sc, s0_sc, s1_sc, m_sc, acc_sc, *, tq):
    qi = pl.program_id(2)
    n_rep_s = tq // LANES

    @pl.when(qi == 0)
    def _():
        vext_sc[:, 0:V_DIM] = v_ref[...]
        vext_sc[:, V_DIM:2 * V_DIM] = jnp.ones((vext_sc.shape[0], V_DIM), BF16)

    m_sc[...] = jnp.full(m_sc.shape, NEG_INF, F32)
    acc_sc[...] = jnp.zeros(acc_sc.shape, F32)

    def scores(t, s_ref):
        k = k_ref[pl.ds(pl.multiple_of(t * tq, tq), tq), :]
        s_ref[...] = _dot_nt(q_ref[...], k)

    def update(t, s_ref, masked):
        s = s_ref[...]
        if masked:
            row = lax.broadcasted_iota(jnp.int32, (tq, tq), 0)
            col = lax.broadcasted_iota(jnp.int32, (tq, tq), 1)
            s = jnp.where(col <= row, s, NEG_INF)
        m_prev = m_sc[...]
        m_new = jnp.maximum(m_prev, jnp.max(s, axis=-1, keepdims=True))
        alpha = jnp.exp(m_prev - m_new)
        p = jnp.exp(s - jnp.tile(m_new, (1, n_rep_s))).astype(BF16)
        v = vext_sc[pl.ds(pl.multiple_of(t * tq, tq), tq), :]
        acc_sc[...] = jnp.tile(alpha, (1, 2)) * acc_sc[...] + _dot(p, v)
        m_sc[...] = m_new

    scores(0, s0_sc)

    def pair(u, carry):
        t = 2 * u
        scores(t + 1, s1_sc)
        update(t, s0_sc, False)
        scores(t + 2, s0_sc)
        update(t + 1, s1_sc, False)
        return carry

    lax.fori_loop(0, qi // 2, pair, 0)

    @pl.when(qi % 2 == 1)
    def _():
        scores(qi, s1_sc)
        update(qi - 1, s0_sc, False)
        update(qi, s1_sc, True)

    @pl.when(qi % 2 == 0)
    def _():
        update(qi, s0_sc, True)

    acc = acc_sc[...]
    o_ref[...] = (acc[:, 0:V_DIM] / acc[:, V_DIM:2 * V_DIM]).astype(o_ref.dtype)


def _mla_flash(q, k, v):
    bsz, hh, s, _ = q.shape
    tq = 512
    return pl.pallas_call(
        functools.partial(_mla_flash_kernel, tq=tq),
        grid=(bsz, hh, s // tq),
        in_specs=[
            pl.BlockSpec((None, None, tq, QK_PAD), lambda b, h, i: (b, h, i, 0)),
            pl.BlockSpec((None, None, s, QK_PAD), lambda b, h, i: (b, h, 0, 0)),
            pl.BlockSpec((None, None, s, V_DIM), lambda b, h, i: (b, h, 0, 0)),
        ],
        out_specs=pl.BlockSpec((None, tq, V_DIM), lambda b, h, i: (b, i, h)),
        out_shape=jax.ShapeDtypeStruct((bsz, s, hh * V_DIM), BF16),
        scratch_shapes=[pltpu.VMEM((s, 2 * V_DIM), BF16),
                        pltpu.VMEM((tq, tq), F32), pltpu.VMEM((tq, tq), F32),
                        pltpu.VMEM((tq, LANES), F32), pltpu.VMEM((tq, 2 * V_DIM), F32)],
        compiler_params=_params("parallel", "parallel", "arbitrary"),
        name="mla_flash",
    )(q, k, v)


def _post_mlp_kernel(x_ref, a_ref, mod_ref, wo_ref, bo_ref, g_ref, w1_ref, w2_ref, gf_ref,
                     out_ref, x1_sc, h_sc, acc_sc, *, final_norm, n_ff, row_chunk):
    j = pl.program_id(2)
    d = x_ref.shape[-1]
    tm = x_ref.shape[0]

    def step(first, last):
        mod = mod_ref[...]
        gt1, sh2, sc2 = mod[:, 2 * d:3 * d], mod[:, 3 * d:4 * d], mod[:, 4 * d:5 * d]
        gt2 = mod[:, 5 * d:6 * d]
        for r in range(tm // row_chunk):
            rows = slice(r * row_chunk, (r + 1) * row_chunk)
            if first:
                y = _dot(a_ref[rows, :], wo_ref[...]) + bo_ref[...]
                x1 = x_ref[rows, :] + gt1 * y
                h = (_rmsnorm(x1, g_ref[...]) * (1.0 + sc2) + sh2).astype(BF16)
                h_sc[rows, :] = h
                if not last:
                    x1_sc[rows, :] = x1
            else:
                h = h_sc[rows, :]
            u = jnp.square(jnp.maximum(_dot(h, w1_ref[...]), 0.0)).astype(BF16)
            acc = _dot(u, w2_ref[...])
            if not first:
                acc = acc_sc[rows, :] + acc
            if last:
                x2 = (x1 if first else x1_sc[rows, :]) + gt2 * acc
                if final_norm:
                    x2 = _rmsnorm(x2, gf_ref[...])
                out_ref[rows, :] = x2
            else:
                acc_sc[rows, :] = acc

    pl.when(j == 0)(functools.partial(step, True, n_ff == 1))
    if n_ff > 1:
        pl.when(j == n_ff - 1)(functools.partial(step, False, True))
    if n_ff > 2:
        pl.when(jnp.logical_and(j > 0, j < n_ff - 1))(functools.partial(step, False, False))


def _post_mlp(x, attn, mods, layer, w_o, b_o, g_mlp, w_ff1, w_ff2, g_final, final_norm):
    bsz, s, d = x.shape
    dff = w_ff1.shape[1]
    tm, tf, row_chunk = 1024, 1024, 256
    const = lambda b, i, j: (0, 0)
    tok = lambda b, i, j: (b, i, 0)
    return pl.pallas_call(
        functools.partial(_post_mlp_kernel, final_norm=final_norm, n_ff=dff // tf,
                          row_chunk=row_chunk),
        grid=(bsz, s // tm, dff // tf),
        in_specs=[
            pl.BlockSpec((None, tm, d), tok),
            pl.BlockSpec((None, tm, attn.shape[-1]), tok),
            pl.BlockSpec((None, None, 1, mods.shape[-1]), lambda b, i, j: (layer, b, 0, 0)),
            pl.BlockSpec(w_o.shape, const),
            pl.BlockSpec((1, d), const),
            pl.BlockSpec((1, d), const),
            pl.BlockSpec((d, tf), lambda b, i, j: (0, j)),
            pl.BlockSpec((tf, d), lambda b, i, j: (j, 0)),
            pl.BlockSpec((1, d), const),
        ],
        out_specs=pl.BlockSpec((None, tm, d), tok),
        out_shape=jax.ShapeDtypeStruct((bsz, s, d), F32),
        scratch_shapes=[pltpu.VMEM((tm, d), F32), pltpu.VMEM((tm, d), BF16),
                        pltpu.VMEM((tm, d), F32)],
        compiler_params=_params("parallel", "parallel", "arbitrary"),
        name="post_mlp_final" if final_norm else "post_mlp",
    )(x, attn, mods, w_o.astype(BF16), b_o.reshape(1, d), g_mlp.reshape(1, d),
      w_ff1.astype(BF16), w_ff2.astype(BF16), g_final.reshape(1, d))


def _swa_proj_kernel(x_ref, mod_ref, g_ref, w_ref, b_ref, o_ref, *, scale):
    d = x_ref.shape[-1]
    mod = mod_ref[...]
    sh1, sc1 = mod[:, 0:d], mod[:, d:2 * d]
    h = (_rmsnorm(x_ref[...], g_ref[...]) * (1.0 + sc1) + sh1).astype(BF16)
    qkv = _dot(h, w_ref[...]) + b_ref[...]
    nq = SWA_HEADS * SWA_HEAD_DIM
    o_ref[:, 0:nq] = (qkv[:, 0:nq] * scale).astype(BF16)
    o_ref[:, nq:] = qkv[:, nq:].astype(BF16)


def _swa_proj(x, mods, layer, g, w_qkv, b_qkv):
    bsz, s, d = x.shape
    n = w_qkv.shape[1]
    tm = 512
    const = lambda b, i: (0, 0)
    return pl.pallas_call(
        functools.partial(_swa_proj_kernel, scale=float(SWA_HEAD_DIM ** -0.5)),
        grid=(bsz, s // tm),
        in_specs=[
            pl.BlockSpec((None, tm, d), lambda b, i: (b, i, 0)),
            pl.BlockSpec((None, None, 1, mods.shape[-1]), lambda b, i: (layer, b, 0, 0)),
            pl.BlockSpec((1, d), const),
            pl.BlockSpec(w_qkv.shape, const),
            pl.BlockSpec((1, n), const),
        ],
        out_specs=pl.BlockSpec((None, tm, n), lambda b, i: (b, i, 0)),
        out_shape=jax.ShapeDtypeStruct((bsz, s, n), BF16),
        compiler_params=_params("parallel", "parallel"),
        name="swa_proj",
    )(x, mods, g.reshape(1, d), w_qkv.astype(BF16), b_qkv.reshape(1, n))


def _swa_attn_kernel(sink_ref, q_ref, kc_ref, vc_ref, kp_ref, vp_ref, o_ref, *, slopes):
    i = pl.program_id(1)
    w = WINDOW
    dh = SWA_HEAD_DIM
    group = SWA_HEADS // SWA_KV_HEADS
    gw = group * dh
    qpos1 = lax.broadcasted_iota(jnp.int32, (w, w), 0)
    kpos1 = lax.broadcasted_iota(jnp.int32, (w, w), 1)
    dist1 = jnp.where(kpos1 > qpos1, qpos1 - kpos1 + w, qpos1 - kpos1).astype(F32)
    qpos = lax.broadcasted_iota(jnp.int32, (group * w, w), 0) & (w - 1)
    kpos = lax.broadcasted_iota(jnp.int32, (group * w, w), 1)
    from_prev = kpos > qpos
    no_prev = jnp.where(i == 0, NEG_INF, 0.0)
    ones_cols = jnp.ones((2 * w, 2 * w - 2 * dh), BF16)
    zero_cols = jnp.zeros((2 * w, dh), BF16)
    for g in range(SWA_KV_HEADS):
        heads = range(g * group, (g + 1) * group)
        bias = jnp.concatenate([dist1 * slopes[hd] for hd in heads], axis=0)
        sink = jnp.concatenate([jnp.full((w, w), sink_ref[hd], F32) for hd in heads], axis=0)
        kv_cols = slice(g * dh, (g + 1) * dh)
        for n in range(q_ref.shape[0] // w):
            rows = slice(n * w, (n + 1) * w)
            if n == 0:
                kband = jnp.concatenate([kp_ref[:, kv_cols], kc_ref[rows, kv_cols]], axis=0)
                vband = jnp.concatenate([vp_ref[:, kv_cols], vc_ref[rows, kv_cols]], axis=0)
            else:
                kband = kc_ref[(n - 1) * w:(n + 1) * w, kv_cols]
                vband = vc_ref[(n - 1) * w:(n + 1) * w, kv_cols]
            vext = jnp.concatenate([vband, zero_cols, ones_cols], axis=1)
            qg = jnp.concatenate([q_ref[rows, hd * dh:(hd + 1) * dh] for hd in heads], axis=0)
            s2 = _dot_nt(qg, kband)
            s = jnp.where(from_prev, s2[:, 0:w], s2[:, w:2 * w]) - bias
            if n == 0:
                s = s + jnp.where(from_prev, no_prev, 0.0)
            m = jnp.maximum(jnp.max(s, axis=-1, keepdims=True), sink)
            p = jnp.exp(s - m)
            pcat = jnp.concatenate([jnp.where(from_prev, p, 0.0), jnp.where(from_prev, 0.0, p)],
                                   axis=1).astype(BF16)
            pv = _dot(pcat, vext)
            o = pv[:, 0:w] / (pv[:, w:2 * w] + jnp.exp(sink - m))
            o_g = jnp.concatenate([o[a * w:(a + 1) * w, 0:dh] for a in range(group)], axis=1)
            o_ref[rows, g * gw:(g + 1) * gw] = o_g.astype(o_ref.dtype)


def _swa_attn(qkv, sinks):
    bsz, s, _ = qkv.shape
    nq = SWA_HEADS * SWA_HEAD_DIM
    nkv = SWA_KV_HEADS * SWA_HEAD_DIM
    tm = 512
    per = tm // WINDOW
    slopes = tuple(float(v) for v in 2.0 ** (-8.0 * np.arange(1, SWA_HEADS + 1) / SWA_HEADS))
    kcol, vcol = nq // nkv, nq // nkv + 1
    prev = lambda b, i: jnp.maximum(i * per - 1, 0)
    return pl.pallas_call(
        functools.partial(_swa_attn_kernel, slopes=slopes),
        grid=(bsz, s // tm),
        in_specs=[
            pl.BlockSpec(memory_space=pltpu.SMEM),
            pl.BlockSpec((None, tm, nq), lambda b, i: (b, i, 0)),
            pl.BlockSpec((None, tm, nkv), lambda b, i: (b, i, kcol)),
            pl.BlockSpec((None, tm, nkv), lambda b, i: (b, i, vcol)),
            pl.BlockSpec((None, WINDOW, nkv), lambda b, i: (b, prev(b, i), kcol)),
            pl.BlockSpec((None, WINDOW, nkv), lambda b, i: (b, prev(b, i), vcol)),
        ],
        out_specs=pl.BlockSpec((None, tm, nq), lambda b, i: (b, i, 0)),
        out_shape=jax.ShapeDtypeStruct((bsz, s, nq), BF16),
        compiler_params=_params("parallel", "parallel"),
        name="swa_attn",
    )(sinks.astype(F32), qkv, qkv, qkv, qkv, qkv)


def kernel(x, c, positions, w_ada, b_ada, g_mix, g_mlp, mla_w_dq, mla_g_q, mla_w_uq, mla_w_dkv,
           mla_g_kv, mla_w_ukv, mla_w_o, swa_w_qkv, swa_b_qkv, swa_sinks, swa_w_o, swa_b_o,
           w_ff1, w_ff2, g_final):
    depth = w_ada.shape[0]
    bsz, s, d = x.shape
    mods = _adaln(c, w_ada, b_ada).reshape(depth, bsz, 1, w_ada.shape[-1])
    cos_t, sin_t = _rope_tables(positions)
    zero_bias = jnp.zeros((d,), F32)
    for i in range(depth):
        j = i // 2
        last = i == depth - 1
        if i % 2 == 0:
            q, k, v = _mla_proj(x, mods, i, g_mix[i], cos_t, sin_t, mla_w_dq[j], mla_g_q[j],
                                mla_w_uq[j], mla_w_dkv[j], mla_g_kv[j], mla_w_ukv[j])
            attn = _mla_flash(q, k, v)
            w_o, b_o = mla_w_o[j], zero_bias
        else:
            qkv = _swa_proj(x, mods, i, g_mix[i], swa_w_qkv[j], swa_b_qkv[j])
            attn = _swa_attn(qkv, swa_sinks[j])
            w_o, b_o = swa_w_o[j], swa_b_o[j]
        x = _post_mlp(x, attn, mods, i, w_o, b_o, g_mlp[i], w_ff1[i], w_ff2[i], g_final, last)
    return x
```

```python
import functools
import math

import numpy as np
import jax
import jax.numpy as jnp
from jax import lax
from jax.experimental import pallas as pl
from jax.experimental.pallas import tpu as pltpu

F32 = jnp.float32
BF16 = jnp.bfloat16

EPS = 1e-6
ROPE_THETA = 10000.0
MLA_HEADS = 8
QK_NOPE = 128
QK_ROPE = 64
V_DIM = 128
KV_LORA = 256
SWA_HEADS = 16
SWA_KV_HEADS = 4
SWA_HEAD_DIM = 64
WINDOW = 128

LANES = 128
QK_PAD = 2 * LANES
VMEM_LIMIT = 56 * 1024 * 1024

NEG_INF = float("-inf")


def _params(*sem):
    return pltpu.CompilerParams(dimension_semantics=sem, vmem_limit_bytes=VMEM_LIMIT)


def _rmsnorm(x, g):
    y = x * lax.rsqrt(jnp.mean(x * x, axis=-1, keepdims=True) + EPS)
    return y * g


def _dot(a, b):
    return jnp.dot(a, b, preferred_element_type=F32)


def _dot_nt(a, b):
    return lax.dot_general(a, b, (((1,), (1,)), ((), ())), preferred_element_type=F32)


def _adaln_kernel(ct_ref, w_ref, b_ref, o_ref):
    ct = ct_ref[...]
    cond = ct * jax.nn.sigmoid(ct)
    w = w_ref[...]
    rows = [jnp.sum(w * cond[:, b:b + 1], axis=0, keepdims=True)
            for b in range(ct.shape[1])]
    o_ref[...] = jnp.concatenate(rows, axis=0) + b_ref[...]


def _adaln(c, w_ada, b_ada):
    depth, d, n = w_ada.shape
    bsz = c.shape[0]
    tn = 1024
    return pl.pallas_call(
        _adaln_kernel,
        grid=(depth, n // tn),
        in_specs=[
            pl.BlockSpec((d, bsz), lambda l, j: (0, 0)),
            pl.BlockSpec((None, d, tn), lambda l, j: (l, 0, j)),
            pl.BlockSpec((None, 1, tn), lambda l, j: (l, 0, j)),
        ],
        out_specs=pl.BlockSpec((None, bsz, tn), lambda l, j: (l, 0, j)),
        out_shape=jax.ShapeDtypeStruct((depth, bsz, n), F32),
        compiler_params=_params("parallel", "parallel"),
        name="adaln",
    )(c.T, w_ada, b_ada.reshape(depth, 1, n))


def _rope_table_kernel(pos_ref, inv_ref, cos_ref, sin_ref):
    ang = pos_ref[...].astype(F32) * inv_ref[...]
    cos_ref[...] = jnp.cos(ang)
    sin_ref[...] = jnp.sin(ang)


def _rope_tables(positions):
    n = positions.size
    half = QK_ROPE // 2
    per_row = LANES // half
    inv_freq = ROPE_THETA ** (-jnp.arange(half, dtype=F32) / half)
    inv = jnp.tile(inv_freq, per_row).reshape(1, LANES)
    pos_rows = jnp.repeat(positions.reshape(n // per_row, per_row), half, axis=1)
    rows = n // per_row
    tm = 512
    cos_t, sin_t = pl.pallas_call(
        _rope_table_kernel,
        grid=(rows // tm,),
        in_specs=[pl.BlockSpec((tm, LANES), lambda i: (i, 0)),
                  pl.BlockSpec((1, LANES), lambda i: (0, 0))],
        out_specs=[pl.BlockSpec((tm, LANES), lambda i: (i, 0)),
                   pl.BlockSpec((tm, LANES), lambda i: (i, 0))],
        out_shape=[jax.ShapeDtypeStruct((rows, LANES), F32)] * 2,
        compiler_params=_params("parallel"),
        name="rope_table",
    )(pos_rows, inv)
    return cos_t.reshape(n, half), sin_t.reshape(n, half)


def _mla_proj_kernel(x_ref, mod_ref, g_ref, cos_ref, sin_ref, wdq_ref, gq_ref, wuq_ref,
                     wdkv_ref, gkv_ref, wukv_ref, q_ref, k_ref, v_ref, *, scale):
    d = x_ref.shape[-1]
    mod = mod_ref[...]
    sh1, sc1 = mod[:, 0:d], mod[:, d:2 * d]
    h = (_rmsnorm(x_ref[...], g_ref[...]) * (1.0 + sc1) + sh1).astype(BF16)
    reps = LANES // cos_ref.shape[-1]
    cos, sin = jnp.tile(cos_ref[...], (1, reps)), jnp.tile(sin_ref[...], (1, reps))
    lane = lax.broadcasted_iota(jnp.int32, cos.shape, 1)
    low = lane < QK_ROPE

    cq = _rmsnorm(_dot(h, wdq_ref[...]), gq_ref[...]).astype(BF16)
    qe = _dot(cq, wuq_ref[...])
    n_nope = MLA_HEADS * QK_NOPE
    n_rope = MLA_HEADS * QK_ROPE
    for hd in range(MLA_HEADS):
        q_ref[hd, :, 0:LANES] = (qe[:, hd * LANES:(hd + 1) * LANES] * scale).astype(BF16)
        j = hd // 2
        a = qe[:, n_nope + j * LANES:n_nope + (j + 1) * LANES]
        b = qe[:, n_nope + n_rope + j * LANES:n_nope + n_rope + (j + 1) * LANES]
        r = (a * cos + b * sin) * scale
        keep = low if hd % 2 == 0 else jnp.logical_not(low)
        q_ref[hd, :, LANES:QK_PAD] = jnp.where(keep, r, 0.0).astype(BF16)

    dkv = _dot(h, wdkv_ref[...])
    ckv = _rmsnorm(dkv[:, 0:KV_LORA], gkv_ref[...]).astype(BF16)
    kr = dkv[:, KV_LORA:KV_LORA + LANES] * cos + dkv[:, KV_LORA + LANES:KV_LORA + 2 * LANES] * sin
    kr_even = jnp.where(low, kr, 0.0).astype(BF16)
    kr_odd = jnp.where(low, 0.0, kr).astype(BF16)
    kv = _dot(ckv, wukv_ref[...])
    for hd in range(MLA_HEADS):
        base = hd * (QK_NOPE + V_DIM)
        k_ref[hd, :, 0:LANES] = kv[:, base:base + QK_NOPE].astype(BF16)
        k_ref[hd, :, LANES:QK_PAD] = kr_even if hd % 2 == 0 else kr_odd
        v_ref[hd] = kv[:, base + QK_NOPE:base + QK_NOPE + V_DIM].astype(BF16)


def _rot_half(w):
    half = w.shape[-1] // 2
    return jnp.concatenate([-w[..., half:], w[..., :half]], axis=-1)


def _mla_proj(x, mods, layer, g, cos_t, sin_t, w_dq, g_q, w_uq, w_dkv, g_kv, w_ukv):
    bsz, s, d = x.shape
    hh = MLA_HEADS
    q_lora = w_dq.shape[1]
    tm = 512
    wq3 = w_uq.reshape(q_lora, hh, QK_NOPE + QK_ROPE)
    wq_rope = wq3[:, :, QK_NOPE:]
    wuq_ext = jnp.concatenate([
        wq3[:, :, :QK_NOPE].reshape(q_lora, hh * QK_NOPE),
        wq_rope.reshape(q_lora, hh * QK_ROPE),
        _rot_half(wq_rope).reshape(q_lora, hh * QK_ROPE)], axis=1).astype(BF16)
    wkr = w_dkv[:, KV_LORA:]
    wkr_rot = _rot_half(wkr)
    wdkv_ext = jnp.concatenate([w_dkv[:, :KV_LORA], wkr, wkr, wkr_rot, wkr_rot], axis=1).astype(BF16)
    scale = float((QK_NOPE + QK_ROPE) ** -0.5 * math.log2(math.e))

    const = lambda b, i: (0, 0)
    tok = lambda b, i: (b * (s // tm) + i, 0)
    out_map = lambda b, i: (b, 0, i, 0)
    return pl.pallas_call(
        functools.partial(_mla_proj_kernel, scale=scale),
        grid=(bsz, s // tm),
        in_specs=[
            pl.BlockSpec((None, tm, d), lambda b, i: (b, i, 0)),
            pl.BlockSpec((None, None, 1, mods.shape[-1]), lambda b, i: (layer, b, 0, 0)),
            pl.BlockSpec((1, d), const),
            pl.BlockSpec((tm, cos_t.shape[-1]), tok),
            pl.BlockSpec((tm, sin_t.shape[-1]), tok),
            pl.BlockSpec(w_dq.shape, const),
            pl.BlockSpec((1, q_lora), const),
            pl.BlockSpec(wuq_ext.shape, const),
            pl.BlockSpec(wdkv_ext.shape, const),
            pl.BlockSpec((1, KV_LORA), const),
            pl.BlockSpec(w_ukv.shape, const),
        ],
        out_specs=[
            pl.BlockSpec((None, hh, tm, QK_PAD), out_map),
            pl.BlockSpec((None, hh, tm, QK_PAD), out_map),
            pl.BlockSpec((None, hh, tm, V_DIM), out_map),
        ],
        out_shape=[
            jax.ShapeDtypeStruct((bsz, hh, s, QK_PAD), BF16),
            jax.ShapeDtypeStruct((bsz, hh, s, QK_PAD), BF16),
            jax.ShapeDtypeStruct((bsz, hh, s, V_DIM), BF16),
        ],
        compiler_params=_params("parallel", "parallel"),
        name="mla_proj",
    )(x, mods, g.reshape(1, d), cos_t, sin_t, w_dq.astype(BF16), g_q.reshape(1, q_lora),
      wuq_ext, wdkv_ext, g_kv.reshape(1, KV_LORA), w_ukv.astype(BF16))


def _mla_flash_kernel(q_ref, k_ref, v_ref, o_ref, vext_sc, s0_sc, s1_sc, m_sc, acc_sc, *, tq):
    qi = pl.program_id(2)
    n_rep_s = tq // LANES

    @pl.when(qi == 0)
    def _():
        vext_sc[:, 0:V_DIM] = v_ref[...]
        vext_sc[:, V_DIM:2 * V_DIM] = jnp.ones((vext_sc.shape[0], V_DIM), BF16)

    m_sc[...] = jnp.full(m_sc.shape, NEG_INF, F32)
    acc_sc[...] = jnp.zeros(acc_sc.shape, F32)

    def scores(t, s_ref):
        k = k_ref[pl.ds(pl.multiple_of(t * tq, tq), tq), :]
        s_ref[...] = _dot_nt(q_ref[...], k)

    def update(t, s_ref, masked):
        s = s_ref[...]
        if masked:
            row = lax.broadcasted_iota(jnp.int32, (tq, tq), 0)
            col = lax.broadcasted_iota(jnp.int32, (tq, tq), 1)
            s = jnp.where(col <= row, s, NEG_INF)
        m_prev = m_sc[...]
        m_new = jnp.maximum(m_prev, jnp.max(s, axis=-1, keepdims=True))
        alpha = jnp.exp2(m_prev - m_new)
        p = jnp.exp2(s - jnp.tile(m_new, (1, n_rep_s))).astype(BF16)
        v = vext_sc[pl.ds(pl.multiple_of(t * tq, tq), tq), :]
        acc_sc[...] = jnp.tile(alpha, (1, 2)) * acc_sc[...] + _dot(p, v)
        m_sc[...] = m_new

    scores(0, s0_sc)

    def pair(u, carry):
        t = 2 * u
        scores(t + 1, s1_sc)
        update(t, s0_sc, False)
        scores(t + 2, s0_sc)
        update(t + 1, s1_sc, False)
        return carry

    lax.fori_loop(0, qi // 2, pair, 0)

    @pl.when(qi % 2 == 1)
    def _():
        scores(qi, s1_sc)
        update(qi - 1, s0_sc, False)
        update(qi, s1_sc, True)

    @pl.when(qi % 2 == 0)
    def _():
        update(qi, s0_sc, True)

    acc = acc_sc[...]
    o_ref[...] = (acc[:, 0:V_DIM] / acc[:, V_DIM:2 * V_DIM]).astype(o_ref.dtype)


def _mla_flash(q, k, v):
    bsz, hh, s, _ = q.shape
    tq = 1024
    return pl.pallas_call(
        functools.partial(_mla_flash_kernel, tq=tq),
        grid=(bsz, hh, s // tq),
        in_specs=[
            pl.BlockSpec((None, None, tq, QK_PAD), lambda b, h, i: (b, h, i, 0)),
            pl.BlockSpec((None, None, s, QK_PAD), lambda b, h, i: (b, h, 0, 0)),
            pl.BlockSpec((None, None, s, V_DIM), lambda b, h, i: (b, h, 0, 0)),
        ],
        out_specs=pl.BlockSpec((None, tq, V_DIM), lambda b, h, i: (b, i, h)),
        out_shape=jax.ShapeDtypeStruct((bsz, s, hh * V_DIM), BF16),
        scratch_shapes=[pltpu.VMEM((s, 2 * V_DIM), BF16),
                        pltpu.VMEM((tq, tq), F32), pltpu.VMEM((tq, tq), F32),
                        pltpu.VMEM((tq, LANES), F32), pltpu.VMEM((tq, 2 * V_DIM), F32)],
        compiler_params=_params("parallel", "parallel", "arbitrary"),
        name="mla_flash",
    )(q, k, v)


def _post_mlp_kernel(x_ref, a_ref, mod_ref, wo_ref, bo_ref, g_ref, w1_ref, w2_ref, gf_ref,
                     out_ref, x1_sc, h_sc, acc_sc, *, final_norm, n_ff, row_chunk):
    j = pl.program_id(2)
    d = x_ref.shape[-1]
    tm = x_ref.shape[0]

    def step(first, last):
        mod = mod_ref[...]
        gt1, sh2, sc2 = mod[:, 2 * d:3 * d], mod[:, 3 * d:4 * d], mod[:, 4 * d:5 * d]
        gt2 = mod[:, 5 * d:6 * d]
        for r in range(tm // row_chunk):
            rows = slice(r * row_chunk, (r + 1) * row_chunk)
            if first:
                y = _dot(a_ref[rows, :], wo_ref[...]) + bo_ref[...]
                x1 = x_ref[rows, :] + gt1 * y
                h = (_rmsnorm(x1, g_ref[...]) * (1.0 + sc2) + sh2).astype(BF16)
                h_sc[rows, :] = h
                if not last:
                    x1_sc[rows, :] = x1
            else:
                h = h_sc[rows, :]
            u = jnp.square(jnp.maximum(_dot(h, w1_ref[...]), 0.0)).astype(BF16)
            acc = _dot(u, w2_ref[...])
            if not first:
                acc = acc_sc[rows, :] + acc
            if last:
                x2 = (x1 if first else x1_sc[rows, :]) + gt2 * acc
                if final_norm:
                    x2 = _rmsnorm(x2, gf_ref[...])
                out_ref[rows, :] = x2
            else:
                acc_sc[rows, :] = acc

    pl.when(j == 0)(functools.partial(step, True, n_ff == 1))
    if n_ff > 1:
        pl.when(j == n_ff - 1)(functools.partial(step, False, True))
    if n_ff > 2:
        pl.when(jnp.logical_and(j > 0, j < n_ff - 1))(functools.partial(step, False, False))


def _post_mlp(x, attn, mods, layer, w_o, b_o, g_mlp, w_ff1, w_ff2, g_final, final_norm):
    bsz, s, d = x.shape
    dff = w_ff1.shape[1]
    tm, tf, row_chunk = 1024, 1024, 256
    const = lambda b, i, j: (0, 0)
    tok = lambda b, i, j: (b, i, 0)
    return pl.pallas_call(
        functools.partial(_post_mlp_kernel, final_norm=final_norm, n_ff=dff // tf,
                          row_chunk=row_chunk),
        grid=(bsz, s // tm, dff // tf),
        in_specs=[
            pl.BlockSpec((None, tm, d), tok),
            pl.BlockSpec((None, tm, attn.shape[-1]), tok),
            pl.BlockSpec((None, None, 1, mods.shape[-1]), lambda b, i, j: (layer, b, 0, 0)),
            pl.BlockSpec(w_o.shape, const),
            pl.BlockSpec((1, d), const),
            pl.BlockSpec((1, d), const),
            pl.BlockSpec((d, tf), lambda b, i, j: (0, j)),
            pl.BlockSpec((tf, d), lambda b, i, j: (j, 0)),
            pl.BlockSpec((1, d), const),
        ],
        out_specs=pl.BlockSpec((None, tm, d), tok),
        out_shape=jax.ShapeDtypeStruct((bsz, s, d), F32),
        scratch_shapes=[pltpu.VMEM((tm, d), F32), pltpu.VMEM((tm, d), BF16),
                        pltpu.VMEM((tm, d), F32)],
        compiler_params=_params("parallel", "parallel", "arbitrary"),
        name="post_mlp_final" if final_norm else "post_mlp",
    )(x, attn, mods, w_o.astype(BF16), b_o.reshape(1, d), g_mlp.reshape(1, d),
      w_ff1.astype(BF16), w_ff2.astype(BF16), g_final.reshape(1, d))


def _swa_proj_kernel(x_ref, mod_ref, g_ref, w_ref, b_ref, o_ref, *, scale):
    d = x_ref.shape[-1]
    mod = mod_ref[...]
    sh1, sc1 = mod[:, 0:d], mod[:, d:2 * d]
    h = (_rmsnorm(x_ref[...], g_ref[...]) * (1.0 + sc1) + sh1).astype(BF16)
    qkv = _dot(h, w_ref[...]) + b_ref[...]
    nq = SWA_HEADS * SWA_HEAD_DIM
    o_ref[:, 0:nq] = (qkv[:, 0:nq] * scale).astype(BF16)
    o_ref[:, nq:] = qkv[:, nq:].astype(BF16)


def _swa_proj(x, mods, layer, g, w_qkv, b_qkv):
    bsz, s, d = x.shape
    n = w_qkv.shape[1]
    tm = 512
    const = lambda b, i: (0, 0)
    return pl.pallas_call(
        functools.partial(_swa_proj_kernel, scale=float(SWA_HEAD_DIM ** -0.5)),
        grid=(bsz, s // tm),
        in_specs=[
            pl.BlockSpec((None, tm, d), lambda b, i: (b, i, 0)),
            pl.BlockSpec((None, None, 1, mods.shape[-1]), lambda b, i: (layer, b, 0, 0)),
            pl.BlockSpec((1, d), const),
            pl.BlockSpec(w_qkv.shape, const),
            pl.BlockSpec((1, n), const),
        ],
        out_specs=pl.BlockSpec((None, tm, n), lambda b, i: (b, i, 0)),
        out_shape=jax.ShapeDtypeStruct((bsz, s, n), BF16),
        compiler_params=_params("parallel", "parallel"),
        name="swa_proj",
    )(x, mods, g.reshape(1, d), w_qkv.astype(BF16), b_qkv.reshape(1, n))


def _swa_attn_kernel(sink_ref, q_ref, kc_ref, vc_ref, kp_ref, vp_ref, o_ref, *, slopes):
    i = pl.program_id(1)
    w = WINDOW
    dh = SWA_HEAD_DIM
    group = SWA_HEADS // SWA_KV_HEADS
    gw = group * dh
    qpos1 = lax.broadcasted_iota(jnp.int32, (w, w), 0)
    kpos1 = lax.broadcasted_iota(jnp.int32, (w, w), 1)
    dist1 = jnp.where(kpos1 > qpos1, qpos1 - kpos1 + w, qpos1 - kpos1).astype(F32)
    qpos = lax.broadcasted_iota(jnp.int32, (group * w, w), 0) & (w - 1)
    kpos = lax.broadcasted_iota(jnp.int32, (group * w, w), 1)
    from_prev = kpos > qpos
    no_prev = jnp.where(i == 0, NEG_INF, 0.0)
    ones_cols = jnp.ones((2 * w, 2 * w - 2 * dh), BF16)
    zero_cols = jnp.zeros((2 * w, dh), BF16)
    for g in range(SWA_KV_HEADS):
        heads = range(g * group, (g + 1) * group)
        bias = jnp.concatenate([dist1 * slopes[hd] for hd in heads], axis=0)
        sink = jnp.concatenate([jnp.full((w, w), sink_ref[hd], F32) for hd in heads], axis=0)
        kv_cols = slice(g * dh, (g + 1) * dh)
        for n in range(q_ref.shape[0] // w):
            rows = slice(n * w, (n + 1) * w)
            if n == 0:
                kband = jnp.concatenate([kp_ref[:, kv_cols], kc_ref[rows, kv_cols]], axis=0)
                vband = jnp.concatenate([vp_ref[:, kv_cols], vc_ref[rows, kv_cols]], axis=0)
            else:
                kband = kc_ref[(n - 1) * w:(n + 1) * w, kv_cols]
                vband = vc_ref[(n - 1) * w:(n + 1) * w, kv_cols]
            vext = jnp.concatenate([vband, zero_cols, ones_cols], axis=1)
            qg = jnp.concatenate([q_ref[rows, hd * dh:(hd + 1) * dh] for hd in heads], axis=0)
            s2 = _dot_nt(qg, kband)
            s = jnp.where(from_prev, s2[:, 0:w], s2[:, w:2 * w]) - bias
            if n == 0:
                s = s + jnp.where(from_prev, no_prev, 0.0)
            m = jnp.maximum(jnp.max(s, axis=-1, keepdims=True), sink)
            p = jnp.exp(s - m)
            pcat = jnp.concatenate([jnp.where(from_prev, p, 0.0), jnp.where(from_prev, 0.0, p)],
                                   axis=1).astype(BF16)
            pv = _dot(pcat, vext)
            o = pv[:, 0:w] / (pv[:, w:2 * w] + jnp.exp(sink - m))
            o_g = jnp.concatenate([o[a * w:(a + 1) * w, 0:dh] for a in range(group)], axis=1)
            o_ref[rows, g * gw:(g + 1) * gw] = o_g.astype(o_ref.dtype)


def _swa_attn(qkv, sinks):
    bsz, s, _ = qkv.shape
    nq = SWA_HEADS * SWA_HEAD_DIM
    nkv = SWA_KV_HEADS * SWA_HEAD_DIM
    tm = 512
    per = tm // WINDOW
    slopes = tuple(float(v) for v in 2.0 ** (-8.0 * np.arange(1, SWA_HEADS + 1) / SWA_HEADS))
    kcol, vcol = nq // nkv, nq // nkv + 1
    prev = lambda b, i: jnp.maximum(i * per - 1, 0)
    return pl.pallas_call(
        functools.partial(_swa_attn_kernel, slopes=slopes),
        grid=(bsz, s // tm),
        in_specs=[
            pl.BlockSpec(memory_space=pltpu.SMEM),
            pl.BlockSpec((None, tm, nq), lambda b, i: (b, i, 0)),
            pl.BlockSpec((None, tm, nkv), lambda b, i: (b, i, kcol)),
            pl.BlockSpec((None, tm, nkv), lambda b, i: (b, i, vcol)),
            pl.BlockSpec((None, WINDOW, nkv), lambda b, i: (b, prev(b, i), kcol)),
            pl.BlockSpec((None, WINDOW, nkv), lambda b, i: (b, prev(b, i), vcol)),
        ],
        out_specs=pl.BlockSpec((None, tm, nq), lambda b, i: (b, i, 0)),
        out_shape=jax.ShapeDtypeStruct((bsz, s, nq), BF16),
        compiler_params=_params("parallel", "parallel"),
        name="swa_attn",
    )(sinks.astype(F32), qkv, qkv, qkv, qkv, qkv)


def kernel(x, c, positions, w_ada, b_ada, g_mix, g_mlp, mla_w_dq, mla_g_q, mla_w_uq, mla_w_dkv,
           mla_g_kv, mla_w_ukv, mla_w_o, swa_w_qkv, swa_b_qkv, swa_sinks, swa_w_o, swa_b_o,
           w_ff1, w_ff2, g_final):
    depth = w_ada.shape[0]
    bsz, s, d = x.shape
    mods = _adaln(c, w_ada, b_ada).reshape(depth, bsz, 1, w_ada.shape[-1])
    cos_t, sin_t = _rope_tables(positions)
    zero_bias = jnp.zeros((d,), F32)
    for i in range(depth):
        j = i // 2
        last = i == depth - 1
        if i % 2 == 0:
            q, k, v = _mla_proj(x, mods, i, g_mix[i], cos_t, sin_t, mla_w_dq[j], mla_g_q[j],
                                mla_w_uq[j], mla_w_dkv[j], mla_g_kv[j], mla_w_ukv[j])
            attn = _mla_flash(q, k, v)
            w_o, b_o = mla_w_o[j], zero_bias
        else:
            qkv = _swa_proj(x, mods, i, g_mix[i], swa_w_qkv[j], swa_b_qkv[j])
            attn = _swa_attn(qkv, swa_sinks[j])
            w_o, b_o = swa_w_o[j], swa_b_o[j]
        x = _post_mlp(x, attn, mods, i, w_o, b_o, g_mlp[i], w_ff1[i], w_ff2[i], g_final, last)
    return x
```

```python
import functools
import math

import numpy as np
import jax
import jax.numpy as jnp
from jax import lax
from jax.experimental import pallas as pl
from jax.experimental.pallas import tpu as pltpu

F32 = jnp.float32
BF16 = jnp.bfloat16

EPS = 1e-6
ROPE_THETA = 10000.0
MLA_HEADS = 8
QK_NOPE = 128
QK_ROPE = 64
V_DIM = 128
KV_LORA = 256
SWA_HEADS = 16
SWA_KV_HEADS = 4
SWA_HEAD_DIM = 64
WINDOW = 128

LANES = 128
QK_PAD = 2 * LANES
VMEM_LIMIT = 56 * 1024 * 1024

NEG_INF = float("-inf")


def _params(*sem):
    return pltpu.CompilerParams(dimension_semantics=sem, vmem_limit_bytes=VMEM_LIMIT)


def _rmsnorm(x, g):
    y = x * lax.rsqrt(jnp.mean(x * x, axis=-1, keepdims=True) + EPS)
    return y * g


def _dot(a, b):
    return jnp.dot(a, b, preferred_element_type=F32)


def _dot_nt(a, b):
    return lax.dot_general(a, b, (((1,), (1,)), ((), ())), preferred_element_type=F32)


def _adaln_kernel(ct_ref, w_ref, b_ref, o_ref):
    ct = ct_ref[...]
    cond = ct * jax.nn.sigmoid(ct)
    w = w_ref[...]
    rows = [jnp.sum(w * cond[:, b:b + 1], axis=0, keepdims=True)
            for b in range(ct.shape[1])]
    o_ref[...] = jnp.concatenate(rows, axis=0) + b_ref[...]


def _adaln(c, w_ada, b_ada):
    depth, d, n = w_ada.shape
    bsz = c.shape[0]
    tn = 1024
    return pl.pallas_call(
        _adaln_kernel,
        grid=(depth, n // tn),
        in_specs=[
            pl.BlockSpec((d, bsz), lambda l, j: (0, 0)),
            pl.BlockSpec((None, d, tn), lambda l, j: (l, 0, j)),
            pl.BlockSpec((None, 1, tn), lambda l, j: (l, 0, j)),
        ],
        out_specs=pl.BlockSpec((None, bsz, tn), lambda l, j: (l, 0, j)),
        out_shape=jax.ShapeDtypeStruct((depth, bsz, n), F32),
        compiler_params=_params("parallel", "parallel"),
        name="adaln",
    )(c.T, w_ada, b_ada.reshape(depth, 1, n))


def _rope_table_kernel(pos_ref, inv_ref, cos_ref, sin_ref):
    ang = pos_ref[...].astype(F32) * inv_ref[...]
    cos_ref[...] = jnp.cos(ang)
    sin_ref[...] = jnp.sin(ang)


def _rope_tables(positions):
    n = positions.size
    half = QK_ROPE // 2
    per_row = LANES // half
    inv_freq = ROPE_THETA ** (-jnp.arange(half, dtype=F32) / half)
    inv = jnp.tile(inv_freq, per_row).reshape(1, LANES)
    pos_rows = jnp.repeat(positions.reshape(n // per_row, per_row), half, axis=1)
    rows = n // per_row
    tm = 512
    cos_t, sin_t = pl.pallas_call(
        _rope_table_kernel,
        grid=(rows // tm,),
        in_specs=[pl.BlockSpec((tm, LANES), lambda i: (i, 0)),
                  pl.BlockSpec((1, LANES), lambda i: (0, 0))],
        out_specs=[pl.BlockSpec((tm, LANES), lambda i: (i, 0)),
                   pl.BlockSpec((tm, LANES), lambda i: (i, 0))],
        out_shape=[jax.ShapeDtypeStruct((rows, LANES), F32)] * 2,
        compiler_params=_params("parallel"),
        name="rope_table",
    )(pos_rows, inv)
    return cos_t.reshape(n, half), sin_t.reshape(n, half)


def _mla_proj_kernel(x_ref, mod_ref, g_ref, cos_ref, sin_ref, wdq_ref, gq_ref, wuq_ref,
                     wdkv_ref, gkv_ref, wukv_ref, q_ref, k_ref, v_ref, *, scale, row_chunk):
    d = x_ref.shape[-1]
    mod = mod_ref[...]
    sh1, sc1 = mod[:, 0:d], mod[:, d:2 * d]
    reps = LANES // cos_ref.shape[-1]
    n_nope = MLA_HEADS * QK_NOPE
    n_rope = MLA_HEADS * QK_ROPE
    n_chunks = x_ref.shape[0] // row_chunk
    lane = lax.broadcasted_iota(jnp.int32, (row_chunk, LANES), 1)
    low = lane < QK_ROPE

    def down(c):
        rows = slice(c * row_chunk, (c + 1) * row_chunk)
        h = (_rmsnorm(x_ref[rows, :], g_ref[...]) * (1.0 + sc1) + sh1).astype(BF16)
        return _dot(h, wdq_ref[...]), _dot(h, wdkv_ref[...])

    def up(cq_raw, dkv):
        cq = _rmsnorm(cq_raw, gq_ref[...]).astype(BF16)
        ckv = _rmsnorm(dkv[:, 0:KV_LORA], gkv_ref[...]).astype(BF16)
        return _dot(cq, wuq_ref[...]), _dot(ckv, wukv_ref[...])

    def emit(c, dkv, qe, kv):
        rows = slice(c * row_chunk, (c + 1) * row_chunk)
        cos = jnp.tile(cos_ref[rows, :], (1, reps))
        sin = jnp.tile(sin_ref[rows, :], (1, reps))
        for hd in range(MLA_HEADS):
            q_ref[hd, rows, 0:LANES] = (qe[:, hd * LANES:(hd + 1) * LANES] * scale).astype(BF16)
            j = hd // 2
            a = qe[:, n_nope + j * LANES:n_nope + (j + 1) * LANES]
            b = qe[:, n_nope + n_rope + j * LANES:n_nope + n_rope + (j + 1) * LANES]
            r = (a * cos + b * sin) * scale
            keep = low if hd % 2 == 0 else jnp.logical_not(low)
            q_ref[hd, rows, LANES:QK_PAD] = jnp.where(keep, r, 0.0).astype(BF16)
        kr = (dkv[:, KV_LORA:KV_LORA + LANES] * cos
              + dkv[:, KV_LORA + LANES:KV_LORA + 2 * LANES] * sin)
        kr_even = jnp.where(low, kr, 0.0).astype(BF16)
        kr_odd = jnp.where(low, 0.0, kr).astype(BF16)
        for hd in range(MLA_HEADS):
            base = hd * (QK_NOPE + V_DIM)
            k_ref[hd, rows, 0:LANES] = kv[:, base:base + QK_NOPE].astype(BF16)
            k_ref[hd, rows, LANES:QK_PAD] = kr_even if hd % 2 == 0 else kr_odd
            v_ref[hd, rows, :] = kv[:, base + QK_NOPE:base + QK_NOPE + V_DIM].astype(BF16)

    downs, ups = {}, {}
    for step in range(n_chunks + 2):
        if step < n_chunks:
            downs[step] = down(step)
        if 1 <= step <= n_chunks:
            ups[step - 1] = up(*downs[step - 1])
        if step >= 2:
            emit(step - 2, downs.pop(step - 2)[1], *ups.pop(step - 2))


def _rot_half(w):
    half = w.shape[-1] // 2
    return jnp.concatenate([-w[..., half:], w[..., :half]], axis=-1)


def _mla_proj(x, mods, layer, g, cos_t, sin_t, w_dq, g_q, w_uq, w_dkv, g_kv, w_ukv):
    bsz, s, d = x.shape
    hh = MLA_HEADS
    q_lora = w_dq.shape[1]
    tm, row_chunk = 1024, 256
    wq3 = w_uq.reshape(q_lora, hh, QK_NOPE + QK_ROPE)
    wq_rope = wq3[:, :, QK_NOPE:]
    wuq_ext = jnp.concatenate([
        wq3[:, :, :QK_NOPE].reshape(q_lora, hh * QK_NOPE),
        wq_rope.reshape(q_lora, hh * QK_ROPE),
        _rot_half(wq_rope).reshape(q_lora, hh * QK_ROPE)], axis=1).astype(BF16)
    wkr = w_dkv[:, KV_LORA:]
    wkr_rot = _rot_half(wkr)
    wdkv_ext = jnp.concatenate([w_dkv[:, :KV_LORA], wkr, wkr, wkr_rot, wkr_rot], axis=1).astype(BF16)
    scale = float((QK_NOPE + QK_ROPE) ** -0.5 * math.log2(math.e))

    const = lambda b, i: (0, 0)
    tok = lambda b, i: (b * (s // tm) + i, 0)
    out_map = lambda b, i: (b, 0, i, 0)
    return pl.pallas_call(
        functools.partial(_mla_proj_kernel, scale=scale, row_chunk=row_chunk),
        grid=(bsz, s // tm),
        in_specs=[
            pl.BlockSpec((None, tm, d), lambda b, i: (b, i, 0)),
            pl.BlockSpec((None, None, 1, mods.shape[-1]), lambda b, i: (layer, b, 0, 0)),
            pl.BlockSpec((1, d), const),
            pl.BlockSpec((tm, cos_t.shape[-1]), tok),
            pl.BlockSpec((tm, sin_t.shape[-1]), tok),
            pl.BlockSpec(w_dq.shape, const),
            pl.BlockSpec((1, q_lora), const),
            pl.BlockSpec(wuq_ext.shape, const),
            pl.BlockSpec(wdkv_ext.shape, const),
            pl.BlockSpec((1, KV_LORA), const),
            pl.BlockSpec(w_ukv.shape, const),
        ],
        out_specs=[
            pl.BlockSpec((None, hh, tm, QK_PAD), out_map),
            pl.BlockSpec((None, hh, tm, QK_PAD), out_map),
            pl.BlockSpec((None, hh, tm, V_DIM), out_map),
        ],
        out_shape=[
            jax.ShapeDtypeStruct((bsz, hh, s, QK_PAD), BF16),
            jax.ShapeDtypeStruct((bsz, hh, s, QK_PAD), BF16),
            jax.ShapeDtypeStruct((bsz, hh, s, V_DIM), BF16),
        ],
        compiler_params=_params("parallel", "parallel"),
        name="mla_proj",
    )(x, mods, g.reshape(1, d), cos_t, sin_t, w_dq.astype(BF16), g_q.reshape(1, q_lora),
      wuq_ext, wdkv_ext, g_kv.reshape(1, KV_LORA), w_ukv.astype(BF16))


def _mla_flash_kernel(q_ref, k_ref, v_ref, o_ref, vext_sc, s0_sc, s1_sc, m_sc, acc_sc, *, tq):
    qi = pl.program_id(2)
    n_rep_s = tq // LANES

    @pl.when(qi == 0)
    def _():
        vext_sc[:, 0:V_DIM] = v_ref[...]
        vext_sc[:, V_DIM:2 * V_DIM] = jnp.ones((vext_sc.shape[0], V_DIM), BF16)

    m_sc[...] = jnp.full(m_sc.shape, NEG_INF, F32)
    acc_sc[...] = jnp.zeros(acc_sc.shape, F32)

    def scores(t, s_ref):
        k = k_ref[pl.ds(pl.multiple_of(t * tq, tq), tq), :]
        s_ref[...] = _dot_nt(q_ref[...], k)

    def update(t, s_ref, masked):
        s = s_ref[...]
        if masked:
            row = lax.broadcasted_iota(jnp.int32, (tq, tq), 0)
            col = lax.broadcasted_iota(jnp.int32, (tq, tq), 1)
            s = jnp.where(col <= row, s, NEG_INF)
        m_prev = m_sc[...]
        m_new = jnp.maximum(m_prev, jnp.max(s, axis=-1, keepdims=True))
        alpha = jnp.exp2(m_prev - m_new)
        p = jnp.exp2(s - jnp.tile(m_new, (1, n_rep_s))).astype(BF16)
        v = vext_sc[pl.ds(pl.multiple_of(t * tq, tq), tq), :]
        acc_sc[...] = jnp.tile(alpha, (1, 2)) * acc_sc[...] + _dot(p, v)
        m_sc[...] = m_new

    scores(0, s0_sc)

    def pair(u, carry):
        t = 2 * u
        scores(t + 1, s1_sc)
        update(t, s0_sc, False)
        scores(t + 2, s0_sc)
        update(t + 1, s1_sc, False)
        return carry

    lax.fori_loop(0, qi // 2, pair, 0)

    @pl.when(qi % 2 == 1)
    def _():
        scores(qi, s1_sc)
        update(qi - 1, s0_sc, False)
        update(qi, s1_sc, True)

    @pl.when(qi % 2 == 0)
    def _():
        update(qi, s0_sc, True)

    acc = acc_sc[...]
    o_ref[...] = (acc[:, 0:V_DIM] / acc[:, V_DIM:2 * V_DIM]).astype(o_ref.dtype)


def _mla_flash(q, k, v):
    bsz, hh, s, _ = q.shape
    tq = 1024
    return pl.pallas_call(
        functools.partial(_mla_flash_kernel, tq=tq),
        grid=(bsz, hh, s // tq),
        in_specs=[
            pl.BlockSpec((None, None, tq, QK_PAD), lambda b, h, i: (b, h, i, 0)),
            pl.BlockSpec((None, None, s, QK_PAD), lambda b, h, i: (b, h, 0, 0)),
            pl.BlockSpec((None, None, s, V_DIM), lambda b, h, i: (b, h, 0, 0)),
        ],
        out_specs=pl.BlockSpec((None, tq, V_DIM), lambda b, h, i: (b, i, h)),
        out_shape=jax.ShapeDtypeStruct((bsz, s, hh * V_DIM), BF16),
        scratch_shapes=[pltpu.VMEM((s, 2 * V_DIM), BF16),
                        pltpu.VMEM((tq, tq), F32), pltpu.VMEM((tq, tq), F32),
                        pltpu.VMEM((tq, LANES), F32), pltpu.VMEM((tq, 2 * V_DIM), F32)],
        compiler_params=_params("parallel", "parallel", "arbitrary"),
        name="mla_flash",
    )(q, k, v)


def _post_mlp_kernel(x_ref, a_ref, mod_ref, wo_ref, bo_ref, g_ref, w1_ref, w2_ref, gf_ref,
                     out_ref, x1_sc, h_sc, acc_sc, *, final_norm, n_ff, row_chunk):
    j = pl.program_id(2)
    d = x_ref.shape[-1]
    tm = x_ref.shape[0]

    def step(first, last):
        mod = mod_ref[...]
        gt1, sh2, sc2 = mod[:, 2 * d:3 * d], mod[:, 3 * d:4 * d], mod[:, 4 * d:5 * d]
        gt2 = mod[:, 5 * d:6 * d]
        w1 = w1_ref[...].astype(BF16)
        w2 = w2_ref[...].astype(BF16)
        for r in range(tm // row_chunk):
            rows = slice(r * row_chunk, (r + 1) * row_chunk)
            if first:
                y = _dot(a_ref[rows, :], wo_ref[...]) + bo_ref[...]
                x1 = x_ref[rows, :] + gt1 * y
                h = (_rmsnorm(x1, g_ref[...]) * (1.0 + sc2) + sh2).astype(BF16)
                h_sc[rows, :] = h
                if not last:
                    x1_sc[rows, :] = x1
            else:
                h = h_sc[rows, :]
            u = jnp.square(jnp.maximum(_dot(h, w1), 0.0)).astype(BF16)
            acc = _dot(u, w2)
            if not first:
                acc = acc_sc[rows, :] + acc
            if last:
                x2 = (x1 if first else x1_sc[rows, :]) + gt2 * acc
                if final_norm:
                    x2 = _rmsnorm(x2, gf_ref[...])
                out_ref[rows, :] = x2
            else:
                acc_sc[rows, :] = acc

    pl.when(j == 0)(functools.partial(step, True, n_ff == 1))
    if n_ff > 1:
        pl.when(j == n_ff - 1)(functools.partial(step, False, True))
    if n_ff > 2:
        pl.when(jnp.logical_and(j > 0, j < n_ff - 1))(functools.partial(step, False, False))


def _post_mlp(x, attn, mods, layer, w_o, b_o, g_mlp, w_ff1, w_ff2, g_final, final_norm):
    bsz, s, d = x.shape
    dff = w_ff1.shape[1]
    tm, tf, row_chunk = 1024, 1024, 256
    const = lambda b, i, j: (0, 0)
    tok = lambda b, i, j: (b, i, 0)
    return pl.pallas_call(
        functools.partial(_post_mlp_kernel, final_norm=final_norm, n_ff=dff // tf,
                          row_chunk=row_chunk),
        grid=(bsz, s // tm, dff // tf),
        in_specs=[
            pl.BlockSpec((None, tm, d), tok),
            pl.BlockSpec((None, tm, attn.shape[-1]), tok),
            pl.BlockSpec((None, None, 1, mods.shape[-1]), lambda b, i, j: (layer, b, 0, 0)),
            pl.BlockSpec(w_o.shape, const),
            pl.BlockSpec((1, d), const),
            pl.BlockSpec((1, d), const),
            pl.BlockSpec((d, tf), lambda b, i, j: (0, j)),
            pl.BlockSpec((tf, d), lambda b, i, j: (j, 0)),
            pl.BlockSpec((1, d), const),
        ],
        out_specs=pl.BlockSpec((None, tm, d), tok),
        out_shape=jax.ShapeDtypeStruct((bsz, s, d), F32),
        scratch_shapes=[pltpu.VMEM((tm, d), F32), pltpu.VMEM((tm, d), BF16),
                        pltpu.VMEM((tm, d), F32)],
        compiler_params=_params("parallel", "parallel", "arbitrary"),
        name="post_mlp_final" if final_norm else "post_mlp",
    )(x, attn, mods, w_o.astype(BF16), b_o.reshape(1, d), g_mlp.reshape(1, d),
      w_ff1, w_ff2, g_final.reshape(1, d))


def _swa_proj_kernel(x_ref, mod_ref, g_ref, w_ref, b_ref, o_ref, *, scale):
    d = x_ref.shape[-1]
    mod = mod_ref[...]
    sh1, sc1 = mod[:, 0:d], mod[:, d:2 * d]
    h = (_rmsnorm(x_ref[...], g_ref[...]) * (1.0 + sc1) + sh1).astype(BF16)
    qkv = _dot(h, w_ref[...]) + b_ref[...]
    nq = SWA_HEADS * SWA_HEAD_DIM
    o_ref[:, 0:nq] = (qkv[:, 0:nq] * scale).astype(BF16)
    o_ref[:, nq:] = qkv[:, nq:].astype(BF16)


def _swa_proj(x, mods, layer, g, w_qkv, b_qkv):
    bsz, s, d = x.shape
    n = w_qkv.shape[1]
    tm = 512
    const = lambda b, i: (0, 0)
    return pl.pallas_call(
        functools.partial(_swa_proj_kernel,
                          scale=float(SWA_HEAD_DIM ** -0.5 * math.log2(math.e))),
        grid=(bsz, s // tm),
        in_specs=[
            pl.BlockSpec((None, tm, d), lambda b, i: (b, i, 0)),
            pl.BlockSpec((None, None, 1, mods.shape[-1]), lambda b, i: (layer, b, 0, 0)),
            pl.BlockSpec((1, d), const),
            pl.BlockSpec(w_qkv.shape, const),
            pl.BlockSpec((1, n), const),
        ],
        out_specs=pl.BlockSpec((None, tm, n), lambda b, i: (b, i, 0)),
        out_shape=jax.ShapeDtypeStruct((bsz, s, n), BF16),
        compiler_params=_params("parallel", "parallel"),
        name="swa_proj",
    )(x, mods, g.reshape(1, d), w_qkv.astype(BF16), b_qkv.reshape(1, n))


def _swa_attn_kernel(sink_ref, q_ref, kc_ref, vc_ref, kp_ref, vp_ref, o_ref,
                     bias_sc, s_sc, p_sc, e_sc, *, slopes):
    first_step = jnp.logical_and(pl.program_id(0) == 0, pl.program_id(1) == 0)
    w = WINDOW
    dh = SWA_HEAD_DIM
    group = SWA_HEADS // SWA_KV_HEADS
    gw = group * dh
    log2e = math.log2(math.e)

    @pl.when(first_step)
    def _():
        qpos = lax.broadcasted_iota(jnp.int32, (w, 2 * w), 0)
        kpos = lax.broadcasted_iota(jnp.int32, (w, 2 * w), 1)
        dist = w + qpos - kpos
        in_window = jnp.logical_and(dist >= 0, dist < w)
        dist_f = dist.astype(F32)
        for hd in range(SWA_HEADS):
            g, a = divmod(hd, group)
            bias_sc[g, a * w:(a + 1) * w, :] = jnp.where(
                in_window, dist_f * (-slopes[hd] * log2e), NEG_INF)

    no_prev = jnp.where(pl.program_id(1) == 0, NEG_INF, 0.0)
    ones_cols = jnp.ones((2 * w, 2 * w - 2 * dh), BF16)
    zero_cols = jnp.zeros((2 * w, dh), BF16)
    items = [(g, n) for g in range(SWA_KV_HEADS) for n in range(q_ref.shape[0] // w)]

    def band(cur_ref, prev_ref, g, n):
        cols = slice(g * dh, (g + 1) * dh)
        if n == 0:
            return jnp.concatenate([prev_ref[:, cols], cur_ref[0:w, cols]], axis=0)
        return cur_ref[(n - 1) * w:(n + 1) * w, cols]

    def sink_rows(g):
        return jnp.concatenate([jnp.full((w, w), sink_ref[hd] * log2e, F32)
                                for hd in range(g * group, (g + 1) * group)], axis=0)

    def stage_scores(idx):
        g, n = items[idx]
        qg = jnp.concatenate([q_ref[n * w:(n + 1) * w, hd * dh:(hd + 1) * dh]
                              for hd in range(g * group, (g + 1) * group)], axis=0)
        s = _dot_nt(qg, band(kc_ref, kp_ref, g, n)) + bias_sc[g]
        if n == 0:
            s = jnp.concatenate([s[:, 0:w] + no_prev, s[:, w:2 * w]], axis=1)
        s_sc[idx % 2] = s

    def stage_softmax(idx):
        g, _ = items[idx]
        s = s_sc[idx % 2]
        sink = sink_rows(g)
        m = jnp.maximum(jnp.max(s, axis=-1, keepdims=True), sink)
        p_sc[idx % 2] = jnp.exp2(s - jnp.tile(m, (1, 2))).astype(BF16)
        e_sc[idx % 2] = jnp.exp2(sink - m)

    def stage_out(idx):
        g, n = items[idx]
        vext = jnp.concatenate([band(vc_ref, vp_ref, g, n), zero_cols, ones_cols], axis=1)
        pv = _dot(p_sc[idx % 2], vext)
        o = pv[:, 0:w] / (pv[:, w:2 * w] + e_sc[idx % 2])
        o_g = jnp.concatenate([o[a * w:(a + 1) * w, 0:dh] for a in range(group)], axis=1)
        o_ref[n * w:(n + 1) * w, g * gw:(g + 1) * gw] = o_g.astype(o_ref.dtype)

    n_items = len(items)
    for step in range(n_items + 2):
        if step < n_items:
            stage_scores(step)
        if 1 <= step <= n_items:
            stage_softmax(step - 1)
        if step >= 2:
            stage_out(step - 2)


def _swa_attn(qkv, sinks):
    bsz, s, _ = qkv.shape
    nq = SWA_HEADS * SWA_HEAD_DIM
    nkv = SWA_KV_HEADS * SWA_HEAD_DIM
    tm = 512
    per = tm // WINDOW
    slopes = tuple(float(v) for v in 2.0 ** (-8.0 * np.arange(1, SWA_HEADS + 1) / SWA_HEADS))
    kcol, vcol = nq // nkv, nq // nkv + 1
    prev = lambda b, i: jnp.maximum(i * per - 1, 0)
    return pl.pallas_call(
        functools.partial(_swa_attn_kernel, slopes=slopes),
        grid=(bsz, s // tm),
        in_specs=[
            pl.BlockSpec(memory_space=pltpu.SMEM),
            pl.BlockSpec((None, tm, nq), lambda b, i: (b, i, 0)),
            pl.BlockSpec((None, tm, nkv), lambda b, i: (b, i, kcol)),
            pl.BlockSpec((None, tm, nkv), lambda b, i: (b, i, vcol)),
            pl.BlockSpec((None, WINDOW, nkv), lambda b, i: (b, prev(b, i), kcol)),
            pl.BlockSpec((None, WINDOW, nkv), lambda b, i: (b, prev(b, i), vcol)),
        ],
        out_specs=pl.BlockSpec((None, tm, nq), lambda b, i: (b, i, 0)),
        out_shape=jax.ShapeDtypeStruct((bsz, s, nq), BF16),
        scratch_shapes=[pltpu.VMEM((SWA_KV_HEADS, nq // nkv * WINDOW, 2 * WINDOW), F32),
                        pltpu.VMEM((2, nq // nkv * WINDOW, 2 * WINDOW), F32),
                        pltpu.VMEM((2, nq // nkv * WINDOW, 2 * WINDOW), BF16),
                        pltpu.VMEM((2, nq // nkv * WINDOW, WINDOW), F32)],
        compiler_params=_params("arbitrary", "arbitrary"),
        name="swa_attn",
    )(sinks.astype(F32), qkv, qkv, qkv, qkv, qkv)


def kernel(x, c, positions, w_ada, b_ada, g_mix, g_mlp, mla_w_dq, mla_g_q, mla_w_uq, mla_w_dkv,
           mla_g_kv, mla_w_ukv, mla_w_o, swa_w_qkv, swa_b_qkv, swa_sinks, swa_w_o, swa_b_o,
           w_ff1, w_ff2, g_final):
    depth = w_ada.shape[0]
    bsz, s, d = x.shape
    mods = _adaln(c, w_ada, b_ada).reshape(depth, bsz, 1, w_ada.shape[-1])
    cos_t, sin_t = _rope_tables(positions)
    zero_bias = jnp.zeros((d,), F32)
    for i in range(depth):
        j = i // 2
        last = i == depth - 1
        if i % 2 == 0:
            q, k, v = _mla_proj(x, mods, i, g_mix[i], cos_t, sin_t, mla_w_dq[j], mla_g_q[j],
                                mla_w_uq[j], mla_w_dkv[j], mla_g_kv[j], mla_w_ukv[j])
            attn = _mla_flash(q, k, v)
            w_o, b_o = mla_w_o[j], zero_bias
        else:
            qkv = _swa_proj(x, mods, i, g_mix[i], swa_w_qkv[j], swa_b_qkv[j])
            attn = _swa_attn(qkv, swa_sinks[j])
            w_o, b_o = swa_w_o[j], swa_b_o[j]
        x = _post_mlp(x, attn, mods, i, w_o, b_o, g_mlp[i], w_ff1[i], w_ff2[i], g_final, last)
    return x
```

```python
import functools
import math

import numpy as np
import jax
import jax.numpy as jnp
from jax import lax
from jax.experimental import pallas as pl
from jax.experimental.pallas import tpu as pltpu

F32 = jnp.float32
BF16 = jnp.bfloat16

EPS = 1e-6
ROPE_THETA = 10000.0
MLA_HEADS = 8
QK_NOPE = 128
QK_ROPE = 64
V_DIM = 128
KV_LORA = 256
SWA_HEADS = 16
SWA_KV_HEADS = 4
SWA_HEAD_DIM = 64
WINDOW = 128

LANES = 128
QK_PAD = 2 * LANES
ONES_ROWS = 16
VMEM_LIMIT = 56 * 1024 * 1024

NEG_INF = float("-inf")


def _params(*sem):
    return pltpu.CompilerParams(dimension_semantics=sem, vmem_limit_bytes=VMEM_LIMIT)


def _rmsnorm(x, g):
    y = x * lax.rsqrt(jnp.mean(x * x, axis=-1, keepdims=True) + EPS)
    return y * g


def _dot(a, b):
    return jnp.dot(a, b, preferred_element_type=F32)


def _dot_nt(a, b):
    return lax.dot_general(a, b, (((1,), (1,)), ((), ())), preferred_element_type=F32)


def _adaln_kernel(ct_ref, w_ref, b_ref, o_ref):
    ct = ct_ref[...]
    cond = ct * jax.nn.sigmoid(ct)
    w = w_ref[...]
    rows = [jnp.sum(w * cond[:, b:b + 1], axis=0, keepdims=True)
            for b in range(ct.shape[1])]
    o_ref[...] = jnp.concatenate(rows, axis=0) + b_ref[...]


def _adaln(c, w_ada, b_ada):
    depth, d, n = w_ada.shape
    bsz = c.shape[0]
    tn = 1024
    return pl.pallas_call(
        _adaln_kernel,
        grid=(depth, n // tn),
        in_specs=[
            pl.BlockSpec((d, bsz), lambda l, j: (0, 0)),
            pl.BlockSpec((None, d, tn), lambda l, j: (l, 0, j)),
            pl.BlockSpec((None, 1, tn), lambda l, j: (l, 0, j)),
        ],
        out_specs=pl.BlockSpec((None, bsz, tn), lambda l, j: (l, 0, j)),
        out_shape=jax.ShapeDtypeStruct((depth, bsz, n), F32),
        compiler_params=_params("parallel", "parallel"),
        name="adaln",
    )(c.T, w_ada, b_ada.reshape(depth, 1, n))


def _rope_table_kernel(pos_ref, inv_ref, cos_ref, sin_ref):
    ang = pos_ref[...].astype(F32) * inv_ref[...]
    cos_ref[...] = jnp.cos(ang)
    sin_ref[...] = jnp.sin(ang)


def _rope_tables(positions):
    n = positions.size
    half = QK_ROPE // 2
    per_row = LANES // half
    inv_freq = ROPE_THETA ** (-jnp.arange(half, dtype=F32) / half)
    inv = jnp.tile(inv_freq, per_row).reshape(1, LANES)
    pos_rows = jnp.repeat(positions.reshape(n // per_row, per_row), half, axis=1)
    rows = n // per_row
    tm = 512
    cos_t, sin_t = pl.pallas_call(
        _rope_table_kernel,
        grid=(rows // tm,),
        in_specs=[pl.BlockSpec((tm, LANES), lambda i: (i, 0)),
                  pl.BlockSpec((1, LANES), lambda i: (0, 0))],
        out_specs=[pl.BlockSpec((tm, LANES), lambda i: (i, 0)),
                   pl.BlockSpec((tm, LANES), lambda i: (i, 0))],
        out_shape=[jax.ShapeDtypeStruct((rows, LANES), F32)] * 2,
        compiler_params=_params("parallel"),
        name="rope_table",
    )(pos_rows, inv)
    return cos_t.reshape(n, half), sin_t.reshape(n, half)


def _mla_proj_kernel(x_ref, mod_ref, g_ref, cos_ref, sin_ref, wdq_ref, gq_ref, wuq_ref,
                     wdkv_ref, gkv_ref, wuk_ref, wuvt_ref, q_ref, k_ref, vt_ref, *, scale,
                     row_chunk):
    d = x_ref.shape[-1]
    mod = mod_ref[...]
    sh1, sc1 = mod[:, 0:d], mod[:, d:2 * d]
    reps = LANES // cos_ref.shape[-1]
    n_nope = MLA_HEADS * QK_NOPE
    n_rope = MLA_HEADS * QK_ROPE
    n_chunks = x_ref.shape[0] // row_chunk
    lane = lax.broadcasted_iota(jnp.int32, (row_chunk, LANES), 1)
    low = lane < QK_ROPE

    def down(c):
        rows = slice(c * row_chunk, (c + 1) * row_chunk)
        h = (_rmsnorm(x_ref[rows, :], g_ref[...]) * (1.0 + sc1) + sh1).astype(BF16)
        return _dot(h, wdq_ref[...]), _dot(h, wdkv_ref[...])

    def up(cq_raw, dkv):
        cq = _rmsnorm(cq_raw, gq_ref[...]).astype(BF16)
        ckv = _rmsnorm(dkv[:, 0:KV_LORA], gkv_ref[...]).astype(BF16)
        return _dot(cq, wuq_ref[...]), _dot(ckv, wuk_ref[...]), _dot_nt(wuvt_ref[...], ckv)

    def emit(c, dkv, qe, kn, vt):
        rows = slice(c * row_chunk, (c + 1) * row_chunk)
        cos = jnp.tile(cos_ref[rows, :], (1, reps))
        sin = jnp.tile(sin_ref[rows, :], (1, reps))
        for hd in range(MLA_HEADS):
            q_ref[hd, rows, 0:LANES] = (qe[:, hd * LANES:(hd + 1) * LANES] * scale).astype(BF16)
            j = hd // 2
            a = qe[:, n_nope + j * LANES:n_nope + (j + 1) * LANES]
            b = qe[:, n_nope + n_rope + j * LANES:n_nope + n_rope + (j + 1) * LANES]
            r = (a * cos + b * sin) * scale
            keep = low if hd % 2 == 0 else jnp.logical_not(low)
            q_ref[hd, rows, LANES:QK_PAD] = jnp.where(keep, r, 0.0).astype(BF16)
        kr = (dkv[:, KV_LORA:KV_LORA + LANES] * cos
              + dkv[:, KV_LORA + LANES:KV_LORA + 2 * LANES] * sin)
        kr_even = jnp.where(low, kr, 0.0).astype(BF16)
        kr_odd = jnp.where(low, 0.0, kr).astype(BF16)
        for hd in range(MLA_HEADS):
            k_ref[hd, rows, 0:LANES] = kn[:, hd * QK_NOPE:(hd + 1) * QK_NOPE].astype(BF16)
            k_ref[hd, rows, LANES:QK_PAD] = kr_even if hd % 2 == 0 else kr_odd
            vt_ref[hd, :, rows] = vt[hd * V_DIM:(hd + 1) * V_DIM, :].astype(BF16)

    downs, ups = {}, {}
    for step in range(n_chunks + 2):
        if step < n_chunks:
            downs[step] = down(step)
        if 1 <= step <= n_chunks:
            ups[step - 1] = up(*downs[step - 1])
        if step >= 2:
            emit(step - 2, downs.pop(step - 2)[1], *ups.pop(step - 2))


def _rot_half(w):
    half = w.shape[-1] // 2
    return jnp.concatenate([-w[..., half:], w[..., :half]], axis=-1)


def _mla_proj(x, mods, layer, g, cos_t, sin_t, w_dq, g_q, w_uq, w_dkv, g_kv, w_ukv):
    bsz, s, d = x.shape
    hh = MLA_HEADS
    q_lora = w_dq.shape[1]
    tm, row_chunk = 1024, 256
    wq3 = w_uq.reshape(q_lora, hh, QK_NOPE + QK_ROPE)
    wq_rope = wq3[:, :, QK_NOPE:]
    wuq_ext = jnp.concatenate([
        wq3[:, :, :QK_NOPE].reshape(q_lora, hh * QK_NOPE),
        wq_rope.reshape(q_lora, hh * QK_ROPE),
        _rot_half(wq_rope).reshape(q_lora, hh * QK_ROPE)], axis=1).astype(BF16)
    wkr = w_dkv[:, KV_LORA:]
    wkr_rot = _rot_half(wkr)
    wdkv_ext = jnp.concatenate([w_dkv[:, :KV_LORA], wkr, wkr, wkr_rot, wkr_rot], axis=1).astype(BF16)
    wkv3 = w_ukv.reshape(KV_LORA, hh, QK_NOPE + V_DIM)
    wuk = wkv3[:, :, :QK_NOPE].reshape(KV_LORA, hh * QK_NOPE).astype(BF16)
    wuvt = wkv3[:, :, QK_NOPE:].reshape(KV_LORA, hh * V_DIM).T.astype(BF16)
    scale = float((QK_NOPE + QK_ROPE) ** -0.5 * math.log2(math.e))

    const = lambda b, i: (0, 0)
    tok = lambda b, i: (b * (s // tm) + i, 0)
    out_map = lambda b, i: (b, 0, i, 0)
    return pl.pallas_call(
        functools.partial(_mla_proj_kernel, scale=scale, row_chunk=row_chunk),
        grid=(bsz, s // tm),
        in_specs=[
            pl.BlockSpec((None, tm, d), lambda b, i: (b, i, 0)),
            pl.BlockSpec((None, None, 1, mods.shape[-1]), lambda b, i: (layer, b, 0, 0)),
            pl.BlockSpec((1, d), const),
            pl.BlockSpec((tm, cos_t.shape[-1]), tok),
            pl.BlockSpec((tm, sin_t.shape[-1]), tok),
            pl.BlockSpec(w_dq.shape, const),
            pl.BlockSpec((1, q_lora), const),
            pl.BlockSpec(wuq_ext.shape, const),
            pl.BlockSpec(wdkv_ext.shape, const),
            pl.BlockSpec((1, KV_LORA), const),
            pl.BlockSpec(wuk.shape, const),
            pl.BlockSpec(wuvt.shape, const),
        ],
        out_specs=[
            pl.BlockSpec((None, hh, tm, QK_PAD), out_map),
            pl.BlockSpec((None, hh, tm, QK_PAD), out_map),
            pl.BlockSpec((None, hh, V_DIM, tm), lambda b, i: (b, 0, 0, i)),
        ],
        out_shape=[
            jax.ShapeDtypeStruct((bsz, hh, s, QK_PAD), BF16),
            jax.ShapeDtypeStruct((bsz, hh, s, QK_PAD), BF16),
            jax.ShapeDtypeStruct((bsz, hh, V_DIM, s), BF16),
        ],
        compiler_params=_params("parallel", "parallel"),
        name="mla_proj",
    )(x, mods, g.reshape(1, d), cos_t, sin_t, w_dq.astype(BF16), g_q.reshape(1, q_lora),
      wuq_ext, wdkv_ext, g_kv.reshape(1, KV_LORA), wuk, wuvt)


def _mla_flash_kernel(q_ref, k_ref, vt_ref, o_ref, vext_sc, s0_sc, s1_sc, m_sc, acc_sc, *, tq):
    qi = pl.program_id(2)

    @pl.when(qi == 0)
    def _():
        for t in range(vext_sc.shape[0]):
            vext_sc[t, 0:V_DIM, :] = vt_ref[:, t * tq:(t + 1) * tq]
            vext_sc[t, V_DIM:, :] = jnp.ones((ONES_ROWS, tq), BF16)

    m_sc[...] = jnp.full(m_sc.shape, NEG_INF, F32)
    acc_sc[...] = jnp.zeros(acc_sc.shape, F32)

    def scores(t, s_ref):
        k = k_ref[pl.ds(pl.multiple_of(t * tq, tq), tq), :]
        s_ref[...] = _dot_nt(k, q_ref[...])

    def update(t, s_ref, masked):
        s = s_ref[...]
        if masked:
            key = lax.broadcasted_iota(jnp.int32, (tq, tq), 0)
            qry = lax.broadcasted_iota(jnp.int32, (tq, tq), 1)
            s = jnp.where(key <= qry, s, NEG_INF)
        m_prev = m_sc[...]
        m_new = jnp.maximum(m_prev, jnp.max(s, axis=0, keepdims=True))
        alpha = jnp.exp2(m_prev - m_new)
        p = jnp.exp2(s - m_new).astype(BF16)
        acc_sc[...] = alpha * acc_sc[...] + _dot(vext_sc[t], p)
        m_sc[...] = m_new

    scores(0, s0_sc)

    def pair(u, carry):
        t = 2 * u
        scores(t + 1, s1_sc)
        update(t, s0_sc, False)
        scores(t + 2, s0_sc)
        update(t + 1, s1_sc, False)
        return carry

    lax.fori_loop(0, qi // 2, pair, 0)

    @pl.when(qi % 2 == 1)
    def _():
        scores(qi, s1_sc)
        update(qi - 1, s0_sc, False)
        update(qi, s1_sc, True)

    @pl.when(qi % 2 == 0)
    def _():
        update(qi, s0_sc, True)

    acc = acc_sc[...]
    o_t = acc[0:V_DIM, :] / acc[V_DIM:V_DIM + 1, :]
    o_ref[...] = o_t.T.astype(o_ref.dtype)


def _mla_flash(q, k, vt):
    bsz, hh, s, _ = q.shape
    tq = 1024
    return pl.pallas_call(
        functools.partial(_mla_flash_kernel, tq=tq),
        grid=(bsz, hh, s // tq),
        in_specs=[
            pl.BlockSpec((None, None, tq, QK_PAD), lambda b, h, i: (b, h, i, 0)),
            pl.BlockSpec((None, None, s, QK_PAD), lambda b, h, i: (b, h, 0, 0)),
            pl.BlockSpec((None, None, V_DIM, s), lambda b, h, i: (b, h, 0, 0)),
        ],
        out_specs=pl.BlockSpec((None, tq, V_DIM), lambda b, h, i: (b, i, h)),
        out_shape=jax.ShapeDtypeStruct((bsz, s, hh * V_DIM), BF16),
        scratch_shapes=[pltpu.VMEM((s // tq, V_DIM + ONES_ROWS, tq), BF16),
                        pltpu.VMEM((tq, tq), F32), pltpu.VMEM((tq, tq), F32),
                        pltpu.VMEM((1, tq), F32), pltpu.VMEM((V_DIM + ONES_ROWS, tq), F32)],
        compiler_params=_params("parallel", "parallel", "arbitrary"),
        name="mla_flash",
    )(q, k, vt)


def _post_mlp_kernel(x_ref, a_ref, mod_ref, wo_ref, bo_ref, g_ref, w1_ref, w2_ref, gf_ref,
                     out_ref, x1_sc, h_sc, acc_sc, *, final_norm, n_ff, row_chunk):
    j = pl.program_id(2)
    d = x_ref.shape[-1]
    tm = x_ref.shape[0]

    def step(first, last):
        mod = mod_ref[...]
        gt1, sh2, sc2 = mod[:, 2 * d:3 * d], mod[:, 3 * d:4 * d], mod[:, 4 * d:5 * d]
        gt2 = mod[:, 5 * d:6 * d]
        w1 = w1_ref[...].astype(BF16)
        w2 = w2_ref[...].astype(BF16)
        for r in range(tm // row_chunk):
            rows = slice(r * row_chunk, (r + 1) * row_chunk)
            if first:
                y = _dot(a_ref[rows, :], wo_ref[...]) + bo_ref[...]
                x1 = x_ref[rows, :] + gt1 * y
                h = (_rmsnorm(x1, g_ref[...]) * (1.0 + sc2) + sh2).astype(BF16)
                h_sc[rows, :] = h
                if not last:
                    x1_sc[rows, :] = x1
            else:
                h = h_sc[rows, :]
            u = jnp.square(jnp.maximum(_dot(h, w1), 0.0)).astype(BF16)
            acc = _dot(u, w2)
            if not first:
                acc = acc_sc[rows, :] + acc
            if last:
                x2 = (x1 if first else x1_sc[rows, :]) + gt2 * acc
                if final_norm:
                    x2 = _rmsnorm(x2, gf_ref[...])
                out_ref[rows, :] = x2
            else:
                acc_sc[rows, :] = acc

    pl.when(j == 0)(functools.partial(step, True, n_ff == 1))
    if n_ff > 1:
        pl.when(j == n_ff - 1)(functools.partial(step, False, True))
    if n_ff > 2:
        pl.when(jnp.logical_and(j > 0, j < n_ff - 1))(functools.partial(step, False, False))


def _post_mlp(x, attn, mods, layer, w_o, b_o, g_mlp, w_ff1, w_ff2, g_final, final_norm):
    bsz, s, d = x.shape
    dff = w_ff1.shape[-1]
    tm, tf, row_chunk = 1024, 1024, 256
    const = lambda b, i, j: (0, 0)
    tok = lambda b, i, j: (b, i, 0)
    return pl.pallas_call(
        functools.partial(_post_mlp_kernel, final_norm=final_norm, n_ff=dff // tf,
                          row_chunk=row_chunk),
        grid=(bsz, s // tm, dff // tf),
        in_specs=[
            pl.BlockSpec((None, tm, d), tok),
            pl.BlockSpec((None, tm, attn.shape[-1]), tok),
            pl.BlockSpec((None, None, 1, mods.shape[-1]), lambda b, i, j: (layer, b, 0, 0)),
            pl.BlockSpec(w_o.shape, const),
            pl.BlockSpec((1, d), const),
            pl.BlockSpec((1, d), const),
            pl.BlockSpec((None, d, tf), lambda b, i, j: (layer, 0, j)),
            pl.BlockSpec((None, tf, d), lambda b, i, j: (layer, j, 0)),
            pl.BlockSpec((1, d), const),
        ],
        out_specs=pl.BlockSpec((None, tm, d), tok),
        out_shape=jax.ShapeDtypeStruct((bsz, s, d), F32),
        scratch_shapes=[pltpu.VMEM((tm, d), F32), pltpu.VMEM((tm, d), BF16),
                        pltpu.VMEM((tm, d), F32)],
        compiler_params=_params("parallel", "parallel", "arbitrary"),
        name="post_mlp_final" if final_norm else "post_mlp",
    )(x, attn, mods, w_o.astype(BF16), b_o.reshape(1, d), g_mlp.reshape(1, d),
      w_ff1, w_ff2, g_final.reshape(1, d))


def _swa_proj_kernel(x_ref, mod_ref, g_ref, w_ref, b_ref, o_ref, *, scale):
    d = x_ref.shape[-1]
    mod = mod_ref[...]
    sh1, sc1 = mod[:, 0:d], mod[:, d:2 * d]
    h = (_rmsnorm(x_ref[...], g_ref[...]) * (1.0 + sc1) + sh1).astype(BF16)
    qkv = _dot(h, w_ref[...]) + b_ref[...]
    nq = SWA_HEADS * SWA_HEAD_DIM
    o_ref[:, 0:nq] = (qkv[:, 0:nq] * scale).astype(BF16)
    o_ref[:, nq:] = qkv[:, nq:].astype(BF16)


def _swa_proj(x, mods, layer, g, w_qkv, b_qkv):
    bsz, s, d = x.shape
    n = w_qkv.shape[1]
    tm = 512
    const = lambda b, i: (0, 0)
    return pl.pallas_call(
        functools.partial(_swa_proj_kernel,
                          scale=float(SWA_HEAD_DIM ** -0.5 * math.log2(math.e))),
        grid=(bsz, s // tm),
        in_specs=[
            pl.BlockSpec((None, tm, d), lambda b, i: (b, i, 0)),
            pl.BlockSpec((None, None, 1, mods.shape[-1]), lambda b, i: (layer, b, 0, 0)),
            pl.BlockSpec((1, d), const),
            pl.BlockSpec(w_qkv.shape, const),
            pl.BlockSpec((1, n), const),
        ],
        out_specs=pl.BlockSpec((None, tm, n), lambda b, i: (b, i, 0)),
        out_shape=jax.ShapeDtypeStruct((bsz, s, n), BF16),
        compiler_params=_params("parallel", "parallel"),
        name="swa_proj",
    )(x, mods, g.reshape(1, d), w_qkv.astype(BF16), b_qkv.reshape(1, n))


def _swa_attn_kernel(sink_ref, q_ref, kc_ref, vc_ref, kp_ref, vp_ref, o_ref,
                     bias_sc, s_sc, p_sc, e_sc, *, slopes):
    first_step = jnp.logical_and(pl.program_id(0) == 0, pl.program_id(1) == 0)
    w = WINDOW
    dh = SWA_HEAD_DIM
    group = SWA_HEADS // SWA_KV_HEADS
    gw = group * dh
    log2e = math.log2(math.e)

    @pl.when(first_step)
    def _():
        qpos = lax.broadcasted_iota(jnp.int32, (w, 2 * w), 0)
        kpos = lax.broadcasted_iota(jnp.int32, (w, 2 * w), 1)
        dist = w + qpos - kpos
        in_window = jnp.logical_and(dist >= 0, dist < w)
        dist_f = dist.astype(F32)
        for hd in range(SWA_HEADS):
            g, a = divmod(hd, group)
            bias_sc[g, a * w:(a + 1) * w, :] = jnp.where(
                in_window, dist_f * (-slopes[hd] * log2e), NEG_INF)

    no_prev = jnp.where(pl.program_id(1) == 0, NEG_INF, 0.0)
    ones_cols = jnp.ones((2 * w, 2 * w - 2 * dh), BF16)
    zero_cols = jnp.zeros((2 * w, dh), BF16)
    items = [(g, n) for g in range(SWA_KV_HEADS) for n in range(q_ref.shape[0] // w)]

    def band(cur_ref, prev_ref, g, n):
        cols = slice(g * dh, (g + 1) * dh)
        if n == 0:
            return jnp.concatenate([prev_ref[:, cols], cur_ref[0:w, cols]], axis=0)
        return cur_ref[(n - 1) * w:(n + 1) * w, cols]

    def sink_rows(g):
        return jnp.concatenate([jnp.full((w, w), sink_ref[hd] * log2e, F32)
                                for hd in range(g * group, (g + 1) * group)], axis=0)

    def stage_scores(idx):
        g, n = items[idx]
        qg = jnp.concatenate([q_ref[n * w:(n + 1) * w, hd * dh:(hd + 1) * dh]
                              for hd in range(g * group, (g + 1) * group)], axis=0)
        s = _dot_nt(qg, band(kc_ref, kp_ref, g, n)) + bias_sc[g]
        if n == 0:
            s = jnp.concatenate([s[:, 0:w] + no_prev, s[:, w:2 * w]], axis=1)
        s_sc[idx % 2] = s

    def stage_softmax(idx):
        g, _ = items[idx]
        s = s_sc[idx % 2]
        sink = sink_rows(g)
        m = jnp.maximum(jnp.max(s, axis=-1, keepdims=True), sink)
        p_sc[idx % 2] = jnp.exp2(s - jnp.tile(m, (1, 2))).astype(BF16)
        e_sc[idx % 2] = jnp.exp2(sink - m)

    def stage_out(idx):
        g, n = items[idx]
        vext = jnp.concatenate([band(vc_ref, vp_ref, g, n), zero_cols, ones_cols], axis=1)
        pv = _dot(p_sc[idx % 2], vext)
        o = pv[:, 0:w] / (pv[:, w:2 * w] + e_sc[idx % 2])
        o_g = jnp.concatenate([o[a * w:(a + 1) * w, 0:dh] for a in range(group)], axis=1)
        o_ref[n * w:(n + 1) * w, g * gw:(g + 1) * gw] = o_g.astype(o_ref.dtype)

    n_items = len(items)
    for step in range(n_items + 2):
        if step < n_items:
            stage_scores(step)
        if 1 <= step <= n_items:
            stage_softmax(step - 1)
        if step >= 2:
            stage_out(step - 2)


def _swa_attn(qkv, sinks):
    bsz, s, _ = qkv.shape
    nq = SWA_HEADS * SWA_HEAD_DIM
    nkv = SWA_KV_HEADS * SWA_HEAD_DIM
    tm = 512
    per = tm // WINDOW
    slopes = tuple(float(v) for v in 2.0 ** (-8.0 * np.arange(1, SWA_HEADS + 1) / SWA_HEADS))
    kcol, vcol = nq // nkv, nq // nkv + 1
    prev = lambda b, i: jnp.maximum(i * per - 1, 0)
    return pl.pallas_call(
        functools.partial(_swa_attn_kernel, slopes=slopes),
        grid=(bsz, s // tm),
        in_specs=[
            pl.BlockSpec(memory_space=pltpu.SMEM),
            pl.BlockSpec((None, tm, nq), lambda b, i: (b, i, 0)),
            pl.BlockSpec((None, tm, nkv), lambda b, i: (b, i, kcol)),
            pl.BlockSpec((None, tm, nkv), lambda b, i: (b, i, vcol)),
            pl.BlockSpec((None, WINDOW, nkv), lambda b, i: (b, prev(b, i), kcol)),
            pl.BlockSpec((None, WINDOW, nkv), lambda b, i: (b, prev(b, i), vcol)),
        ],
        out_specs=pl.BlockSpec((None, tm, nq), lambda b, i: (b, i, 0)),
        out_shape=jax.ShapeDtypeStruct((bsz, s, nq), BF16),
        scratch_shapes=[pltpu.VMEM((SWA_KV_HEADS, nq // nkv * WINDOW, 2 * WINDOW), F32),
                        pltpu.VMEM((2, nq // nkv * WINDOW, 2 * WINDOW), F32),
                        pltpu.VMEM((2, nq // nkv * WINDOW, 2 * WINDOW), BF16),
                        pltpu.VMEM((2, nq // nkv * WINDOW, WINDOW), F32)],
        compiler_params=_params("arbitrary", "arbitrary"),
        name="swa_attn",
    )(sinks.astype(F32), qkv, qkv, qkv, qkv, qkv)


def kernel(x, c, positions, w_ada, b_ada, g_mix, g_mlp, mla_w_dq, mla_g_q, mla_w_uq, mla_w_dkv,
           mla_g_kv, mla_w_ukv, mla_w_o, swa_w_qkv, swa_b_qkv, swa_sinks, swa_w_o, swa_b_o,
           w_ff1, w_ff2, g_final):
    depth = w_ada.shape[0]
    bsz, s, d = x.shape
    mods = _adaln(c, w_ada, b_ada).reshape(depth, bsz, 1, w_ada.shape[-1])
    cos_t, sin_t = _rope_tables(positions)
    zero_bias = jnp.zeros((d,), F32)
    for i in range(depth):
        j = i // 2
        last = i == depth - 1
        if i % 2 == 0:
            q, k, v = _mla_proj(x, mods, i, g_mix[i], cos_t, sin_t, mla_w_dq[j], mla_g_q[j],
                                mla_w_uq[j], mla_w_dkv[j], mla_g_kv[j], mla_w_ukv[j])
            attn = _mla_flash(q, k, v)
            w_o, b_o = mla_w_o[j], zero_bias
        else:
            qkv = _swa_proj(x, mods, i, g_mix[i], swa_w_qkv[j], swa_b_qkv[j])
            attn = _swa_attn(qkv, swa_sinks[j])
            w_o, b_o = swa_w_o[j], swa_b_o[j]
        x = _post_mlp(x, attn, mods, i, w_o, b_o, g_mlp[i], w_ff1, w_ff2, g_final, last)
    return x
```

```python
import functools
import math

import numpy as np
import jax
import jax.numpy as jnp
from jax import lax
from jax.experimental import pallas as pl
from jax.experimental.pallas import tpu as pltpu

F32 = jnp.float32
BF16 = jnp.bfloat16

EPS = 1e-6
ROPE_THETA = 10000.0
MLA_HEADS = 8
QK_NOPE = 128
QK_ROPE = 64
V_DIM = 128
KV_LORA = 256
SWA_HEADS = 16
SWA_KV_HEADS = 4
SWA_HEAD_DIM = 64
WINDOW = 128

LANES = 128
QK_PAD = 2 * LANES
ONES_ROWS = 16
VMEM_LIMIT = 56 * 1024 * 1024

NEG_INF = float("-inf")


def _params(*sem):
    return pltpu.CompilerParams(dimension_semantics=sem, vmem_limit_bytes=VMEM_LIMIT)


def _rmsnorm(x, g):
    y = x * lax.rsqrt(jnp.mean(x * x, axis=-1, keepdims=True) + EPS)
    return y * g


def _dot(a, b):
    return jnp.dot(a, b, preferred_element_type=F32)


def _dot_nt(a, b):
    return lax.dot_general(a, b, (((1,), (1,)), ((), ())), preferred_element_type=F32)


def _adaln_kernel(ct_ref, w_ref, b_ref, o_ref):
    ct = ct_ref[...]
    cond = ct * jax.nn.sigmoid(ct)
    w = w_ref[...]
    rows = [jnp.sum(w * cond[:, b:b + 1], axis=0, keepdims=True)
            for b in range(ct.shape[1])]
    o_ref[...] = jnp.concatenate(rows, axis=0) + b_ref[...]


def _adaln(c, w_ada, b_ada):
    depth, d, n = w_ada.shape
    bsz = c.shape[0]
    tn = 1024
    return pl.pallas_call(
        _adaln_kernel,
        grid=(depth, n // tn),
        in_specs=[
            pl.BlockSpec((d, bsz), lambda l, j: (0, 0)),
            pl.BlockSpec((None, d, tn), lambda l, j: (l, 0, j)),
            pl.BlockSpec((None, 1, tn), lambda l, j: (l, 0, j)),
        ],
        out_specs=pl.BlockSpec((None, bsz, tn), lambda l, j: (l, 0, j)),
        out_shape=jax.ShapeDtypeStruct((depth, bsz, n), F32),
        compiler_params=_params("parallel", "parallel"),
        name="adaln",
    )(c.T, w_ada, b_ada.reshape(depth, 1, n))


def _rope_table_kernel(pos_ref, inv_ref, cos_ref, sin_ref):
    ang = pos_ref[...].astype(F32) * inv_ref[...]
    cos_ref[...] = jnp.cos(ang)
    sin_ref[...] = jnp.sin(ang)


def _rope_tables(positions):
    n = positions.size
    half = QK_ROPE // 2
    per_row = LANES // half
    inv_freq = ROPE_THETA ** (-jnp.arange(half, dtype=F32) / half)
    inv = jnp.tile(inv_freq, per_row).reshape(1, LANES)
    pos_rows = jnp.repeat(positions.reshape(n // per_row, per_row), half, axis=1)
    rows = n // per_row
    tm = 512
    cos_t, sin_t = pl.pallas_call(
        _rope_table_kernel,
        grid=(rows // tm,),
        in_specs=[pl.BlockSpec((tm, LANES), lambda i: (i, 0)),
                  pl.BlockSpec((1, LANES), lambda i: (0, 0))],
        out_specs=[pl.BlockSpec((tm, LANES), lambda i: (i, 0)),
                   pl.BlockSpec((tm, LANES), lambda i: (i, 0))],
        out_shape=[jax.ShapeDtypeStruct((rows, LANES), F32)] * 2,
        compiler_params=_params("parallel"),
        name="rope_table",
    )(pos_rows, inv)
    return cos_t.reshape(n, half), sin_t.reshape(n, half)


def _mla_proj_kernel(x_ref, mod_ref, g_ref, cos_ref, sin_ref, wdq_ref, gq_ref, wuq_ref,
                     wdkv_ref, gkv_ref, wuk_ref, wuvt_ref, q_ref, k_ref, vt_ref, *, scale,
                     row_chunk):
    d = x_ref.shape[-1]
    mod = mod_ref[...]
    sh1, sc1 = mod[:, 0:d], mod[:, d:2 * d]
    reps = LANES // cos_ref.shape[-1]
    n_nope = MLA_HEADS * QK_NOPE
    n_rope = MLA_HEADS * QK_ROPE
    n_chunks = x_ref.shape[0] // row_chunk
    lane = lax.broadcasted_iota(jnp.int32, (row_chunk, LANES), 1)
    low = lane < QK_ROPE

    def down(c):
        rows = slice(c * row_chunk, (c + 1) * row_chunk)
        h = (_rmsnorm(x_ref[rows, :], g_ref[...]) * (1.0 + sc1) + sh1).astype(BF16)
        return _dot(h, wdq_ref[...]), _dot(h, wdkv_ref[...])

    def up(cq_raw, dkv):
        cq = _rmsnorm(cq_raw, gq_ref[...]).astype(BF16)
        ckv = _rmsnorm(dkv[:, 0:KV_LORA], gkv_ref[...]).astype(BF16)
        return _dot(cq, wuq_ref[...]), _dot(ckv, wuk_ref[...]), _dot_nt(wuvt_ref[...], ckv)

    def emit(c, dkv, qe, kn, vt):
        rows = slice(c * row_chunk, (c + 1) * row_chunk)
        cos = jnp.tile(cos_ref[rows, :], (1, reps))
        sin = jnp.tile(sin_ref[rows, :], (1, reps))
        for hd in range(MLA_HEADS):
            q_ref[hd, rows, 0:LANES] = (qe[:, hd * LANES:(hd + 1) * LANES] * scale).astype(BF16)
            j = hd // 2
            a = qe[:, n_nope + j * LANES:n_nope + (j + 1) * LANES]
            b = qe[:, n_nope + n_rope + j * LANES:n_nope + n_rope + (j + 1) * LANES]
            r = (a * cos + b * sin) * scale
            keep = low if hd % 2 == 0 else jnp.logical_not(low)
            q_ref[hd, rows, LANES:QK_PAD] = jnp.where(keep, r, 0.0).astype(BF16)
        kr = (dkv[:, KV_LORA:KV_LORA + LANES] * cos
              + dkv[:, KV_LORA + LANES:KV_LORA + 2 * LANES] * sin)
        kr_even = jnp.where(low, kr, 0.0).astype(BF16)
        kr_odd = jnp.where(low, 0.0, kr).astype(BF16)
        for hd in range(MLA_HEADS):
            k_ref[hd, rows, 0:LANES] = kn[:, hd * QK_NOPE:(hd + 1) * QK_NOPE].astype(BF16)
            k_ref[hd, rows, LANES:QK_PAD] = kr_even if hd % 2 == 0 else kr_odd
            vt_ref[hd, :, rows] = vt[hd * V_DIM:(hd + 1) * V_DIM, :].astype(BF16)

    downs, ups = {}, {}
    for step in range(n_chunks + 2):
        if step < n_chunks:
            downs[step] = down(step)
        if 1 <= step <= n_chunks:
            ups[step - 1] = up(*downs[step - 1])
        if step >= 2:
            emit(step - 2, downs.pop(step - 2)[1], *ups.pop(step - 2))


def _rot_half(w):
    half = w.shape[-1] // 2
    return jnp.concatenate([-w[..., half:], w[..., :half]], axis=-1)


def _mla_proj(x, mods, layer, g, cos_t, sin_t, w_dq, g_q, w_uq, w_dkv, g_kv, w_ukv):
    bsz, s, d = x.shape
    hh = MLA_HEADS
    q_lora = w_dq.shape[1]
    tm, row_chunk = 1024, 256
    wq3 = w_uq.reshape(q_lora, hh, QK_NOPE + QK_ROPE)
    wq_rope = wq3[:, :, QK_NOPE:]
    wuq_ext = jnp.concatenate([
        wq3[:, :, :QK_NOPE].reshape(q_lora, hh * QK_NOPE),
        wq_rope.reshape(q_lora, hh * QK_ROPE),
        _rot_half(wq_rope).reshape(q_lora, hh * QK_ROPE)], axis=1).astype(BF16)
    wkr = w_dkv[:, KV_LORA:]
    wkr_rot = _rot_half(wkr)
    wdkv_ext = jnp.concatenate([w_dkv[:, :KV_LORA], wkr, wkr, wkr_rot, wkr_rot], axis=1).astype(BF16)
    wkv3 = w_ukv.reshape(KV_LORA, hh, QK_NOPE + V_DIM)
    wuk = wkv3[:, :, :QK_NOPE].reshape(KV_LORA, hh * QK_NOPE).astype(BF16)
    wuvt = wkv3[:, :, QK_NOPE:].reshape(KV_LORA, hh * V_DIM).T.astype(BF16)
    scale = float((QK_NOPE + QK_ROPE) ** -0.5 * math.log2(math.e))

    const = lambda b, i: (0, 0)
    tok = lambda b, i: (b * (s // tm) + i, 0)
    out_map = lambda b, i: (b, 0, i, 0)
    return pl.pallas_call(
        functools.partial(_mla_proj_kernel, scale=scale, row_chunk=row_chunk),
        grid=(bsz, s // tm),
        in_specs=[
            pl.BlockSpec((None, tm, d), lambda b, i: (b, i, 0)),
            pl.BlockSpec((None, None, 1, mods.shape[-1]), lambda b, i: (layer, b, 0, 0)),
            pl.BlockSpec((1, d), const),
            pl.BlockSpec((tm, cos_t.shape[-1]), tok),
            pl.BlockSpec((tm, sin_t.shape[-1]), tok),
            pl.BlockSpec(w_dq.shape, const),
            pl.BlockSpec((1, q_lora), const),
            pl.BlockSpec(wuq_ext.shape, const),
            pl.BlockSpec(wdkv_ext.shape, const),
            pl.BlockSpec((1, KV_LORA), const),
            pl.BlockSpec(wuk.shape, const),
            pl.BlockSpec(wuvt.shape, const),
        ],
        out_specs=[
            pl.BlockSpec((None, hh, tm, QK_PAD), out_map),
            pl.BlockSpec((None, hh, tm, QK_PAD), out_map),
            pl.BlockSpec((None, hh, V_DIM, tm), lambda b, i: (b, 0, 0, i)),
        ],
        out_shape=[
            jax.ShapeDtypeStruct((bsz, hh, s, QK_PAD), BF16),
            jax.ShapeDtypeStruct((bsz, hh, s, QK_PAD), BF16),
            jax.ShapeDtypeStruct((bsz, hh, V_DIM, s), BF16),
        ],
        compiler_params=_params("parallel", "parallel"),
        name="mla_proj",
    )(x, mods, g.reshape(1, d), cos_t, sin_t, w_dq.astype(BF16), g_q.reshape(1, q_lora),
      wuq_ext, wdkv_ext, g_kv.reshape(1, KV_LORA), wuk, wuvt)


def _mla_flash_kernel(q_ref, k_ref, vt_ref, o_ref, vext_sc, s0_sc, s1_sc, m_sc, acc_sc, *, tq):
    qi = pl.program_id(2)

    @pl.when(qi == 0)
    def _():
        for t in range(vext_sc.shape[0]):
            vext_sc[t, 0:V_DIM, :] = vt_ref[:, t * tq:(t + 1) * tq]
            vext_sc[t, V_DIM:, :] = jnp.ones((ONES_ROWS, tq), BF16)

    m_sc[...] = jnp.full(m_sc.shape, NEG_INF, F32)
    acc_sc[...] = jnp.zeros(acc_sc.shape, F32)

    def scores(t, s_ref):
        k = k_ref[pl.ds(pl.multiple_of(t * tq, tq), tq), :]
        s_ref[...] = _dot_nt(k, q_ref[...])

    def update(t, s_ref, masked):
        s = s_ref[...]
        if masked:
            key = lax.broadcasted_iota(jnp.int32, (tq, tq), 0)
            qry = lax.broadcasted_iota(jnp.int32, (tq, tq), 1)
            s = jnp.where(key <= qry, s, NEG_INF)
        m_prev = m_sc[...]
        m_new = jnp.maximum(m_prev, jnp.max(s, axis=0, keepdims=True))
        alpha = jnp.exp2(m_prev - m_new)
        p = jnp.exp2(s - m_new).astype(BF16)
        acc_sc[...] = alpha * acc_sc[...] + _dot(vext_sc[t], p)
        m_sc[...] = m_new

    scores(0, s0_sc)

    def pair(u, carry):
        t = 2 * u
        scores(t + 1, s1_sc)
        update(t, s0_sc, False)
        scores(t + 2, s0_sc)
        update(t + 1, s1_sc, False)
        return carry

    lax.fori_loop(0, qi // 2, pair, 0)

    @pl.when(qi % 2 == 1)
    def _():
        scores(qi, s1_sc)
        update(qi - 1, s0_sc, False)
        update(qi, s1_sc, True)

    @pl.when(qi % 2 == 0)
    def _():
        update(qi, s0_sc, True)

    acc = acc_sc[...]
    o_t = acc[0:V_DIM, :] / acc[V_DIM:V_DIM + 1, :]
    o_ref[...] = o_t.T.astype(o_ref.dtype)


def _mla_flash(q, k, vt):
    bsz, hh, s, _ = q.shape
    tq = 1024
    return pl.pallas_call(
        functools.partial(_mla_flash_kernel, tq=tq),
        grid=(bsz, hh, s // tq),
        in_specs=[
            pl.BlockSpec((None, None, tq, QK_PAD), lambda b, h, i: (b, h, i, 0)),
            pl.BlockSpec((None, None, s, QK_PAD), lambda b, h, i: (b, h, 0, 0)),
            pl.BlockSpec((None, None, V_DIM, s), lambda b, h, i: (b, h, 0, 0)),
        ],
        out_specs=pl.BlockSpec((None, tq, V_DIM), lambda b, h, i: (b, i, h)),
        out_shape=jax.ShapeDtypeStruct((bsz, s, hh * V_DIM), BF16),
        scratch_shapes=[pltpu.VMEM((s // tq, V_DIM + ONES_ROWS, tq), BF16),
                        pltpu.VMEM((tq, tq), F32), pltpu.VMEM((tq, tq), F32),
                        pltpu.VMEM((1, tq), F32), pltpu.VMEM((V_DIM + ONES_ROWS, tq), F32)],
        compiler_params=_params("parallel", "parallel", "arbitrary"),
        name="mla_flash",
    )(q, k, vt)


def _post_mlp_kernel(x_ref, a_ref, mod_ref, wo_ref, bo_ref, g_ref, w1_ref, w2_ref, gf_ref,
                     out_ref, x1_sc, h_sc, acc_sc, *, final_norm, n_ff, row_chunk):
    j = pl.program_id(2)
    d = x_ref.shape[-1]
    tm = x_ref.shape[0]

    def step(first, last):
        mod = mod_ref[...]
        gt1, sh2, sc2 = mod[:, 2 * d:3 * d], mod[:, 3 * d:4 * d], mod[:, 4 * d:5 * d]
        gt2 = mod[:, 5 * d:6 * d]
        w1 = w1_ref[...].astype(BF16)
        w2 = w2_ref[...].astype(BF16)
        for r in range(tm // row_chunk):
            rows = slice(r * row_chunk, (r + 1) * row_chunk)
            if first:
                y = _dot(a_ref[rows, :], wo_ref[...]) + bo_ref[...]
                x1 = x_ref[rows, :] + gt1 * y
                h = (_rmsnorm(x1, g_ref[...]) * (1.0 + sc2) + sh2).astype(BF16)
                h_sc[rows, :] = h
                if not last:
                    x1_sc[rows, :] = x1
            else:
                h = h_sc[rows, :]
            u = jnp.square(jnp.maximum(_dot(h, w1), 0.0)).astype(BF16)
            acc = _dot(u, w2)
            if not first:
                acc = acc_sc[rows, :] + acc
            if last:
                x2 = (x1 if first else x1_sc[rows, :]) + gt2 * acc
                if final_norm:
                    x2 = _rmsnorm(x2, gf_ref[...])
                out_ref[rows, :] = x2
            else:
                acc_sc[rows, :] = acc

    pl.when(j == 0)(functools.partial(step, True, n_ff == 1))
    if n_ff > 1:
        pl.when(j == n_ff - 1)(functools.partial(step, False, True))
    if n_ff > 2:
        pl.when(jnp.logical_and(j > 0, j < n_ff - 1))(functools.partial(step, False, False))


def _post_mlp(x, attn, mods, layer, w_o, b_o, g_mlp, w_ff1, w_ff2, g_final, final_norm):
    bsz, s, d = x.shape
    dff = w_ff1.shape[-1]
    tm, tf, row_chunk = 1024, 1024, 256
    const = lambda b, i, j: (0, 0)
    tok = lambda b, i, j: (b, i, 0)
    return pl.pallas_call(
        functools.partial(_post_mlp_kernel, final_norm=final_norm, n_ff=dff // tf,
                          row_chunk=row_chunk),
        grid=(bsz, s // tm, dff // tf),
        in_specs=[
            pl.BlockSpec((None, tm, d), tok),
            pl.BlockSpec((None, tm, attn.shape[-1]), tok),
            pl.BlockSpec((None, None, 1, mods.shape[-1]), lambda b, i, j: (layer, b, 0, 0)),
            pl.BlockSpec(w_o.shape, const),
            pl.BlockSpec((1, d), const),
            pl.BlockSpec((1, d), const),
            pl.BlockSpec((None, d, tf), lambda b, i, j: (layer, 0, j)),
            pl.BlockSpec((None, tf, d), lambda b, i, j: (layer, j, 0)),
            pl.BlockSpec((1, d), const),
        ],
        out_specs=pl.BlockSpec((None, tm, d), tok),
        out_shape=jax.ShapeDtypeStruct((bsz, s, d), F32),
        scratch_shapes=[pltpu.VMEM((tm, d), F32), pltpu.VMEM((tm, d), BF16),
                        pltpu.VMEM((tm, d), F32)],
        compiler_params=_params("parallel", "parallel", "arbitrary"),
        name="post_mlp_final" if final_norm else "post_mlp",
    )(x, attn, mods, w_o.astype(BF16), b_o.reshape(1, d), g_mlp.reshape(1, d),
      w_ff1, w_ff2, g_final.reshape(1, d))


def _swa_proj_kernel(x_ref, mod_ref, g_ref, w_ref, b_ref, o_ref, *, scale):
    d = x_ref.shape[-1]
    mod = mod_ref[...]
    sh1, sc1 = mod[:, 0:d], mod[:, d:2 * d]
    h = (_rmsnorm(x_ref[...], g_ref[...]) * (1.0 + sc1) + sh1).astype(BF16)
    qkv = _dot(h, w_ref[...]) + b_ref[...]
    nq = SWA_HEADS * SWA_HEAD_DIM
    o_ref[:, 0:nq] = (qkv[:, 0:nq] * scale).astype(BF16)
    o_ref[:, nq:] = qkv[:, nq:].astype(BF16)


def _swa_window_order(t, axis):
    group = SWA_HEADS // SWA_KV_HEADS
    shape = t.shape
    t = t.reshape(shape[:axis] + (SWA_KV_HEADS, group, SWA_HEAD_DIM) + shape[axis + 1:])
    return jnp.swapaxes(t, axis, axis + 1).reshape(shape)


def _swa_proj(x, mods, layer, g, w_qkv, b_qkv):
    bsz, s, d = x.shape
    n = w_qkv.shape[1]
    tm = 512
    const = lambda b, i: (0, 0)
    return pl.pallas_call(
        functools.partial(_swa_proj_kernel,
                          scale=float(SWA_HEAD_DIM ** -0.5 * math.log2(math.e))),
        grid=(bsz, s // tm),
        in_specs=[
            pl.BlockSpec((None, tm, d), lambda b, i: (b, i, 0)),
            pl.BlockSpec((None, None, 1, mods.shape[-1]), lambda b, i: (layer, b, 0, 0)),
            pl.BlockSpec((1, d), const),
            pl.BlockSpec(w_qkv.shape, const),
            pl.BlockSpec((1, n), const),
        ],
        out_specs=pl.BlockSpec((None, tm, n), lambda b, i: (b, i, 0)),
        out_shape=jax.ShapeDtypeStruct((bsz, s, n), BF16),
        compiler_params=_params("parallel", "parallel"),
        name="swa_proj",
    )(x, mods, g.reshape(1, d), w_qkv.astype(BF16), b_qkv.reshape(1, n))


def _swa_attn_kernel(sink_ref, q_ref, kc_ref, vc_ref, kp_ref, vp_ref, o_ref, bias_sc, *, slopes):
    first_step = jnp.logical_and(pl.program_id(0) == 0, pl.program_id(1) == 0)
    w = WINDOW
    dh = SWA_HEAD_DIM
    n_kv = SWA_KV_HEADS
    group = SWA_HEADS // n_kv
    gw = n_kv * dh
    log2e = math.log2(math.e)

    @pl.when(first_step)
    def _():
        qpos = lax.broadcasted_iota(jnp.int32, (w, 2 * w), 0)
        kpos = lax.broadcasted_iota(jnp.int32, (w, 2 * w), 1)
        dist = w + qpos - kpos
        in_window = jnp.logical_and(dist >= 0, dist < w)
        dist_f = dist.astype(F32)
        for hd in range(SWA_HEADS):
            bias_sc[hd * w:(hd + 1) * w, :] = jnp.where(
                in_window, dist_f * (-slopes[hd] * log2e), NEG_INF)

    no_prev = jnp.where(pl.program_id(1) == 0, NEG_INF, 0.0)
    ones_cols = jnp.ones((2 * w, w), BF16)
    lane_group = lax.broadcasted_iota(jnp.int32, (w, n_kv * dh), 1) // dh
    low_half = lax.broadcasted_iota(jnp.int32, (w, w), 1) < dh
    half_rows = (n_kv // 2) * group * w
    sink = jnp.concatenate([jnp.full((w, w), sink_ref[hd] * log2e, F32)
                            for hd in range(SWA_HEADS)], axis=0)

    def band(cur_ref, prev_ref, n):
        if n == 0:
            return jnp.concatenate([prev_ref[...], cur_ref[0:w, :]], axis=0)
        return cur_ref[(n - 1) * w:(n + 1) * w, :]

    def stage_scores(n):
        lhs = []
        for g in range(n_kv):
            for a in range(group):
                qwin = q_ref[n * w:(n + 1) * w, a * gw:(a + 1) * gw]
                lhs.append(jnp.where(lane_group == g, qwin, jnp.zeros_like(qwin)))
        s = _dot_nt(jnp.concatenate(lhs, axis=0), band(kc_ref, kp_ref, n)) + bias_sc[...]
        if n == 0:
            s = jnp.concatenate([s[:, 0:w] + no_prev, s[:, w:2 * w]], axis=1)
        return s

    def stage_softmax(s):
        m = jnp.maximum(jnp.max(s, axis=-1, keepdims=True), sink)
        return jnp.exp2(s - jnp.tile(m, (1, 2))).astype(BF16), jnp.exp2(sink - m)

    def stage_out(n, p, e):
        vall = band(vc_ref, vp_ref, n)
        pv_a = _dot(p[0:half_rows], jnp.concatenate([vall[:, 0:w], ones_cols], axis=1))
        pv_b = _dot(p[half_rows:], jnp.concatenate([ones_cols, vall[:, w:2 * w]], axis=1))
        o_a = pv_a[:, 0:w] / (pv_a[:, w:2 * w] + e[0:half_rows])
        o_b = pv_b[:, w:2 * w] / (pv_b[:, 0:w] + e[half_rows:])
        for a in range(group):
            lo = jnp.where(low_half, o_a[a * w:(a + 1) * w], o_a[(group + a) * w:(group + a + 1) * w])
            hi = jnp.where(low_half, o_b[a * w:(a + 1) * w], o_b[(group + a) * w:(group + a + 1) * w])
            o_ref[n * w:(n + 1) * w, a * gw:(a + 1) * gw] = jnp.concatenate(
                [lo, hi], axis=1).astype(o_ref.dtype)

    n_blocks = q_ref.shape[0] // w
    scores, probs = {}, {}
    for step in range(n_blocks + 2):
        if step < n_blocks:
            scores[step] = stage_scores(step)
        if 1 <= step <= n_blocks:
            probs[step - 1] = stage_softmax(scores.pop(step - 1))
        if step >= 2:
            stage_out(step - 2, *probs.pop(step - 2))


def _swa_attn(qkv, sinks):
    bsz, s, _ = qkv.shape
    nq = SWA_HEADS * SWA_HEAD_DIM
    nkv = SWA_KV_HEADS * SWA_HEAD_DIM
    tm = 512
    per = tm // WINDOW
    slopes = tuple(float(v) for v in 2.0 ** (-8.0 * np.arange(1, SWA_HEADS + 1) / SWA_HEADS))
    kcol, vcol = nq // nkv, nq // nkv + 1
    prev = lambda b, i: jnp.maximum(i * per - 1, 0)
    return pl.pallas_call(
        functools.partial(_swa_attn_kernel, slopes=slopes),
        grid=(bsz, s // tm),
        in_specs=[
            pl.BlockSpec(memory_space=pltpu.SMEM),
            pl.BlockSpec((None, tm, nq), lambda b, i: (b, i, 0)),
            pl.BlockSpec((None, tm, nkv), lambda b, i: (b, i, kcol)),
            pl.BlockSpec((None, tm, nkv), lambda b, i: (b, i, vcol)),
            pl.BlockSpec((None, WINDOW, nkv), lambda b, i: (b, prev(b, i), kcol)),
            pl.BlockSpec((None, WINDOW, nkv), lambda b, i: (b, prev(b, i), vcol)),
        ],
        out_specs=pl.BlockSpec((None, tm, nq), lambda b, i: (b, i, 0)),
        out_shape=jax.ShapeDtypeStruct((bsz, s, nq), BF16),
        scratch_shapes=[pltpu.VMEM((SWA_HEADS * WINDOW, 2 * WINDOW), F32)],
        compiler_params=_params("arbitrary", "arbitrary"),
        name="swa_attn",
    )(sinks.astype(F32), qkv, qkv, qkv, qkv, qkv)


def kernel(x, c, positions, w_ada, b_ada, g_mix, g_mlp, mla_w_dq, mla_g_q, mla_w_uq, mla_w_dkv,
           mla_g_kv, mla_w_ukv, mla_w_o, swa_w_qkv, swa_b_qkv, swa_sinks, swa_w_o, swa_b_o,
           w_ff1, w_ff2, g_final):
    depth = w_ada.shape[0]
    bsz, s, d = x.shape
    mods = _adaln(c, w_ada, b_ada).reshape(depth, bsz, 1, w_ada.shape[-1])
    cos_t, sin_t = _rope_tables(positions)
    zero_bias = jnp.zeros((d,), F32)
    for i in range(depth):
        j = i // 2
        last = i == depth - 1
        if i % 2 == 0:
            q, k, v = _mla_proj(x, mods, i, g_mix[i], cos_t, sin_t, mla_w_dq[j], mla_g_q[j],
                                mla_w_uq[j], mla_w_dkv[j], mla_g_kv[j], mla_w_ukv[j])
            attn = _mla_flash(q, k, v)
            w_o, b_o = mla_w_o[j], zero_bias
        else:
            nq = SWA_HEADS * SWA_HEAD_DIM
            w_qkv = jnp.concatenate([_swa_window_order(swa_w_qkv[j][:, :nq], 1),
                                     swa_w_qkv[j][:, nq:]], axis=1)
            b_qkv = jnp.concatenate([_swa_window_order(swa_b_qkv[j][:nq], 0), swa_b_qkv[j][nq:]])
            qkv = _swa_proj(x, mods, i, g_mix[i], w_qkv, b_qkv)
            attn = _swa_attn(qkv, swa_sinks[j])
            w_o, b_o = _swa_window_order(swa_w_o[j], 0), swa_b_o[j]
        x = _post_mlp(x, attn, mods, i, w_o, b_o, g_mlp[i], w_ff1, w_ff2, g_final, last)
    return x
```

```python
import functools
import math

import numpy as np
import jax
import jax.numpy as jnp
from jax import lax
from jax.experimental import pallas as pl
from jax.experimental.pallas import tpu as pltpu

F32 = jnp.float32
BF16 = jnp.bfloat16

EPS = 1e-6
ROPE_THETA = 10000.0
MLA_HEADS = 8
QK_NOPE = 128
QK_ROPE = 64
V_DIM = 128
KV_LORA = 256
SWA_HEADS = 16
SWA_KV_HEADS = 4
SWA_HEAD_DIM = 64
WINDOW = 128

LANES = 128
QK_PAD = 2 * LANES
ONES_ROWS = 16
VMEM_LIMIT = 56 * 1024 * 1024

NEG_INF = float("-inf")


def _params(*sem):
    return pltpu.CompilerParams(dimension_semantics=sem, vmem_limit_bytes=VMEM_LIMIT)


def _rmsnorm(x, g):
    y = x * lax.rsqrt(jnp.mean(x * x, axis=-1, keepdims=True) + EPS)
    return y * g


def _dot(a, b):
    return jnp.dot(a, b, preferred_element_type=F32)


def _dot_nt(a, b):
    return lax.dot_general(a, b, (((1,), (1,)), ((), ())), preferred_element_type=F32)


def _adaln_kernel(ct_ref, w_ref, b_ref, o_ref):
    ct = ct_ref[...]
    cond = ct * jax.nn.sigmoid(ct)
    w = w_ref[...]
    rows = [jnp.sum(w * cond[:, b:b + 1], axis=0, keepdims=True)
            for b in range(ct.shape[1])]
    o_ref[...] = jnp.concatenate(rows, axis=0) + b_ref[...]


def _adaln(c, w_ada, b_ada):
    depth, d, n = w_ada.shape
    bsz = c.shape[0]
    tn = 1024
    return pl.pallas_call(
        _adaln_kernel,
        grid=(depth, n // tn),
        in_specs=[
            pl.BlockSpec((d, bsz), lambda l, j: (0, 0)),
            pl.BlockSpec((None, d, tn), lambda l, j: (l, 0, j)),
            pl.BlockSpec((None, 1, tn), lambda l, j: (l, 0, j)),
        ],
        out_specs=pl.BlockSpec((None, bsz, tn), lambda l, j: (l, 0, j)),
        out_shape=jax.ShapeDtypeStruct((depth, bsz, n), F32),
        compiler_params=_params("parallel", "parallel"),
        name="adaln",
    )(c.T, w_ada, b_ada.reshape(depth, 1, n))


def _rope_table_kernel(pos_ref, inv_ref, cos_ref, sin_ref):
    ang = pos_ref[...].astype(F32) * inv_ref[...]
    cos_ref[...] = jnp.cos(ang)
    sin_ref[...] = jnp.sin(ang)


def _rope_tables(positions):
    n = positions.size
    half = QK_ROPE // 2
    per_row = LANES // half
    inv_freq = ROPE_THETA ** (-jnp.arange(half, dtype=F32) / half)
    inv = jnp.tile(inv_freq, per_row).reshape(1, LANES)
    pos_rows = jnp.repeat(positions.reshape(n // per_row, per_row), half, axis=1)
    rows = n // per_row
    tm = 512
    cos_t, sin_t = pl.pallas_call(
        _rope_table_kernel,
        grid=(rows // tm,),
        in_specs=[pl.BlockSpec((tm, LANES), lambda i: (i, 0)),
                  pl.BlockSpec((1, LANES), lambda i: (0, 0))],
        out_specs=[pl.BlockSpec((tm, LANES), lambda i: (i, 0)),
                   pl.BlockSpec((tm, LANES), lambda i: (i, 0))],
        out_shape=[jax.ShapeDtypeStruct((rows, LANES), F32)] * 2,
        compiler_params=_params("parallel"),
        name="rope_table",
    )(pos_rows, inv)
    return cos_t.reshape(n, half), sin_t.reshape(n, half)


def _mla_proj_kernel(x_ref, mod_ref, g_ref, cos_ref, sin_ref, wdq_ref, gq_ref, wuq_ref,
                     wdkv_ref, gkv_ref, wuk_ref, wuvt_ref, q_ref, k_ref, vt_ref, *, scale,
                     row_chunk):
    d = x_ref.shape[-1]
    mod = mod_ref[...]
    sh1, sc1 = mod[:, 0:d], mod[:, d:2 * d]
    reps = LANES // cos_ref.shape[-1]
    n_nope = MLA_HEADS * QK_NOPE
    n_rope = MLA_HEADS * QK_ROPE
    n_chunks = x_ref.shape[0] // row_chunk
    lane = lax.broadcasted_iota(jnp.int32, (row_chunk, LANES), 1)
    low = lane < QK_ROPE

    def down(c):
        rows = slice(c * row_chunk, (c + 1) * row_chunk)
        h = (_rmsnorm(x_ref[rows, :], g_ref[...]) * (1.0 + sc1) + sh1).astype(BF16)
        return _dot(h, wdq_ref[...]), _dot(h, wdkv_ref[...])

    def up(cq_raw, dkv):
        cq = _rmsnorm(cq_raw, gq_ref[...]).astype(BF16)
        ckv = _rmsnorm(dkv[:, 0:KV_LORA], gkv_ref[...]).astype(BF16)
        return _dot(cq, wuq_ref[...]), _dot(ckv, wuk_ref[...]), _dot_nt(wuvt_ref[...], ckv)

    def emit(c, dkv, qe, kn, vt):
        rows = slice(c * row_chunk, (c + 1) * row_chunk)
        cos = jnp.tile(cos_ref[rows, :], (1, reps))
        sin = jnp.tile(sin_ref[rows, :], (1, reps))
        for hd in range(MLA_HEADS):
            q_ref[hd, rows, 0:LANES] = (qe[:, hd * LANES:(hd + 1) * LANES] * scale).astype(BF16)
            j = hd // 2
            a = qe[:, n_nope + j * LANES:n_nope + (j + 1) * LANES]
            b = qe[:, n_nope + n_rope + j * LANES:n_nope + n_rope + (j + 1) * LANES]
            r = (a * cos + b * sin) * scale
            keep = low if hd % 2 == 0 else jnp.logical_not(low)
            q_ref[hd, rows, LANES:QK_PAD] = jnp.where(keep, r, 0.0).astype(BF16)
        kr = (dkv[:, KV_LORA:KV_LORA + LANES] * cos
              + dkv[:, KV_LORA + LANES:KV_LORA + 2 * LANES] * sin)
        kr_even = jnp.where(low, kr, 0.0).astype(BF16)
        kr_odd = jnp.where(low, 0.0, kr).astype(BF16)
        for hd in range(MLA_HEADS):
            k_ref[hd, rows, 0:LANES] = kn[:, hd * QK_NOPE:(hd + 1) * QK_NOPE].astype(BF16)
            k_ref[hd, rows, LANES:QK_PAD] = kr_even if hd % 2 == 0 else kr_odd
            vt_ref[hd, :, rows] = vt[hd * V_DIM:(hd + 1) * V_DIM, :].astype(BF16)

    downs, ups = {}, {}
    for step in range(n_chunks + 2):
        if step < n_chunks:
            downs[step] = down(step)
        if 1 <= step <= n_chunks:
            ups[step - 1] = up(*downs[step - 1])
        if step >= 2:
            emit(step - 2, downs.pop(step - 2)[1], *ups.pop(step - 2))


def _rot_half(w):
    half = w.shape[-1] // 2
    return jnp.concatenate([-w[..., half:], w[..., :half]], axis=-1)


def _mla_proj(x, mods, layer, g, cos_t, sin_t, w_dq, g_q, w_uq, w_dkv, g_kv, w_ukv):
    bsz, s, d = x.shape
    hh = MLA_HEADS
    q_lora = w_dq.shape[1]
    tm, row_chunk = 1024, 256
    wq3 = w_uq.reshape(q_lora, hh, QK_NOPE + QK_ROPE)
    wq_rope = wq3[:, :, QK_NOPE:]
    wuq_ext = jnp.concatenate([
        wq3[:, :, :QK_NOPE].reshape(q_lora, hh * QK_NOPE),
        wq_rope.reshape(q_lora, hh * QK_ROPE),
        _rot_half(wq_rope).reshape(q_lora, hh * QK_ROPE)], axis=1).astype(BF16)
    wkr = w_dkv[:, KV_LORA:]
    wkr_rot = _rot_half(wkr)
    wdkv_ext = jnp.concatenate([w_dkv[:, :KV_LORA], wkr, wkr, wkr_rot, wkr_rot], axis=1).astype(BF16)
    wkv3 = w_ukv.reshape(KV_LORA, hh, QK_NOPE + V_DIM)
    wuk = wkv3[:, :, :QK_NOPE].reshape(KV_LORA, hh * QK_NOPE).astype(BF16)
    wuvt = wkv3[:, :, QK_NOPE:].reshape(KV_LORA, hh * V_DIM).T.astype(BF16)
    scale = float((QK_NOPE + QK_ROPE) ** -0.5 * math.log2(math.e))

    const = lambda b, i: (0, 0)
    tok = lambda b, i: (b * (s // tm) + i, 0)
    out_map = lambda b, i: (b, 0, i, 0)
    return pl.pallas_call(
        functools.partial(_mla_proj_kernel, scale=scale, row_chunk=row_chunk),
        grid=(bsz, s // tm),
        in_specs=[
            pl.BlockSpec((None, tm, d), lambda b, i: (b, i, 0)),
            pl.BlockSpec((None, None, 1, mods.shape[-1]), lambda b, i: (layer, b, 0, 0)),
            pl.BlockSpec((1, d), const),
            pl.BlockSpec((tm, cos_t.shape[-1]), tok),
            pl.BlockSpec((tm, sin_t.shape[-1]), tok),
            pl.BlockSpec(w_dq.shape, const),
            pl.BlockSpec((1, q_lora), const),
            pl.BlockSpec(wuq_ext.shape, const),
            pl.BlockSpec(wdkv_ext.shape, const),
            pl.BlockSpec((1, KV_LORA), const),
            pl.BlockSpec(wuk.shape, const),
            pl.BlockSpec(wuvt.shape, const),
        ],
        out_specs=[
            pl.BlockSpec((None, hh, tm, QK_PAD), out_map),
            pl.BlockSpec((None, hh, tm, QK_PAD), out_map),
            pl.BlockSpec((None, hh, V_DIM, tm), lambda b, i: (b, 0, 0, i)),
        ],
        out_shape=[
            jax.ShapeDtypeStruct((bsz, hh, s, QK_PAD), BF16),
            jax.ShapeDtypeStruct((bsz, hh, s, QK_PAD), BF16),
            jax.ShapeDtypeStruct((bsz, hh, V_DIM, s), BF16),
        ],
        compiler_params=_params("parallel", "parallel"),
        name="mla_proj",
    )(x, mods, g.reshape(1, d), cos_t, sin_t, w_dq.astype(BF16), g_q.reshape(1, q_lora),
      wuq_ext, wdkv_ext, g_kv.reshape(1, KV_LORA), wuk, wuvt)


def _mla_flash_kernel(q_ref, k_ref, vt_ref, o_ref, vext_sc, s0_sc, s1_sc, m_sc, acc_sc, *, tq):
    qi = pl.program_id(2)
    half = tq // 2

    @pl.when(qi == 0)
    def _():
        for t in range(vext_sc.shape[0]):
            vext_sc[t, 0:V_DIM, :] = vt_ref[:, t * tq:(t + 1) * tq]
            vext_sc[t, V_DIM:, :] = jnp.ones((ONES_ROWS, tq), BF16)

    m_sc[...] = jnp.full(m_sc.shape, NEG_INF, F32)
    acc_sc[...] = jnp.zeros(acc_sc.shape, F32)

    def scores(t, s_ref):
        k = k_ref[pl.ds(pl.multiple_of(t * tq, tq), tq), :]
        s_ref[...] = _dot_nt(k, q_ref[...])

    def scores_diag(s_ref):
        base = pl.multiple_of(qi * tq, tq)
        s_ref[0:half, :] = _dot_nt(k_ref[pl.ds(base, half), :], q_ref[...])
        s_ref[half:tq, half:tq] = _dot_nt(k_ref[pl.ds(base + half, half), :], q_ref[half:tq, :])

    def softmax_pv(s, vt, cols):
        m_prev = m_sc[:, cols]
        m_new = jnp.maximum(m_prev, jnp.max(s, axis=0, keepdims=True))
        alpha = jnp.exp2(m_prev - m_new)
        p = jnp.exp2(s - m_new).astype(BF16)
        acc_sc[:, cols] = alpha * acc_sc[:, cols] + _dot(vt, p)
        m_sc[:, cols] = m_new

    def update(t, s_ref):
        softmax_pv(s_ref[...], vext_sc[t], slice(0, tq))

    def update_diag(s_ref):
        causal = (lax.broadcasted_iota(jnp.int32, (half, half), 0)
                  <= lax.broadcasted_iota(jnp.int32, (half, half), 1))
        vt = vext_sc[qi]
        s_top = s_ref[0:half, :]
        s_top = jnp.concatenate([jnp.where(causal, s_top[:, 0:half], NEG_INF), s_top[:, half:tq]],
                                axis=1)
        softmax_pv(s_top, vt[:, 0:half], slice(0, tq))
        s_bot = jnp.where(causal, s_ref[half:tq, half:tq], NEG_INF)
        softmax_pv(s_bot, vt[:, half:tq], slice(half, tq))

    def pair(u, carry):
        t = 2 * u
        scores(t + 1, s1_sc)
        update(t, s0_sc)
        scores(t + 2, s0_sc)
        update(t + 1, s1_sc)
        return carry

    @pl.when(qi > 0)
    def _():
        scores(0, s0_sc)

    odd = qi % 2 == 1
    lax.fori_loop(0, jnp.where(odd, qi // 2, jnp.maximum(qi // 2 - 1, 0)), pair, 0)

    @pl.when(odd)
    def _():
        scores_diag(s1_sc)
        update(qi - 1, s0_sc)
        update_diag(s1_sc)

    @pl.when(jnp.logical_and(jnp.logical_not(odd), qi > 0))
    def _():
        scores(qi - 1, s1_sc)
        update(qi - 2, s0_sc)
        scores_diag(s0_sc)
        update(qi - 1, s1_sc)
        update_diag(s0_sc)

    @pl.when(qi == 0)
    def _():
        scores_diag(s0_sc)
        update_diag(s0_sc)

    acc = acc_sc[...]
    o_t = acc[0:V_DIM, :] / acc[V_DIM:V_DIM + 1, :]
    o_ref[...] = o_t.T.astype(o_ref.dtype)


def _mla_flash(q, k, vt):
    bsz, hh, s, _ = q.shape
    tq = 1024
    return pl.pallas_call(
        functools.partial(_mla_flash_kernel, tq=tq),
        grid=(bsz, hh, s // tq),
        in_specs=[
            pl.BlockSpec((None, None, tq, QK_PAD), lambda b, h, i: (b, h, i, 0)),
            pl.BlockSpec((None, None, s, QK_PAD), lambda b, h, i: (b, h, 0, 0)),
            pl.BlockSpec((None, None, V_DIM, s), lambda b, h, i: (b, h, 0, 0)),
        ],
        out_specs=pl.BlockSpec((None, tq, V_DIM), lambda b, h, i: (b, i, h)),
        out_shape=jax.ShapeDtypeStruct((bsz, s, hh * V_DIM), BF16),
        scratch_shapes=[pltpu.VMEM((s // tq, V_DIM + ONES_ROWS, tq), BF16),
                        pltpu.VMEM((tq, tq), F32), pltpu.VMEM((tq, tq), F32),
                        pltpu.VMEM((1, tq), F32), pltpu.VMEM((V_DIM + ONES_ROWS, tq), F32)],
        compiler_params=_params("parallel", "parallel", "arbitrary"),
        name="mla_flash",
    )(q, k, vt)


def _post_mlp_kernel(x_ref, a_ref, mod_ref, wo_ref, bo_ref, g_ref, w1_ref, w2_ref, gf_ref,
                     out_ref, x1_sc, h_sc, acc_sc, *, final_norm, n_ff, row_chunk):
    j = pl.program_id(2)
    d = x_ref.shape[-1]
    tm = x_ref.shape[0]

    def step(first, last):
        mod = mod_ref[...]
        gt1, sh2, sc2 = mod[:, 2 * d:3 * d], mod[:, 3 * d:4 * d], mod[:, 4 * d:5 * d]
        gt2 = mod[:, 5 * d:6 * d]
        w1 = w1_ref[...].astype(BF16)
        w2 = w2_ref[...].astype(BF16)
        for r in range(tm // row_chunk):
            rows = slice(r * row_chunk, (r + 1) * row_chunk)
            if first:
                y = _dot(a_ref[rows, :], wo_ref[...]) + bo_ref[...]
                x1 = x_ref[rows, :] + gt1 * y
                h = (_rmsnorm(x1, g_ref[...]) * (1.0 + sc2) + sh2).astype(BF16)
                h_sc[rows, :] = h
                if not last:
                    x1_sc[rows, :] = x1
            else:
                h = h_sc[rows, :]
            u = jnp.square(jnp.maximum(_dot(h, w1), 0.0)).astype(BF16)
            acc = _dot(u, w2)
            if not first:
                acc = acc_sc[rows, :] + acc
            if last:
                x2 = (x1 if first else x1_sc[rows, :]) + gt2 * acc
                if final_norm:
                    x2 = _rmsnorm(x2, gf_ref[...])
                out_ref[rows, :] = x2
            else:
                acc_sc[rows, :] = acc

    pl.when(j == 0)(functools.partial(step, True, n_ff == 1))
    if n_ff > 1:
        pl.when(j == n_ff - 1)(functools.partial(step, False, True))
    if n_ff > 2:
        pl.when(jnp.logical_and(j > 0, j < n_ff - 1))(functools.partial(step, False, False))


def _post_mlp(x, attn, mods, layer, w_o, b_o, g_mlp, w_ff1, w_ff2, g_final, final_norm):
    bsz, s, d = x.shape
    dff = w_ff1.shape[-1]
    tm, tf, row_chunk = 1024, 1024, 512
    const = lambda b, i, j: (0, 0)
    tok = lambda b, i, j: (b, i, 0)
    return pl.pallas_call(
        functools.partial(_post_mlp_kernel, final_norm=final_norm, n_ff=dff // tf,
                          row_chunk=row_chunk),
        grid=(bsz, s // tm, dff // tf),
        in_specs=[
            pl.BlockSpec((None, tm, d), tok),
            pl.BlockSpec((None, tm, attn.shape[-1]), tok),
            pl.BlockSpec((None, None, 1, mods.shape[-1]), lambda b, i, j: (layer, b, 0, 0)),
            pl.BlockSpec(w_o.shape, const),
            pl.BlockSpec((1, d), const),
            pl.BlockSpec((1, d), const),
            pl.BlockSpec((None, d, tf), lambda b, i, j: (layer, 0, j)),
            pl.BlockSpec((None, tf, d), lambda b, i, j: (layer, j, 0)),
            pl.BlockSpec((1, d), const),
        ],
        out_specs=pl.BlockSpec((None, tm, d), tok),
        out_shape=jax.ShapeDtypeStruct((bsz, s, d), F32),
        scratch_shapes=[pltpu.VMEM((tm, d), F32), pltpu.VMEM((tm, d), BF16),
                        pltpu.VMEM((tm, d), F32)],
        compiler_params=_params("parallel", "parallel", "arbitrary"),
        name="post_mlp_final" if final_norm else "post_mlp",
    )(x, attn, mods, w_o.astype(BF16), b_o.reshape(1, d), g_mlp.reshape(1, d),
      w_ff1, w_ff2, g_final.reshape(1, d))


def _swa_proj_kernel(x_ref, mod_ref, g_ref, w_ref, b_ref, o_ref, *, scale):
    d = x_ref.shape[-1]
    mod = mod_ref[...]
    sh1, sc1 = mod[:, 0:d], mod[:, d:2 * d]
    h = (_rmsnorm(x_ref[...], g_ref[...]) * (1.0 + sc1) + sh1).astype(BF16)
    qkv = _dot(h, w_ref[...]) + b_ref[...]
    nq = SWA_HEADS * SWA_HEAD_DIM
    o_ref[:, 0:nq] = (qkv[:, 0:nq] * scale).astype(BF16)
    o_ref[:, nq:] = qkv[:, nq:].astype(BF16)


def _swa_window_order(t, axis):
    group = SWA_HEADS // SWA_KV_HEADS
    shape = t.shape
    t = t.reshape(shape[:axis] + (SWA_KV_HEADS, group, SWA_HEAD_DIM) + shape[axis + 1:])
    return jnp.swapaxes(t, axis, axis + 1).reshape(shape)


def _swa_proj(x, mods, layer, g, w_qkv, b_qkv):
    bsz, s, d = x.shape
    n = w_qkv.shape[1]
    tm = 512
    const = lambda b, i: (0, 0)
    return pl.pallas_call(
        functools.partial(_swa_proj_kernel,
                          scale=float(SWA_HEAD_DIM ** -0.5 * math.log2(math.e))),
        grid=(bsz, s // tm),
        in_specs=[
            pl.BlockSpec((None, tm, d), lambda b, i: (b, i, 0)),
            pl.BlockSpec((None, None, 1, mods.shape[-1]), lambda b, i: (layer, b, 0, 0)),
            pl.BlockSpec((1, d), const),
            pl.BlockSpec(w_qkv.shape, const),
            pl.BlockSpec((1, n), const),
        ],
        out_specs=pl.BlockSpec((None, tm, n), lambda b, i: (b, i, 0)),
        out_shape=jax.ShapeDtypeStruct((bsz, s, n), BF16),
        compiler_params=_params("parallel", "parallel"),
        name="swa_proj",
    )(x, mods, g.reshape(1, d), w_qkv.astype(BF16), b_qkv.reshape(1, n))


def _swa_attn_kernel(sink_ref, q_ref, kc_ref, vc_ref, kp_ref, vp_ref, o_ref, bias_sc, *, slopes):
    first_step = jnp.logical_and(pl.program_id(0) == 0, pl.program_id(1) == 0)
    w = WINDOW
    dh = SWA_HEAD_DIM
    n_kv = SWA_KV_HEADS
    group = SWA_HEADS // n_kv
    gw = n_kv * dh
    log2e = math.log2(math.e)

    @pl.when(first_step)
    def _():
        qpos = lax.broadcasted_iota(jnp.int32, (w, 2 * w), 0)
        kpos = lax.broadcasted_iota(jnp.int32, (w, 2 * w), 1)
        dist = w + qpos - kpos
        in_window = jnp.logical_and(dist >= 0, dist < w)
        dist_f = dist.astype(F32)
        for hd in range(SWA_HEADS):
            bias_sc[hd * w:(hd + 1) * w, :] = jnp.where(
                in_window, dist_f * (-slopes[hd] * log2e), NEG_INF)

    no_prev = jnp.where(pl.program_id(1) == 0, NEG_INF, 0.0)
    ones_cols = jnp.ones((2 * w, w), BF16)
    lane_group = lax.broadcasted_iota(jnp.int32, (w, n_kv * dh), 1) // dh
    low_half = lax.broadcasted_iota(jnp.int32, (w, w), 1) < dh
    half_rows = (n_kv // 2) * group * w
    sink = jnp.concatenate([jnp.full((w, w), sink_ref[hd] * log2e, F32)
                            for hd in range(SWA_HEADS)], axis=0)

    def band(cur_ref, prev_ref, n):
        if n == 0:
            return jnp.concatenate([prev_ref[...], cur_ref[0:w, :]], axis=0)
        return cur_ref[(n - 1) * w:(n + 1) * w, :]

    def stage_scores(n):
        lhs = []
        for g in range(n_kv):
            for a in range(group):
                qwin = q_ref[n * w:(n + 1) * w, a * gw:(a + 1) * gw]
                lhs.append(jnp.where(lane_group == g, qwin, jnp.zeros_like(qwin)))
        s = _dot_nt(jnp.concatenate(lhs, axis=0), band(kc_ref, kp_ref, n)) + bias_sc[...]
        if n == 0:
            s = jnp.concatenate([s[:, 0:w] + no_prev, s[:, w:2 * w]], axis=1)
        return s

    def stage_softmax(s):
        m = jnp.maximum(jnp.max(s, axis=-1, keepdims=True), sink)
        return jnp.exp2(s - jnp.tile(m, (1, 2))).astype(BF16), jnp.exp2(sink - m)

    def stage_out(n, p, e):
        vall = band(vc_ref, vp_ref, n)
        pv_a = _dot(p[0:half_rows], jnp.concatenate([vall[:, 0:w], ones_cols], axis=1))
        pv_b = _dot(p[half_rows:], jnp.concatenate([ones_cols, vall[:, w:2 * w]], axis=1))
        o_a = pv_a[:, 0:w] / (pv_a[:, w:2 * w] + e[0:half_rows])
        o_b = pv_b[:, w:2 * w] / (pv_b[:, 0:w] + e[half_rows:])
        for a in range(group):
            lo = jnp.where(low_half, o_a[a * w:(a + 1) * w], o_a[(group + a) * w:(group + a + 1) * w])
            hi = jnp.where(low_half, o_b[a * w:(a + 1) * w], o_b[(group + a) * w:(group + a + 1) * w])
            o_ref[n * w:(n + 1) * w, a * gw:(a + 1) * gw] = jnp.concatenate(
                [lo, hi], axis=1).astype(o_ref.dtype)

    n_blocks = q_ref.shape[0] // w
    scores, probs = {}, {}
    for step in range(n_blocks + 2):
        if step < n_blocks:
            scores[step] = stage_scores(step)
        if 1 <= step <= n_blocks:
            probs[step - 1] = stage_softmax(scores.pop(step - 1))
        if step >= 2:
            stage_out(step - 2, *probs.pop(step - 2))


def _swa_attn(qkv, sinks):
    bsz, s, _ = qkv.shape
    nq = SWA_HEADS * SWA_HEAD_DIM
    nkv = SWA_KV_HEADS * SWA_HEAD_DIM
    tm = 512
    per = tm // WINDOW
    slopes = tuple(float(v) for v in 2.0 ** (-8.0 * np.arange(1, SWA_HEADS + 1) / SWA_HEADS))
    kcol, vcol = nq // nkv, nq // nkv + 1
    prev = lambda b, i: jnp.maximum(i * per - 1, 0)
    return pl.pallas_call(
        functools.partial(_swa_attn_kernel, slopes=slopes),
        grid=(bsz, s // tm),
        in_specs=[
            pl.BlockSpec(memory_space=pltpu.SMEM),
            pl.BlockSpec((None, tm, nq), lambda b, i: (b, i, 0)),
            pl.BlockSpec((None, tm, nkv), lambda b, i: (b, i, kcol)),
            pl.BlockSpec((None, tm, nkv), lambda b, i: (b, i, vcol)),
            pl.BlockSpec((None, WINDOW, nkv), lambda b, i: (b, prev(b, i), kcol)),
            pl.BlockSpec((None, WINDOW, nkv), lambda b, i: (b, prev(b, i), vcol)),
        ],
        out_specs=pl.BlockSpec((None, tm, nq), lambda b, i: (b, i, 0)),
        out_shape=jax.ShapeDtypeStruct((bsz, s, nq), BF16),
        scratch_shapes=[pltpu.VMEM((SWA_HEADS * WINDOW, 2 * WINDOW), F32)],
        compiler_params=_params("arbitrary", "arbitrary"),
        name="swa_attn",
    )(sinks.astype(F32), qkv, qkv, qkv, qkv, qkv)


def kernel(x, c, positions, w_ada, b_ada, g_mix, g_mlp, mla_w_dq, mla_g_q, mla_w_uq, mla_w_dkv,
           mla_g_kv, mla_w_ukv, mla_w_o, swa_w_qkv, swa_b_qkv, swa_sinks, swa_w_o, swa_b_o,
           w_ff1, w_ff2, g_final):
    depth = w_ada.shape[0]
    bsz, s, d = x.shape
    mods = _adaln(c, w_ada, b_ada).reshape(depth, bsz, 1, w_ada.shape[-1])
    cos_t, sin_t = _rope_tables(positions)
    zero_bias = jnp.zeros((d,), F32)
    for i in range(depth):
        j = i // 2
        last = i == depth - 1
        if i % 2 == 0:
            q, k, v = _mla_proj(x, mods, i, g_mix[i], cos_t, sin_t, mla_w_dq[j], mla_g_q[j],
                                mla_w_uq[j], mla_w_dkv[j], mla_g_kv[j], mla_w_ukv[j])
            attn = _mla_flash(q, k, v)
            w_o, b_o = mla_w_o[j], zero_bias
        else:
            nq = SWA_HEADS * SWA_HEAD_DIM
            w_qkv = jnp.concatenate([_swa_window_order(swa_w_qkv[j][:, :nq], 1),
                                     swa_w_qkv[j][:, nq:]], axis=1)
            b_qkv = jnp.concatenate([_swa_window_order(swa_b_qkv[j][:nq], 0), swa_b_qkv[j][nq:]])
            qkv = _swa_proj(x, mods, i, g_mix[i], w_qkv, b_qkv)
            attn = _swa_attn(qkv, swa_sinks[j])
            w_o, b_o = _swa_window_order(swa_w_o[j], 0), swa_b_o[j]
        x = _post_mlp(x, attn, mods, i, w_o, b_o, g_mlp[i], w_ff1, w_ff2, g_final, last)
    return x
```

```python
import functools
import math

import numpy as np
import jax
import jax.numpy as jnp
from jax import lax
from jax.experimental import pallas as pl
from jax.experimental.pallas import tpu as pltpu

F32 = jnp.float32
BF16 = jnp.bfloat16

EPS = 1e-6
ROPE_THETA = 10000.0
MLA_HEADS = 8
QK_NOPE = 128
QK_ROPE = 64
V_DIM = 128
KV_LORA = 256
SWA_HEADS = 16
SWA_KV_HEADS = 4
SWA_HEAD_DIM = 64
WINDOW = 128

LANES = 128
QK_PAD = 2 * LANES
ONES_ROWS = 16
VMEM_LIMIT = 56 * 1024 * 1024

NEG_INF = float("-inf")


def _params(*sem):
    return pltpu.CompilerParams(dimension_semantics=sem, vmem_limit_bytes=VMEM_LIMIT)


def _rmsnorm(x, g):
    y = x * lax.rsqrt(jnp.mean(x * x, axis=-1, keepdims=True) + EPS)
    return y * g


def _dot(a, b):
    return jnp.dot(a, b, preferred_element_type=F32)


def _dot_nt(a, b):
    return lax.dot_general(a, b, (((1,), (1,)), ((), ())), preferred_element_type=F32)


def _adaln_kernel(ct_ref, w_ref, b_ref, o_ref):
    ct = ct_ref[...]
    cond = ct * jax.nn.sigmoid(ct)
    w = w_ref[...]
    rows = [jnp.sum(w * cond[:, b:b + 1], axis=0, keepdims=True)
            for b in range(ct.shape[1])]
    o_ref[...] = jnp.concatenate(rows, axis=0) + b_ref[...]


def _adaln(c, w_ada, b_ada):
    depth, d, n = w_ada.shape
    bsz = c.shape[0]
    tn = 1024
    return pl.pallas_call(
        _adaln_kernel,
        grid=(depth, n // tn),
        in_specs=[
            pl.BlockSpec((d, bsz), lambda l, j: (0, 0)),
            pl.BlockSpec((None, d, tn), lambda l, j: (l, 0, j)),
            pl.BlockSpec((None, 1, tn), lambda l, j: (l, 0, j)),
        ],
        out_specs=pl.BlockSpec((None, bsz, tn), lambda l, j: (l, 0, j)),
        out_shape=jax.ShapeDtypeStruct((depth, bsz, n), F32),
        compiler_params=_params("parallel", "parallel"),
        name="adaln",
    )(c.T, w_ada, b_ada.reshape(depth, 1, n))


def _rope_table_kernel(pos_ref, inv_ref, cos_ref, sin_ref):
    ang = pos_ref[...].astype(F32) * inv_ref[...]
    cos_ref[...] = jnp.cos(ang)
    sin_ref[...] = jnp.sin(ang)


def _rope_tables(positions):
    n = positions.size
    half = QK_ROPE // 2
    per_row = LANES // half
    inv_freq = ROPE_THETA ** (-jnp.arange(half, dtype=F32) / half)
    inv = jnp.tile(inv_freq, per_row).reshape(1, LANES)
    pos_rows = jnp.repeat(positions.reshape(n // per_row, per_row), half, axis=1)
    rows = n // per_row
    tm = 512
    cos_t, sin_t = pl.pallas_call(
        _rope_table_kernel,
        grid=(rows // tm,),
        in_specs=[pl.BlockSpec((tm, LANES), lambda i: (i, 0)),
                  pl.BlockSpec((1, LANES), lambda i: (0, 0))],
        out_specs=[pl.BlockSpec((tm, LANES), lambda i: (i, 0)),
                   pl.BlockSpec((tm, LANES), lambda i: (i, 0))],
        out_shape=[jax.ShapeDtypeStruct((rows, LANES), F32)] * 2,
        compiler_params=_params("parallel"),
        name="rope_table",
    )(pos_rows, inv)
    return cos_t.reshape(n, half), sin_t.reshape(n, half)


def _mla_proj_kernel(x_ref, mod_ref, g_ref, cos_ref, sin_ref, wdq_ref, gq_ref, wuq_ref,
                     wdkv_ref, gkv_ref, wuk_ref, wuvt_ref, q_ref, k_ref, vt_ref, *, scale,
                     row_chunk):
    d = x_ref.shape[-1]
    mod = mod_ref[...]
    sh1, sc1 = mod[:, 0:d], mod[:, d:2 * d]
    reps = LANES // cos_ref.shape[-1]
    n_nope = MLA_HEADS * QK_NOPE
    n_rope = MLA_HEADS * QK_ROPE
    n_chunks = x_ref.shape[0] // row_chunk
    lane = lax.broadcasted_iota(jnp.int32, (row_chunk, LANES), 1)
    low = lane < QK_ROPE

    def down(c):
        rows = slice(c * row_chunk, (c + 1) * row_chunk)
        h = (_rmsnorm(x_ref[rows, :], g_ref[...]) * (1.0 + sc1) + sh1).astype(BF16)
        return _dot(h, wdq_ref[...]), _dot(h, wdkv_ref[...])

    def up(cq_raw, dkv):
        cq = _rmsnorm(cq_raw, gq_ref[...]).astype(BF16)
        ckv = _rmsnorm(dkv[:, 0:KV_LORA], gkv_ref[...]).astype(BF16)
        return _dot(cq, wuq_ref[...]), _dot(ckv, wuk_ref[...]), _dot_nt(wuvt_ref[...], ckv)

    def emit(c, dkv, qe, kn, vt):
        rows = slice(c * row_chunk, (c + 1) * row_chunk)
        cos = jnp.tile(cos_ref[rows, :], (1, reps))
        sin = jnp.tile(sin_ref[rows, :], (1, reps))
        for hd in range(MLA_HEADS):
            q_ref[hd, rows, 0:LANES] = (qe[:, hd * LANES:(hd + 1) * LANES] * scale).astype(BF16)
            j = hd // 2
            a = qe[:, n_nope + j * LANES:n_nope + (j + 1) * LANES]
            b = qe[:, n_nope + n_rope + j * LANES:n_nope + n_rope + (j + 1) * LANES]
            r = (a * cos + b * sin) * scale
            keep = low if hd % 2 == 0 else jnp.logical_not(low)
            q_ref[hd, rows, LANES:QK_PAD] = jnp.where(keep, r, 0.0).astype(BF16)
        kr = (dkv[:, KV_LORA:KV_LORA + LANES] * cos
              + dkv[:, KV_LORA + LANES:KV_LORA + 2 * LANES] * sin)
        kr_even = jnp.where(low, kr, 0.0).astype(BF16)
        kr_odd = jnp.where(low, 0.0, kr).astype(BF16)
        for hd in range(MLA_HEADS):
            k_ref[hd, rows, 0:LANES] = kn[:, hd * QK_NOPE:(hd + 1) * QK_NOPE].astype(BF16)
            k_ref[hd, rows, LANES:QK_PAD] = kr_even if hd % 2 == 0 else kr_odd
            vt_ref[hd, :, rows] = vt[hd * V_DIM:(hd + 1) * V_DIM, :].astype(BF16)

    downs, ups = {}, {}
    for step in range(n_chunks + 2):
        if step < n_chunks:
            downs[step] = down(step)
        if 1 <= step <= n_chunks:
            ups[step - 1] = up(*downs[step - 1])
        if step >= 2:
            emit(step - 2, downs.pop(step - 2)[1], *ups.pop(step - 2))


def _rot_half(w):
    half = w.shape[-1] // 2
    return jnp.concatenate([-w[..., half:], w[..., :half]], axis=-1)


def _mla_proj(x, mods, layer, g, cos_t, sin_t, w_dq, g_q, w_uq, w_dkv, g_kv, w_ukv):
    bsz, s, d = x.shape
    hh = MLA_HEADS
    q_lora = w_dq.shape[1]
    tm, row_chunk = 1024, 512
    wq3 = w_uq.reshape(q_lora, hh, QK_NOPE + QK_ROPE)
    wq_rope = wq3[:, :, QK_NOPE:]
    wuq_ext = jnp.concatenate([
        wq3[:, :, :QK_NOPE].reshape(q_lora, hh * QK_NOPE),
        wq_rope.reshape(q_lora, hh * QK_ROPE),
        _rot_half(wq_rope).reshape(q_lora, hh * QK_ROPE)], axis=1).astype(BF16)
    wkr = w_dkv[:, KV_LORA:]
    wkr_rot = _rot_half(wkr)
    wdkv_ext = jnp.concatenate([w_dkv[:, :KV_LORA], wkr, wkr, wkr_rot, wkr_rot], axis=1).astype(BF16)
    wkv3 = w_ukv.reshape(KV_LORA, hh, QK_NOPE + V_DIM)
    wuk = wkv3[:, :, :QK_NOPE].reshape(KV_LORA, hh * QK_NOPE).astype(BF16)
    wuvt = wkv3[:, :, QK_NOPE:].reshape(KV_LORA, hh * V_DIM).T.astype(BF16)
    scale = float((QK_NOPE + QK_ROPE) ** -0.5 * math.log2(math.e))

    const = lambda b, i: (0, 0)
    tok = lambda b, i: (b * (s // tm) + i, 0)
    out_map = lambda b, i: (b, 0, i, 0)
    return pl.pallas_call(
        functools.partial(_mla_proj_kernel, scale=scale, row_chunk=row_chunk),
        grid=(bsz, s // tm),
        in_specs=[
            pl.BlockSpec((None, tm, d), lambda b, i: (b, i, 0)),
            pl.BlockSpec((None, None, 1, mods.shape[-1]), lambda b, i: (layer, b, 0, 0)),
            pl.BlockSpec((1, d), const),
            pl.BlockSpec((tm, cos_t.shape[-1]), tok),
            pl.BlockSpec((tm, sin_t.shape[-1]), tok),
            pl.BlockSpec(w_dq.shape, const),
            pl.BlockSpec((1, q_lora), const),
            pl.BlockSpec(wuq_ext.shape, const),
            pl.BlockSpec(wdkv_ext.shape, const),
            pl.BlockSpec((1, KV_LORA), const),
            pl.BlockSpec(wuk.shape, const),
            pl.BlockSpec(wuvt.shape, const),
        ],
        out_specs=[
            pl.BlockSpec((None, hh, tm, QK_PAD), out_map),
            pl.BlockSpec((None, hh, tm, QK_PAD), out_map),
            pl.BlockSpec((None, hh, V_DIM, tm), lambda b, i: (b, 0, 0, i)),
        ],
        out_shape=[
            jax.ShapeDtypeStruct((bsz, hh, s, QK_PAD), BF16),
            jax.ShapeDtypeStruct((bsz, hh, s, QK_PAD), BF16),
            jax.ShapeDtypeStruct((bsz, hh, V_DIM, s), BF16),
        ],
        compiler_params=_params("parallel", "parallel"),
        name="mla_proj",
    )(x, mods, g.reshape(1, d), cos_t, sin_t, w_dq.astype(BF16), g_q.reshape(1, q_lora),
      wuq_ext, wdkv_ext, g_kv.reshape(1, KV_LORA), wuk, wuvt)


def _mla_flash_kernel(q_ref, k_ref, vt_ref, o_ref, vext_sc, s0_sc, s1_sc, m_sc, acc_sc, *, tq):
    qi = pl.program_id(2)
    half = tq // 2

    @pl.when(qi == 0)
    def _():
        for t in range(vext_sc.shape[0]):
            vext_sc[t, 0:V_DIM, :] = vt_ref[:, t * tq:(t + 1) * tq]
            vext_sc[t, V_DIM:, :] = jnp.ones((ONES_ROWS, tq), BF16)

    m_sc[...] = jnp.full(m_sc.shape, NEG_INF, F32)
    acc_sc[...] = jnp.zeros(acc_sc.shape, F32)

    def scores(t, s_ref):
        k = k_ref[pl.ds(pl.multiple_of(t * tq, tq), tq), :]
        s_ref[...] = _dot_nt(k, q_ref[...])

    def scores_diag(s_ref):
        base = pl.multiple_of(qi * tq, tq)
        s_ref[0:half, :] = _dot_nt(k_ref[pl.ds(base, half), :], q_ref[...])
        s_ref[half:tq, half:tq] = _dot_nt(k_ref[pl.ds(base + half, half), :], q_ref[half:tq, :])

    def softmax_pv(s, vt, cols):
        m_prev = m_sc[:, cols]
        m_new = jnp.maximum(m_prev, jnp.max(s, axis=0, keepdims=True))
        alpha = jnp.exp2(m_prev - m_new)
        p = jnp.exp2(s - m_new).astype(BF16)
        acc_sc[:, cols] = alpha * acc_sc[:, cols] + _dot(vt, p)
        m_sc[:, cols] = m_new

    def update(t, s_ref):
        softmax_pv(s_ref[...], vext_sc[t], slice(0, tq))

    def update_diag(s_ref):
        causal = (lax.broadcasted_iota(jnp.int32, (half, half), 0)
                  <= lax.broadcasted_iota(jnp.int32, (half, half), 1))
        vt = vext_sc[qi]
        s_top = s_ref[0:half, :]
        s_top = jnp.concatenate([jnp.where(causal, s_top[:, 0:half], NEG_INF), s_top[:, half:tq]],
                                axis=1)
        softmax_pv(s_top, vt[:, 0:half], slice(0, tq))
        s_bot = jnp.where(causal, s_ref[half:tq, half:tq], NEG_INF)
        softmax_pv(s_bot, vt[:, half:tq], slice(half, tq))

    def pair(u, carry):
        t = 2 * u
        scores(t + 1, s1_sc)
        update(t, s0_sc)
        scores(t + 2, s0_sc)
        update(t + 1, s1_sc)
        return carry

    @pl.when(qi > 0)
    def _():
        scores(0, s0_sc)

    odd = qi % 2 == 1
    lax.fori_loop(0, jnp.where(odd, qi // 2, jnp.maximum(qi // 2 - 1, 0)), pair, 0)

    @pl.when(odd)
    def _():
        scores_diag(s1_sc)
        update(qi - 1, s0_sc)
        update_diag(s1_sc)

    @pl.when(jnp.logical_and(jnp.logical_not(odd), qi > 0))
    def _():
        scores(qi - 1, s1_sc)
        update(qi - 2, s0_sc)
        scores_diag(s0_sc)
        update(qi - 1, s1_sc)
        update_diag(s0_sc)

    @pl.when(qi == 0)
    def _():
        scores_diag(s0_sc)
        update_diag(s0_sc)

    acc = acc_sc[...]
    o_t = acc[0:V_DIM, :] / acc[V_DIM:V_DIM + 1, :]
    o_ref[...] = o_t.T.astype(o_ref.dtype)


def _mla_flash(q, k, vt):
    bsz, hh, s, _ = q.shape
    tq = 1024
    return pl.pallas_call(
        functools.partial(_mla_flash_kernel, tq=tq),
        grid=(bsz, hh, s // tq),
        in_specs=[
            pl.BlockSpec((None, None, tq, QK_PAD), lambda b, h, i: (b, h, i, 0)),
            pl.BlockSpec((None, None, s, QK_PAD), lambda b, h, i: (b, h, 0, 0)),
            pl.BlockSpec((None, None, V_DIM, s), lambda b, h, i: (b, h, 0, 0)),
        ],
        out_specs=pl.BlockSpec((None, tq, V_DIM), lambda b, h, i: (b, i, h)),
        out_shape=jax.ShapeDtypeStruct((bsz, s, hh * V_DIM), BF16),
        scratch_shapes=[pltpu.VMEM((s // tq, V_DIM + ONES_ROWS, tq), BF16),
                        pltpu.VMEM((tq, tq), F32), pltpu.VMEM((tq, tq), F32),
                        pltpu.VMEM((1, tq), F32), pltpu.VMEM((V_DIM + ONES_ROWS, tq), F32)],
        compiler_params=_params("parallel", "parallel", "arbitrary"),
        name="mla_flash",
    )(q, k, vt)


def _post_mlp_kernel(x_ref, a_ref, mod_ref, wo_ref, bo_ref, g_ref, w1_ref, w2_ref, gf_ref,
                     out_ref, x1_sc, h_sc, acc_sc, *, final_norm, n_ff, row_chunk):
    j = pl.program_id(2)
    d = x_ref.shape[-1]
    tm = x_ref.shape[0]

    def step(first, last):
        mod = mod_ref[...]
        gt1, sh2, sc2 = mod[:, 2 * d:3 * d], mod[:, 3 * d:4 * d], mod[:, 4 * d:5 * d]
        gt2 = mod[:, 5 * d:6 * d]
        w1 = w1_ref[...].astype(BF16)
        w2 = w2_ref[...].astype(BF16)
        for r in range(tm // row_chunk):
            rows = slice(r * row_chunk, (r + 1) * row_chunk)
            if first:
                y = _dot(a_ref[rows, :], wo_ref[...]) + bo_ref[...]
                x1 = x_ref[rows, :] + gt1 * y
                h = (_rmsnorm(x1, g_ref[...]) * (1.0 + sc2) + sh2).astype(BF16)
                h_sc[rows, :] = h
                if not last:
                    x1_sc[rows, :] = x1
            else:
                h = h_sc[rows, :]
            u = jnp.square(jnp.maximum(_dot(h, w1), 0.0)).astype(BF16)
            acc = _dot(u, w2)
            if not first:
                acc = acc_sc[rows, :] + acc
            if last:
                x2 = (x1 if first else x1_sc[rows, :]) + gt2 * acc
                if final_norm:
                    x2 = _rmsnorm(x2, gf_ref[...])
                out_ref[rows, :] = x2
            else:
                acc_sc[rows, :] = acc

    pl.when(j == 0)(functools.partial(step, True, n_ff == 1))
    if n_ff > 1:
        pl.when(j == n_ff - 1)(functools.partial(step, False, True))
    if n_ff > 2:
        pl.when(jnp.logical_and(j > 0, j < n_ff - 1))(functools.partial(step, False, False))


def _post_mlp(x, attn, mods, layer, w_o, b_o, g_mlp, w_ff1, w_ff2, g_final, final_norm):
    bsz, s, d = x.shape
    dff = w_ff1.shape[-1]
    tm, tf, row_chunk = 1024, 1024, 1024
    const = lambda b, i, j: (0, 0)
    tok = lambda b, i, j: (b, i, 0)
    return pl.pallas_call(
        functools.partial(_post_mlp_kernel, final_norm=final_norm, n_ff=dff // tf,
                          row_chunk=row_chunk),
        grid=(bsz, s // tm, dff // tf),
        in_specs=[
            pl.BlockSpec((None, tm, d), tok),
            pl.BlockSpec((None, tm, attn.shape[-1]), tok),
            pl.BlockSpec((None, None, 1, mods.shape[-1]), lambda b, i, j: (layer, b, 0, 0)),
            pl.BlockSpec(w_o.shape, const),
            pl.BlockSpec((1, d), const),
            pl.BlockSpec((1, d), const),
            pl.BlockSpec((None, d, tf), lambda b, i, j: (layer, 0, j)),
            pl.BlockSpec((None, tf, d), lambda b, i, j: (layer, j, 0)),
            pl.BlockSpec((1, d), const),
        ],
        out_specs=pl.BlockSpec((None, tm, d), tok),
        out_shape=jax.ShapeDtypeStruct((bsz, s, d), F32),
        scratch_shapes=[pltpu.VMEM((tm, d), F32), pltpu.VMEM((tm, d), BF16),
                        pltpu.VMEM((tm, d), F32)],
        compiler_params=_params("parallel", "parallel", "arbitrary"),
        name="post_mlp_final" if final_norm else "post_mlp",
    )(x, attn, mods, w_o.astype(BF16), b_o.reshape(1, d), g_mlp.reshape(1, d),
      w_ff1, w_ff2, g_final.reshape(1, d))


def _swa_proj_kernel(x_ref, mod_ref, g_ref, w_ref, b_ref, o_ref, *, scale):
    d = x_ref.shape[-1]
    mod = mod_ref[...]
    sh1, sc1 = mod[:, 0:d], mod[:, d:2 * d]
    h = (_rmsnorm(x_ref[...], g_ref[...]) * (1.0 + sc1) + sh1).astype(BF16)
    qkv = _dot(h, w_ref[...]) + b_ref[...]
    nq = SWA_HEADS * SWA_HEAD_DIM
    o_ref[:, 0:nq] = (qkv[:, 0:nq] * scale).astype(BF16)
    o_ref[:, nq:] = qkv[:, nq:].astype(BF16)


def _swa_window_order(t, axis):
    group = SWA_HEADS // SWA_KV_HEADS
    shape = t.shape
    t = t.reshape(shape[:axis] + (SWA_KV_HEADS, group, SWA_HEAD_DIM) + shape[axis + 1:])
    return jnp.swapaxes(t, axis, axis + 1).reshape(shape)


def _swa_proj(x, mods, layer, g, w_qkv, b_qkv):
    bsz, s, d = x.shape
    n = w_qkv.shape[1]
    tm = 1024
    const = lambda b, i: (0, 0)
    return pl.pallas_call(
        functools.partial(_swa_proj_kernel,
                          scale=float(SWA_HEAD_DIM ** -0.5 * math.log2(math.e))),
        grid=(bsz, s // tm),
        in_specs=[
            pl.BlockSpec((None, tm, d), lambda b, i: (b, i, 0)),
            pl.BlockSpec((None, None, 1, mods.shape[-1]), lambda b, i: (layer, b, 0, 0)),
            pl.BlockSpec((1, d), const),
            pl.BlockSpec(w_qkv.shape, const),
            pl.BlockSpec((1, n), const),
        ],
        out_specs=pl.BlockSpec((None, tm, n), lambda b, i: (b, i, 0)),
        out_shape=jax.ShapeDtypeStruct((bsz, s, n), BF16),
        compiler_params=_params("parallel", "parallel"),
        name="swa_proj",
    )(x, mods, g.reshape(1, d), w_qkv.astype(BF16), b_qkv.reshape(1, n))


def _swa_attn_kernel(sink_ref, q_ref, kc_ref, vc_ref, kp_ref, vp_ref, o_ref, bias_sc, *, slopes):
    first_step = jnp.logical_and(pl.program_id(0) == 0, pl.program_id(1) == 0)
    w = WINDOW
    dh = SWA_HEAD_DIM
    n_kv = SWA_KV_HEADS
    group = SWA_HEADS // n_kv
    gw = n_kv * dh
    log2e = math.log2(math.e)

    @pl.when(first_step)
    def _():
        qpos = lax.broadcasted_iota(jnp.int32, (w, 2 * w), 0)
        kpos = lax.broadcasted_iota(jnp.int32, (w, 2 * w), 1)
        dist = w + qpos - kpos
        in_window = jnp.logical_and(dist >= 0, dist < w)
        dist_f = dist.astype(F32)
        for hd in range(SWA_HEADS):
            bias_sc[hd * w:(hd + 1) * w, :] = jnp.where(
                in_window, dist_f * (-slopes[hd] * log2e), NEG_INF)

    no_prev = jnp.where(pl.program_id(1) == 0, NEG_INF, 0.0)
    ones_cols = jnp.ones((2 * w, w), BF16)
    lane_group = lax.broadcasted_iota(jnp.int32, (w, n_kv * dh), 1) // dh
    low_half = lax.broadcasted_iota(jnp.int32, (w, w), 1) < dh
    half_rows = (n_kv // 2) * group * w
    sink = jnp.concatenate([jnp.full((w, w), sink_ref[hd] * log2e, F32)
                            for hd in range(SWA_HEADS)], axis=0)

    def band(cur_ref, prev_ref, n):
        if n == 0:
            return jnp.concatenate([prev_ref[...], cur_ref[0:w, :]], axis=0)
        return cur_ref[(n - 1) * w:(n + 1) * w, :]

    def stage_scores(n):
        lhs = []
        for g in range(n_kv):
            for a in range(group):
                qwin = q_ref[n * w:(n + 1) * w, a * gw:(a + 1) * gw]
                lhs.append(jnp.where(lane_group == g, qwin, jnp.zeros_like(qwin)))
        s = _dot_nt(jnp.concatenate(lhs, axis=0), band(kc_ref, kp_ref, n)) + bias_sc[...]
        if n == 0:
            s = jnp.concatenate([s[:, 0:w] + no_prev, s[:, w:2 * w]], axis=1)
        return s

    def stage_softmax(s):
        m = jnp.maximum(jnp.max(s, axis=-1, keepdims=True), sink)
        return jnp.exp2(s - jnp.tile(m, (1, 2))).astype(BF16), jnp.exp2(sink - m)

    def stage_out(n, p, e):
        vall = band(vc_ref, vp_ref, n)
        pv_a = _dot(p[0:half_rows], jnp.concatenate([vall[:, 0:w], ones_cols], axis=1))
        pv_b = _dot(p[half_rows:], jnp.concatenate([ones_cols, vall[:, w:2 * w]], axis=1))
        o_a = pv_a[:, 0:w] / (pv_a[:, w:2 * w] + e[0:half_rows])
        o_b = pv_b[:, w:2 * w] / (pv_b[:, 0:w] + e[half_rows:])
        for a in range(group):
            lo = jnp.where(low_half, o_a[a * w:(a + 1) * w], o_a[(group + a) * w:(group + a + 1) * w])
            hi = jnp.where(low_half, o_b[a * w:(a + 1) * w], o_b[(group + a) * w:(group + a + 1) * w])
            o_ref[n * w:(n + 1) * w, a * gw:(a + 1) * gw] = jnp.concatenate(
                [lo, hi], axis=1).astype(o_ref.dtype)

    n_blocks = q_ref.shape[0] // w
    scores, probs = {}, {}
    for step in range(n_blocks + 2):
        if step < n_blocks:
            scores[step] = stage_scores(step)
        if 1 <= step <= n_blocks:
            probs[step - 1] = stage_softmax(scores.pop(step - 1))
        if step >= 2:
            stage_out(step - 2, *probs.pop(step - 2))


def _swa_attn(qkv, sinks):
    bsz, s, _ = qkv.shape
    nq = SWA_HEADS * SWA_HEAD_DIM
    nkv = SWA_KV_HEADS * SWA_HEAD_DIM
    tm = 2048
    per = tm // WINDOW
    slopes = tuple(float(v) for v in 2.0 ** (-8.0 * np.arange(1, SWA_HEADS + 1) / SWA_HEADS))
    kcol, vcol = nq // nkv, nq // nkv + 1
    prev = lambda b, i: jnp.maximum(i * per - 1, 0)
    return pl.pallas_call(
        functools.partial(_swa_attn_kernel, slopes=slopes),
        grid=(bsz, s // tm),
        in_specs=[
            pl.BlockSpec(memory_space=pltpu.SMEM),
            pl.BlockSpec((None, tm, nq), lambda b, i: (b, i, 0)),
            pl.BlockSpec((None, tm, nkv), lambda b, i: (b, i, kcol)),
            pl.BlockSpec((None, tm, nkv), lambda b, i: (b, i, vcol)),
            pl.BlockSpec((None, WINDOW, nkv), lambda b, i: (b, prev(b, i), kcol)),
            pl.BlockSpec((None, WINDOW, nkv), lambda b, i: (b, prev(b, i), vcol)),
        ],
        out_specs=pl.BlockSpec((None, tm, nq), lambda b, i: (b, i, 0)),
        out_shape=jax.ShapeDtypeStruct((bsz, s, nq), BF16),
        scratch_shapes=[pltpu.VMEM((SWA_HEADS * WINDOW, 2 * WINDOW), F32)],
        compiler_params=_params("arbitrary", "arbitrary"),
        name="swa_attn",
    )(sinks.astype(F32), qkv, qkv, qkv, qkv, qkv)


def kernel(x, c, positions, w_ada, b_ada, g_mix, g_mlp, mla_w_dq, mla_g_q, mla_w_uq, mla_w_dkv,
           mla_g_kv, mla_w_ukv, mla_w_o, swa_w_qkv, swa_b_qkv, swa_sinks, swa_w_o, swa_b_o,
           w_ff1, w_ff2, g_final):
    depth = w_ada.shape[0]
    bsz, s, d = x.shape
    mods = _adaln(c, w_ada, b_ada).reshape(depth, bsz, 1, w_ada.shape[-1])
    cos_t, sin_t = _rope_tables(positions)
    zero_bias = jnp.zeros((d,), F32)
    for i in range(depth):
        j = i // 2
        last = i == depth - 1
        if i % 2 == 0:
            q, k, v = _mla_proj(x, mods, i, g_mix[i], cos_t, sin_t, mla_w_dq[j], mla_g_q[j],
                                mla_w_uq[j], mla_w_dkv[j], mla_g_kv[j], mla_w_ukv[j])
            attn = _mla_flash(q, k, v)
            w_o, b_o = mla_w_o[j], zero_bias
        else:
            nq = SWA_HEADS * SWA_HEAD_DIM
            w_qkv = jnp.concatenate([_swa_window_order(swa_w_qkv[j][:, :nq], 1),
                                     swa_w_qkv[j][:, nq:]], axis=1)
            b_qkv = jnp.concatenate([_swa_window_order(swa_b_qkv[j][:nq], 0), swa_b_qkv[j][nq:]])
            qkv = _swa_proj(x, mods, i, g_mix[i], w_qkv, b_qkv)
            attn = _swa_attn(qkv, swa_sinks[j])
            w_o, b_o = _swa_window_order(swa_w_o[j], 0), swa_b_o[j]
        x = _post_mlp(x, attn, mods, i, w_o, b_o, g_mlp[i], w_ff1, w_ff2, g_final, last)
    return x
```

```python
import functools
import math

import numpy as np
import jax
import jax.numpy as jnp
from jax import lax
from jax.experimental import pallas as pl
from jax.experimental.pallas import tpu as pltpu

F32 = jnp.float32
BF16 = jnp.bfloat16

EPS = 1e-6
ROPE_THETA = 10000.0
MLA_HEADS = 8
QK_NOPE = 128
QK_ROPE = 64
V_DIM = 128
KV_LORA = 256
SWA_HEADS = 16
SWA_KV_HEADS = 4
SWA_HEAD_DIM = 64
WINDOW = 128

LANES = 128
QK_PAD = 2 * LANES
ONES_ROWS = 16
VMEM_LIMIT = 56 * 1024 * 1024

NEG_INF = float("-inf")


def _params(*sem):
    return pltpu.CompilerParams(dimension_semantics=sem, vmem_limit_bytes=VMEM_LIMIT)


def _rmsnorm(x, g):
    y = x * lax.rsqrt(jnp.mean(x * x, axis=-1, keepdims=True) + EPS)
    return y * g


def _dot(a, b):
    return jnp.dot(a, b, preferred_element_type=F32)


def _dot_nt(a, b):
    return lax.dot_general(a, b, (((1,), (1,)), ((), ())), preferred_element_type=F32)


def _adaln_kernel(ct_ref, w_ref, b_ref, o_ref):
    ct = ct_ref[...]
    cond = ct * jax.nn.sigmoid(ct)
    w = w_ref[...]
    rows = [jnp.sum(w * cond[:, b:b + 1], axis=0, keepdims=True)
            for b in range(ct.shape[1])]
    o_ref[...] = jnp.concatenate(rows, axis=0) + b_ref[...]


def _adaln(c, w_ada, b_ada):
    depth, d, n = w_ada.shape
    bsz = c.shape[0]
    tn = 1024
    return pl.pallas_call(
        _adaln_kernel,
        grid=(depth, n // tn),
        in_specs=[
            pl.BlockSpec((d, bsz), lambda l, j: (0, 0)),
            pl.BlockSpec((None, d, tn), lambda l, j: (l, 0, j)),
            pl.BlockSpec((None, 1, tn), lambda l, j: (l, 0, j)),
        ],
        out_specs=pl.BlockSpec((None, bsz, tn), lambda l, j: (l, 0, j)),
        out_shape=jax.ShapeDtypeStruct((depth, bsz, n), F32),
        compiler_params=_params("parallel", "parallel"),
        name="adaln",
    )(c.T, w_ada, b_ada.reshape(depth, 1, n))


def _rope_table_kernel(pos_ref, inv_ref, cos_ref, sin_ref):
    ang = pos_ref[...].astype(F32) * inv_ref[...]
    cos_ref[...] = jnp.cos(ang)
    sin_ref[...] = jnp.sin(ang)


def _rope_tables(positions):
    n = positions.size
    half = QK_ROPE // 2
    per_row = LANES // half
    inv_freq = ROPE_THETA ** (-jnp.arange(half, dtype=F32) / half)
    inv = jnp.tile(inv_freq, per_row).reshape(1, LANES)
    pos_rows = jnp.repeat(positions.reshape(n // per_row, per_row), half, axis=1)
    rows = n // per_row
    tm = 512
    cos_t, sin_t = pl.pallas_call(
        _rope_table_kernel,
        grid=(rows // tm,),
        in_specs=[pl.BlockSpec((tm, LANES), lambda i: (i, 0)),
                  pl.BlockSpec((1, LANES), lambda i: (0, 0))],
        out_specs=[pl.BlockSpec((tm, LANES), lambda i: (i, 0)),
                   pl.BlockSpec((tm, LANES), lambda i: (i, 0))],
        out_shape=[jax.ShapeDtypeStruct((rows, LANES), F32)] * 2,
        compiler_params=_params("parallel"),
        name="rope_table",
    )(pos_rows, inv)
    return cos_t.reshape(n, half), sin_t.reshape(n, half)


def _mla_proj_kernel(x_ref, mod_ref, g_ref, cos_ref, sin_ref, wdq_ref, gq_ref, wuqt_ref,
                     wdkv_ref, gkv_ref, wuk_ref, wuvt_ref, qt_ref, k_ref, vt_ref, *, scale,
                     row_chunk):
    d = x_ref.shape[-1]
    mod = mod_ref[...]
    sh1, sc1 = mod[:, 0:d], mod[:, d:2 * d]
    reps = LANES // cos_ref.shape[-1]
    n_nope = MLA_HEADS * QK_NOPE
    n_rope = MLA_HEADS * QK_ROPE
    n_chunks = x_ref.shape[0] // row_chunk
    zero_rows = jnp.zeros((QK_PAD - QK_NOPE - QK_ROPE, row_chunk), BF16)

    def down(c):
        rows = slice(c * row_chunk, (c + 1) * row_chunk)
        h = (_rmsnorm(x_ref[rows, :], g_ref[...]) * (1.0 + sc1) + sh1).astype(BF16)
        return _dot(h, wdq_ref[...]), _dot(h, wdkv_ref[...])

    def up(cq_raw, dkv):
        cq = _rmsnorm(cq_raw, gq_ref[...]).astype(BF16)
        ckv = _rmsnorm(dkv[:, 0:KV_LORA], gkv_ref[...]).astype(BF16)
        return (_dot_nt(wuqt_ref[...], cq), _dot(ckv, wuk_ref[...]),
                _dot_nt(wuvt_ref[...], ckv))

    def emit(c, dkv, qt, kn, vt):
        rows = slice(c * row_chunk, (c + 1) * row_chunk)
        cos = jnp.tile(cos_ref[rows, :], (1, reps))
        sin = jnp.tile(sin_ref[rows, :], (1, reps))
        cos_t = cos.T[0:QK_ROPE, :]
        sin_t = sin.T[0:QK_ROPE, :]
        for hd in range(MLA_HEADS):
            qt_ref[hd, 0:QK_NOPE, rows] = (qt[hd * QK_NOPE:(hd + 1) * QK_NOPE, :]
                                           * scale).astype(BF16)
            a = qt[n_nope + hd * QK_ROPE:n_nope + (hd + 1) * QK_ROPE, :]
            b = qt[n_nope + n_rope + hd * QK_ROPE:n_nope + n_rope + (hd + 1) * QK_ROPE, :]
            qt_ref[hd, QK_NOPE:QK_NOPE + QK_ROPE, rows] = (
                (a * cos_t + b * sin_t) * scale).astype(BF16)
            qt_ref[hd, QK_NOPE + QK_ROPE:QK_PAD, rows] = zero_rows
        kr = (dkv[:, KV_LORA:KV_LORA + LANES] * cos
              + dkv[:, KV_LORA + LANES:KV_LORA + 2 * LANES] * sin).astype(BF16)
        for hd in range(MLA_HEADS):
            k_ref[hd, rows, 0:LANES] = kn[:, hd * QK_NOPE:(hd + 1) * QK_NOPE].astype(BF16)
            k_ref[hd, rows, LANES:QK_PAD] = kr
            vt_ref[hd, :, rows] = vt[hd * V_DIM:(hd + 1) * V_DIM, :].astype(BF16)

    downs, ups = {}, {}
    for step in range(n_chunks + 2):
        if step < n_chunks:
            downs[step] = down(step)
        if 1 <= step <= n_chunks:
            ups[step - 1] = up(*downs[step - 1])
        if step >= 2:
            emit(step - 2, downs.pop(step - 2)[1], *ups.pop(step - 2))


def _rot_half(w):
    half = w.shape[-1] // 2
    return jnp.concatenate([-w[..., half:], w[..., :half]], axis=-1)


def _mla_proj(x, mods, layer, g, cos_t, sin_t, w_dq, g_q, w_uq, w_dkv, g_kv, w_ukv):
    bsz, s, d = x.shape
    hh = MLA_HEADS
    q_lora = w_dq.shape[1]
    tm, row_chunk = 1024, 512
    wq3 = w_uq.reshape(q_lora, hh, QK_NOPE + QK_ROPE)
    wq_rope = wq3[:, :, QK_NOPE:]
    wuqt_ext = jnp.concatenate([
        wq3[:, :, :QK_NOPE].reshape(q_lora, hh * QK_NOPE),
        wq_rope.reshape(q_lora, hh * QK_ROPE),
        _rot_half(wq_rope).reshape(q_lora, hh * QK_ROPE)], axis=1).T.astype(BF16)
    wkr = w_dkv[:, KV_LORA:]
    pad = jnp.zeros((d, LANES - QK_ROPE), w_dkv.dtype)
    wdkv_ext = jnp.concatenate([w_dkv[:, :KV_LORA], wkr, pad, _rot_half(wkr), pad],
                               axis=1).astype(BF16)
    wkv3 = w_ukv.reshape(KV_LORA, hh, QK_NOPE + V_DIM)
    wuk = wkv3[:, :, :QK_NOPE].reshape(KV_LORA, hh * QK_NOPE).astype(BF16)
    wuvt = wkv3[:, :, QK_NOPE:].reshape(KV_LORA, hh * V_DIM).T.astype(BF16)
    scale = float((QK_NOPE + QK_ROPE) ** -0.5 * math.log2(math.e))

    const = lambda b, i: (0, 0)
    tok = lambda b, i: (b * (s // tm) + i, 0)
    out_map = lambda b, i: (b, 0, i, 0)
    out_map_t = lambda b, i: (b, 0, 0, i)
    return pl.pallas_call(
        functools.partial(_mla_proj_kernel, scale=scale, row_chunk=row_chunk),
        grid=(bsz, s // tm),
        in_specs=[
            pl.BlockSpec((None, tm, d), lambda b, i: (b, i, 0)),
            pl.BlockSpec((None, None, 1, mods.shape[-1]), lambda b, i: (layer, b, 0, 0)),
            pl.BlockSpec((1, d), const),
            pl.BlockSpec((tm, cos_t.shape[-1]), tok),
            pl.BlockSpec((tm, sin_t.shape[-1]), tok),
            pl.BlockSpec(w_dq.shape, const),
            pl.BlockSpec((1, q_lora), const),
            pl.BlockSpec(wuqt_ext.shape, const),
            pl.BlockSpec(wdkv_ext.shape, const),
            pl.BlockSpec((1, KV_LORA), const),
            pl.BlockSpec(wuk.shape, const),
            pl.BlockSpec(wuvt.shape, const),
        ],
        out_specs=[
            pl.BlockSpec((None, hh, QK_PAD, tm), out_map_t),
            pl.BlockSpec((None, hh, tm, QK_PAD), out_map),
            pl.BlockSpec((None, hh, V_DIM, tm), out_map_t),
        ],
        out_shape=[
            jax.ShapeDtypeStruct((bsz, hh, QK_PAD, s), BF16),
            jax.ShapeDtypeStruct((bsz, hh, s, QK_PAD), BF16),
            jax.ShapeDtypeStruct((bsz, hh, V_DIM, s), BF16),
        ],
        compiler_params=_params("parallel", "parallel"),
        name="mla_proj",
    )(x, mods, g.reshape(1, d), cos_t, sin_t, w_dq.astype(BF16), g_q.reshape(1, q_lora),
      wuqt_ext, wdkv_ext, g_kv.reshape(1, KV_LORA), wuk, wuvt)


def _mla_flash_kernel(qt_ref, k_ref, vt_ref, o_ref, vext_sc, s0_sc, s1_sc, m_sc, acc_sc, *, tq):
    qi = pl.program_id(2)
    half = tq // 2

    @pl.when(qi == 0)
    def _():
        for t in range(vext_sc.shape[0]):
            vext_sc[t, 0:V_DIM, :] = vt_ref[:, t * tq:(t + 1) * tq]
            vext_sc[t, V_DIM:, :] = jnp.ones((ONES_ROWS, tq), BF16)

    m_sc[...] = jnp.full(m_sc.shape, NEG_INF, F32)
    acc_sc[...] = jnp.zeros(acc_sc.shape, F32)

    def scores(t, s_ref):
        k = k_ref[pl.ds(pl.multiple_of(t * tq, tq), tq), :]
        s_ref[...] = _dot(k, qt_ref[...])

    def scores_diag(s_ref):
        base = pl.multiple_of(qi * tq, tq)
        s_ref[0:half, :] = _dot(k_ref[pl.ds(base, half), :], qt_ref[...])
        s_ref[half:tq, half:tq] = _dot(k_ref[pl.ds(base + half, half), :], qt_ref[:, half:tq])

    def softmax_pv(s, vt, cols):
        m_prev = m_sc[:, cols]
        m_new = jnp.maximum(m_prev, jnp.max(s, axis=0, keepdims=True))
        alpha = jnp.exp2(m_prev - m_new)
        p = jnp.exp2(s - m_new).astype(BF16)
        acc_sc[:, cols] = alpha * acc_sc[:, cols] + _dot(vt, p)
        m_sc[:, cols] = m_new

    def update(t, s_ref):
        softmax_pv(s_ref[...], vext_sc[t], slice(0, tq))

    def update_diag(s_ref):
        causal = (lax.broadcasted_iota(jnp.int32, (half, half), 0)
                  <= lax.broadcasted_iota(jnp.int32, (half, half), 1))
        vt = vext_sc[qi]
        s_top = s_ref[0:half, :]
        s_top = jnp.concatenate([jnp.where(causal, s_top[:, 0:half], NEG_INF), s_top[:, half:tq]],
                                axis=1)
        softmax_pv(s_top, vt[:, 0:half], slice(0, tq))
        s_bot = jnp.where(causal, s_ref[half:tq, half:tq], NEG_INF)
        softmax_pv(s_bot, vt[:, half:tq], slice(half, tq))

    def pair(u, carry):
        t = 2 * u
        scores(t + 1, s1_sc)
        update(t, s0_sc)
        scores(t + 2, s0_sc)
        update(t + 1, s1_sc)
        return carry

    @pl.when(qi > 0)
    def _():
        scores(0, s0_sc)

    odd = qi % 2 == 1
    lax.fori_loop(0, jnp.where(odd, qi // 2, jnp.maximum(qi // 2 - 1, 0)), pair, 0)

    @pl.when(odd)
    def _():
        scores_diag(s1_sc)
        update(qi - 1, s0_sc)
        update_diag(s1_sc)

    @pl.when(jnp.logical_and(jnp.logical_not(odd), qi > 0))
    def _():
        scores(qi - 1, s1_sc)
        update(qi - 2, s0_sc)
        scores_diag(s0_sc)
        update(qi - 1, s1_sc)
        update_diag(s0_sc)

    @pl.when(qi == 0)
    def _():
        scores_diag(s0_sc)
        update_diag(s0_sc)

    acc = acc_sc[...]
    o_t = acc[0:V_DIM, :] / acc[V_DIM:V_DIM + 1, :]
    o_ref[...] = o_t.T.astype(o_ref.dtype)


def _mla_flash(qt, k, vt):
    bsz, hh, s, _ = k.shape
    tq = 1024
    return pl.pallas_call(
        functools.partial(_mla_flash_kernel, tq=tq),
        grid=(bsz, hh, s // tq),
        in_specs=[
            pl.BlockSpec((None, None, QK_PAD, tq), lambda b, h, i: (b, h, 0, i)),
            pl.BlockSpec((None, None, s, QK_PAD), lambda b, h, i: (b, h, 0, 0)),
            pl.BlockSpec((None, None, V_DIM, s), lambda b, h, i: (b, h, 0, 0)),
        ],
        out_specs=pl.BlockSpec((None, tq, V_DIM), lambda b, h, i: (b, i, h)),
        out_shape=jax.ShapeDtypeStruct((bsz, s, hh * V_DIM), BF16),
        scratch_shapes=[pltpu.VMEM((s // tq, V_DIM + ONES_ROWS, tq), BF16),
                        pltpu.VMEM((tq, tq), F32), pltpu.VMEM((tq, tq), F32),
                        pltpu.VMEM((1, tq), F32), pltpu.VMEM((V_DIM + ONES_ROWS, tq), F32)],
        compiler_params=_params("parallel", "parallel", "arbitrary"),
        name="mla_flash",
    )(qt, k, vt)


def _post_mlp_kernel(x_ref, a_ref, mod_ref, wo_ref, bo_ref, g_ref, w1_ref, w2_ref, gf_ref,
                     out_ref, x1_sc, h_sc, acc_sc, *, final_norm, n_ff, row_chunk):
    j = pl.program_id(2)
    d = x_ref.shape[-1]
    tm = x_ref.shape[0]

    def step(first, last):
        mod = mod_ref[...]
        gt1, sh2, sc2 = mod[:, 2 * d:3 * d], mod[:, 3 * d:4 * d], mod[:, 4 * d:5 * d]
        gt2 = mod[:, 5 * d:6 * d]
        w1 = w1_ref[...].astype(BF16)
        w2 = w2_ref[...].astype(BF16)
        for r in range(tm // row_chunk):
            rows = slice(r * row_chunk, (r + 1) * row_chunk)
            if first:
                y = _dot(a_ref[rows, :], wo_ref[...]) + bo_ref[...]
                x1 = x_ref[rows, :] + gt1 * y
                h = (_rmsnorm(x1, g_ref[...]) * (1.0 + sc2) + sh2).astype(BF16)
                h_sc[rows, :] = h
                if not last:
                    x1_sc[rows, :] = x1
            else:
                h = h_sc[rows, :]
            u = jnp.square(jnp.maximum(_dot(h, w1), 0.0)).astype(BF16)
            acc = _dot(u, w2)
            if not first:
                acc = acc_sc[rows, :] + acc
            if last:
                x2 = (x1 if first else x1_sc[rows, :]) + gt2 * acc
                if final_norm:
                    x2 = _rmsnorm(x2, gf_ref[...])
                out_ref[rows, :] = x2
            else:
                acc_sc[rows, :] = acc

    pl.when(j == 0)(functools.partial(step, True, n_ff == 1))
    if n_ff > 1:
        pl.when(j == n_ff - 1)(functools.partial(step, False, True))
    if n_ff > 2:
        pl.when(jnp.logical_and(j > 0, j < n_ff - 1))(functools.partial(step, False, False))


def _post_mlp(x, attn, mods, layer, w_o, b_o, g_mlp, w_ff1, w_ff2, g_final, final_norm):
    bsz, s, d = x.shape
    dff = w_ff1.shape[-1]
    tm, tf, row_chunk = 1024, 1024, 1024
    const = lambda b, i, j: (0, 0)
    tok = lambda b, i, j: (b, i, 0)
    return pl.pallas_call(
        functools.partial(_post_mlp_kernel, final_norm=final_norm, n_ff=dff // tf,
                          row_chunk=row_chunk),
        grid=(bsz, s // tm, dff // tf),
        in_specs=[
            pl.BlockSpec((None, tm, d), tok),
            pl.BlockSpec((None, tm, attn.shape[-1]), tok),
            pl.BlockSpec((None, None, 1, mods.shape[-1]), lambda b, i, j: (layer, b, 0, 0)),
            pl.BlockSpec(w_o.shape, const),
            pl.BlockSpec((1, d), const),
            pl.BlockSpec((1, d), const),
            pl.BlockSpec((None, d, tf), lambda b, i, j: (layer, 0, j)),
            pl.BlockSpec((None, tf, d), lambda b, i, j: (layer, j, 0)),
            pl.BlockSpec((1, d), const),
        ],
        out_specs=pl.BlockSpec((None, tm, d), tok),
        out_shape=jax.ShapeDtypeStruct((bsz, s, d), F32),
        scratch_shapes=[pltpu.VMEM((tm, d), F32), pltpu.VMEM((tm, d), BF16),
                        pltpu.VMEM((tm, d), F32)],
        compiler_params=_params("parallel", "parallel", "arbitrary"),
        name="post_mlp_final" if final_norm else "post_mlp",
    )(x, attn, mods, w_o.astype(BF16), b_o.reshape(1, d), g_mlp.reshape(1, d),
      w_ff1, w_ff2, g_final.reshape(1, d))


def _swa_proj_kernel(x_ref, mod_ref, g_ref, w_ref, b_ref, o_ref, *, scale):
    d = x_ref.shape[-1]
    mod = mod_ref[...]
    sh1, sc1 = mod[:, 0:d], mod[:, d:2 * d]
    h = (_rmsnorm(x_ref[...], g_ref[...]) * (1.0 + sc1) + sh1).astype(BF16)
    qkv = _dot(h, w_ref[...]) + b_ref[...]
    nq = SWA_HEADS * SWA_HEAD_DIM
    o_ref[:, 0:nq] = (qkv[:, 0:nq] * scale).astype(BF16)
    o_ref[:, nq:] = qkv[:, nq:].astype(BF16)


def _swa_window_order(t, axis):
    group = SWA_HEADS // SWA_KV_HEADS
    shape = t.shape
    t = t.reshape(shape[:axis] + (SWA_KV_HEADS, group, SWA_HEAD_DIM) + shape[axis + 1:])
    return jnp.swapaxes(t, axis, axis + 1).reshape(shape)


def _swa_proj(x, mods, layer, g, w_qkv, b_qkv):
    bsz, s, d = x.shape
    n = w_qkv.shape[1]
    tm = 1024
    const = lambda b, i: (0, 0)
    return pl.pallas_call(
        functools.partial(_swa_proj_kernel,
                          scale=float(SWA_HEAD_DIM ** -0.5 * math.log2(math.e))),
        grid=(bsz, s // tm),
        in_specs=[
            pl.BlockSpec((None, tm, d), lambda b, i: (b, i, 0)),
            pl.BlockSpec((None, None, 1, mods.shape[-1]), lambda b, i: (layer, b, 0, 0)),
            pl.BlockSpec((1, d), const),
            pl.BlockSpec(w_qkv.shape, const),
            pl.BlockSpec((1, n), const),
        ],
        out_specs=pl.BlockSpec((None, tm, n), lambda b, i: (b, i, 0)),
        out_shape=jax.ShapeDtypeStruct((bsz, s, n), BF16),
        compiler_params=_params("parallel", "parallel"),
        name="swa_proj",
    )(x, mods, g.reshape(1, d), w_qkv.astype(BF16), b_qkv.reshape(1, n))


def _swa_attn_kernel(sink_ref, q_ref, kc_ref, vc_ref, kp_ref, vp_ref, o_ref, bias_sc, *, slopes):
    first_step = jnp.logical_and(pl.program_id(0) == 0, pl.program_id(1) == 0)
    w = WINDOW
    dh = SWA_HEAD_DIM
    n_kv = SWA_KV_HEADS
    group = SWA_HEADS // n_kv
    gw = n_kv * dh
    log2e = math.log2(math.e)

    @pl.when(first_step)
    def _():
        qpos = lax.broadcasted_iota(jnp.int32, (w, 2 * w), 0)
        kpos = lax.broadcasted_iota(jnp.int32, (w, 2 * w), 1)
        dist = w + qpos - kpos
        in_window = jnp.logical_and(dist >= 0, dist < w)
        dist_f = dist.astype(F32)
        for hd in range(SWA_HEADS):
            bias_sc[hd * w:(hd + 1) * w, :] = jnp.where(
                in_window, dist_f * (-slopes[hd] * log2e), NEG_INF)

    no_prev = jnp.where(pl.program_id(1) == 0, NEG_INF, 0.0)
    ones_cols = jnp.ones((2 * w, w), BF16)
    lane_group = lax.broadcasted_iota(jnp.int32, (w, n_kv * dh), 1) // dh
    low_half = lax.broadcasted_iota(jnp.int32, (w, w), 1) < dh
    half_rows = (n_kv // 2) * group * w
    sink = jnp.concatenate([jnp.full((w, w), sink_ref[hd] * log2e, F32)
                            for hd in range(SWA_HEADS)], axis=0)

    def band(cur_ref, prev_ref, n):
        if n == 0:
            return jnp.concatenate([prev_ref[...], cur_ref[0:w, :]], axis=0)
        return cur_ref[(n - 1) * w:(n + 1) * w, :]

    def stage_scores(n):
        lhs = []
        for g in range(n_kv):
            for a in range(group):
                qwin = q_ref[n * w:(n + 1) * w, a * gw:(a + 1) * gw]
                lhs.append(jnp.where(lane_group == g, qwin, jnp.zeros_like(qwin)))
        s = _dot_nt(jnp.concatenate(lhs, axis=0), band(kc_ref, kp_ref, n)) + bias_sc[...]
        if n == 0:
            s = jnp.concatenate([s[:, 0:w] + no_prev, s[:, w:2 * w]], axis=1)
        return s

    def stage_softmax(s):
        m = jnp.maximum(jnp.max(s, axis=-1, keepdims=True), sink)
        return jnp.exp2(s - jnp.tile(m, (1, 2))).astype(BF16), jnp.exp2(sink - m)

    def stage_out(n, p, e):
        vall = band(vc_ref, vp_ref, n)
        pv_a = _dot(p[0:half_rows], jnp.concatenate([vall[:, 0:w], ones_cols], axis=1))
        pv_b = _dot(p[half_rows:], jnp.concatenate([ones_cols, vall[:, w:2 * w]], axis=1))
        o_a = pv_a[:, 0:w] / (pv_a[:, w:2 * w] + e[0:half_rows])
        o_b = pv_b[:, w:2 * w] / (pv_b[:, 0:w] + e[half_rows:])
        for a in range(group):
            lo = jnp.where(low_half, o_a[a * w:(a + 1) * w], o_a[(group + a) * w:(group + a + 1) * w])
            hi = jnp.where(low_half, o_b[a * w:(a + 1) * w], o_b[(group + a) * w:(group + a + 1) * w])
            o_ref[n * w:(n + 1) * w, a * gw:(a + 1) * gw] = jnp.concatenate(
                [lo, hi], axis=1).astype(o_ref.dtype)

    n_blocks = q_ref.shape[0] // w
    scores, probs = {}, {}
    for step in range(n_blocks + 2):
        if step < n_blocks:
            scores[step] = stage_scores(step)
        if 1 <= step <= n_blocks:
            probs[step - 1] = stage_softmax(scores.pop(step - 1))
        if step >= 2:
            stage_out(step - 2, *probs.pop(step - 2))


def _swa_attn(qkv, sinks):
    bsz, s, _ = qkv.shape
    nq = SWA_HEADS * SWA_HEAD_DIM
    nkv = SWA_KV_HEADS * SWA_HEAD_DIM
    tm = 2048
    per = tm // WINDOW
    slopes = tuple(float(v) for v in 2.0 ** (-8.0 * np.arange(1, SWA_HEADS + 1) / SWA_HEADS))
    kcol, vcol = nq // nkv, nq // nkv + 1
    prev = lambda b, i: jnp.maximum(i * per - 1, 0)
    return pl.pallas_call(
        functools.partial(_swa_attn_kernel, slopes=slopes),
        grid=(bsz, s // tm),
        in_specs=[
            pl.BlockSpec(memory_space=pltpu.SMEM),
            pl.BlockSpec((None, tm, nq), lambda b, i: (b, i, 0)),
            pl.BlockSpec((None, tm, nkv), lambda b, i: (b, i, kcol)),
            pl.BlockSpec((None, tm, nkv), lambda b, i: (b, i, vcol)),
            pl.BlockSpec((None, WINDOW, nkv), lambda b, i: (b, prev(b, i), kcol)),
            pl.BlockSpec((None, WINDOW, nkv), lambda b, i: (b, prev(b, i), vcol)),
        ],
        out_specs=pl.BlockSpec((None, tm, nq), lambda b, i: (b, i, 0)),
        out_shape=jax.ShapeDtypeStruct((bsz, s, nq), BF16),
        scratch_shapes=[pltpu.VMEM((SWA_HEADS * WINDOW, 2 * WINDOW), F32)],
        compiler_params=_params("arbitrary", "arbitrary"),
        name="swa_attn",
    )(sinks.astype(F32), qkv, qkv, qkv, qkv, qkv)


def kernel(x, c, positions, w_ada, b_ada, g_mix, g_mlp, mla_w_dq, mla_g_q, mla_w_uq, mla_w_dkv,
           mla_g_kv, mla_w_ukv, mla_w_o, swa_w_qkv, swa_b_qkv, swa_sinks, swa_w_o, swa_b_o,
           w_ff1, w_ff2, g_final):
    depth = w_ada.shape[0]
    bsz, s, d = x.shape
    mods = _adaln(c, w_ada, b_ada).reshape(depth, bsz, 1, w_ada.shape[-1])
    cos_t, sin_t = _rope_tables(positions)
    zero_bias = jnp.zeros((d,), F32)
    for i in range(depth):
        j = i // 2
        last = i == depth - 1
        if i % 2 == 0:
            q, k, v = _mla_proj(x, mods, i, g_mix[i], cos_t, sin_t, mla_w_dq[j], mla_g_q[j],
                                mla_w_uq[j], mla_w_dkv[j], mla_g_kv[j], mla_w_ukv[j])
            attn = _mla_flash(q, k, v)
            w_o, b_o = mla_w_o[j], zero_bias
        else:
            nq = SWA_HEADS * SWA_HEAD_DIM
            w_qkv = jnp.concatenate([_swa_window_order(swa_w_qkv[j][:, :nq], 1),
                                     swa_w_qkv[j][:, nq:]], axis=1)
            b_qkv = jnp.concatenate([_swa_window_order(swa_b_qkv[j][:nq], 0), swa_b_qkv[j][nq:]])
            qkv = _swa_proj(x, mods, i, g_mix[i], w_qkv, b_qkv)
            attn = _swa_attn(qkv, swa_sinks[j])
            w_o, b_o = _swa_window_order(swa_w_o[j], 0), swa_b_o[j]
        x = _post_mlp(x, attn, mods, i, w_o, b_o, g_mlp[i], w_ff1, w_ff2, g_final, last)
    return x
```

```python
import functools
import math

import numpy as np
import jax
import jax.numpy as jnp
from jax import lax
from jax.experimental import pallas as pl
from jax.experimental.pallas import tpu as pltpu

F32 = jnp.float32
BF16 = jnp.bfloat16

EPS = 1e-6
ROPE_THETA = 10000.0
MLA_HEADS = 8
QK_NOPE = 128
QK_ROPE = 64
V_DIM = 128
KV_LORA = 256
SWA_HEADS = 16
SWA_KV_HEADS = 4
SWA_HEAD_DIM = 64
WINDOW = 128

LANES = 128
QK_PAD = 2 * LANES
ONES_ROWS = 16
VMEM_LIMIT = 56 * 1024 * 1024

NEG_INF = float("-inf")


def _params(*sem):
    return pltpu.CompilerParams(dimension_semantics=sem, vmem_limit_bytes=VMEM_LIMIT)


def _rmsnorm(x, g):
    y = x * lax.rsqrt(jnp.mean(x * x, axis=-1, keepdims=True) + EPS)
    return y * g


def _dot(a, b):
    return jnp.dot(a, b, preferred_element_type=F32)


def _dot_nt(a, b):
    return lax.dot_general(a, b, (((1,), (1,)), ((), ())), preferred_element_type=F32)


def _adaln_kernel(ct_ref, w_ref, b_ref, o_ref):
    ct = ct_ref[...]
    cond = ct * jax.nn.sigmoid(ct)
    w = w_ref[...]
    rows = [jnp.sum(w * cond[:, b:b + 1], axis=0, keepdims=True)
            for b in range(ct.shape[1])]
    o_ref[...] = jnp.concatenate(rows, axis=0) + b_ref[...]


def _adaln(c, w_ada, b_ada):
    depth, d, n = w_ada.shape
    bsz = c.shape[0]
    tn = 1024
    return pl.pallas_call(
        _adaln_kernel,
        grid=(depth, n // tn),
        in_specs=[
            pl.BlockSpec((d, bsz), lambda l, j: (0, 0)),
            pl.BlockSpec((None, d, tn), lambda l, j: (l, 0, j)),
            pl.BlockSpec((None, 1, tn), lambda l, j: (l, 0, j)),
        ],
        out_specs=pl.BlockSpec((None, bsz, tn), lambda l, j: (l, 0, j)),
        out_shape=jax.ShapeDtypeStruct((depth, bsz, n), F32),
        compiler_params=_params("parallel", "parallel"),
        name="adaln",
    )(c.T, w_ada, b_ada.reshape(depth, 1, n))


def _rope_table_kernel(pos_ref, inv_ref, cos_ref, sin_ref):
    ang = pos_ref[...].astype(F32) * inv_ref[...]
    cos_ref[...] = jnp.cos(ang)
    sin_ref[...] = jnp.sin(ang)


def _rope_tables(positions):
    n = positions.size
    half = QK_ROPE // 2
    per_row = LANES // half
    inv_freq = ROPE_THETA ** (-jnp.arange(half, dtype=F32) / half)
    inv = jnp.tile(inv_freq, per_row).reshape(1, LANES)
    pos_rows = jnp.repeat(positions.reshape(n // per_row, per_row), half, axis=1)
    rows = n // per_row
    tm = 512
    cos_t, sin_t = pl.pallas_call(
        _rope_table_kernel,
        grid=(rows // tm,),
        in_specs=[pl.BlockSpec((tm, LANES), lambda i: (i, 0)),
                  pl.BlockSpec((1, LANES), lambda i: (0, 0))],
        out_specs=[pl.BlockSpec((tm, LANES), lambda i: (i, 0)),
                   pl.BlockSpec((tm, LANES), lambda i: (i, 0))],
        out_shape=[jax.ShapeDtypeStruct((rows, LANES), F32)] * 2,
        compiler_params=_params("parallel"),
        name="rope_table",
    )(pos_rows, inv)
    return cos_t.reshape(n, half), sin_t.reshape(n, half)


def _mla_proj_kernel(x_ref, mod_ref, g_ref, cos_ref, sin_ref, wdq_ref, gq_ref, wuqt_ref,
                     wdkv_ref, gkv_ref, wuk_ref, wuvt_ref, qt_ref, k_ref, vt_ref, *, scale,
                     row_chunk):
    d = x_ref.shape[-1]
    mod = mod_ref[...]
    sh1, sc1 = mod[:, 0:d], mod[:, d:2 * d]
    reps = LANES // cos_ref.shape[-1]
    n_nope = MLA_HEADS * QK_NOPE
    n_rope = MLA_HEADS * QK_ROPE
    n_chunks = x_ref.shape[0] // row_chunk
    zero_rows = jnp.zeros((QK_PAD - QK_NOPE - QK_ROPE, row_chunk), BF16)

    def down(c):
        rows = slice(c * row_chunk, (c + 1) * row_chunk)
        h = (_rmsnorm(x_ref[rows, :], g_ref[...]) * (1.0 + sc1) + sh1).astype(BF16)
        return _dot(h, wdq_ref[...]), _dot(h, wdkv_ref[...])

    def up(cq_raw, dkv):
        cq = _rmsnorm(cq_raw, gq_ref[...]).astype(BF16)
        ckv = _rmsnorm(dkv[:, 0:KV_LORA], gkv_ref[...]).astype(BF16)
        return (_dot_nt(wuqt_ref[...], cq), _dot(ckv, wuk_ref[...]),
                _dot_nt(wuvt_ref[...], ckv))

    def emit(c, dkv, qt, kn, vt):
        rows = slice(c * row_chunk, (c + 1) * row_chunk)
        cos = jnp.tile(cos_ref[rows, :], (1, reps))
        sin = jnp.tile(sin_ref[rows, :], (1, reps))
        cos_t = cos.T[0:QK_ROPE, :]
        sin_t = sin.T[0:QK_ROPE, :]
        for hd in range(MLA_HEADS):
            qt_ref[hd, 0:QK_NOPE, rows] = (qt[hd * QK_NOPE:(hd + 1) * QK_NOPE, :]
                                           * scale).astype(BF16)
            a = qt[n_nope + hd * QK_ROPE:n_nope + (hd + 1) * QK_ROPE, :]
            b = qt[n_nope + n_rope + hd * QK_ROPE:n_nope + n_rope + (hd + 1) * QK_ROPE, :]
            qt_ref[hd, QK_NOPE:QK_NOPE + QK_ROPE, rows] = (
                (a * cos_t + b * sin_t) * scale).astype(BF16)
            qt_ref[hd, QK_NOPE + QK_ROPE:QK_PAD, rows] = zero_rows
        kr = (dkv[:, KV_LORA:KV_LORA + LANES] * cos
              + dkv[:, KV_LORA + LANES:KV_LORA + 2 * LANES] * sin).astype(BF16)
        for hd in range(MLA_HEADS):
            k_ref[hd, rows, 0:LANES] = kn[:, hd * QK_NOPE:(hd + 1) * QK_NOPE].astype(BF16)
            k_ref[hd, rows, LANES:QK_PAD] = kr
            vt_ref[hd, :, rows] = vt[hd * V_DIM:(hd + 1) * V_DIM, :].astype(BF16)

    downs, ups = {}, {}
    for step in range(n_chunks + 2):
        if step < n_chunks:
            downs[step] = down(step)
        if 1 <= step <= n_chunks:
            ups[step - 1] = up(*downs[step - 1])
        if step >= 2:
            emit(step - 2, downs.pop(step - 2)[1], *ups.pop(step - 2))


def _rot_half(w):
    half = w.shape[-1] // 2
    return jnp.concatenate([-w[..., half:], w[..., :half]], axis=-1)


def _mla_proj(x, mods, layer, g, cos_t, sin_t, w_dq, g_q, w_uq, w_dkv, g_kv, w_ukv):
    bsz, s, d = x.shape
    hh = MLA_HEADS
    q_lora = w_dq.shape[1]
    tm, row_chunk = 1024, 512
    wq3 = w_uq.reshape(q_lora, hh, QK_NOPE + QK_ROPE)
    wq_rope = wq3[:, :, QK_NOPE:]
    wuqt_ext = jnp.concatenate([
        wq3[:, :, :QK_NOPE].reshape(q_lora, hh * QK_NOPE),
        wq_rope.reshape(q_lora, hh * QK_ROPE),
        _rot_half(wq_rope).reshape(q_lora, hh * QK_ROPE)], axis=1).T.astype(BF16)
    wkr = w_dkv[:, KV_LORA:]
    pad = jnp.zeros((d, LANES - QK_ROPE), w_dkv.dtype)
    wdkv_ext = jnp.concatenate([w_dkv[:, :KV_LORA], wkr, pad, _rot_half(wkr), pad],
                               axis=1).astype(BF16)
    wkv3 = w_ukv.reshape(KV_LORA, hh, QK_NOPE + V_DIM)
    wuk = wkv3[:, :, :QK_NOPE].reshape(KV_LORA, hh * QK_NOPE).astype(BF16)
    wuvt = wkv3[:, :, QK_NOPE:].reshape(KV_LORA, hh * V_DIM).T.astype(BF16)
    scale = float((QK_NOPE + QK_ROPE) ** -0.5 * math.log2(math.e))

    const = lambda b, i: (0, 0)
    tok = lambda b, i: (b * (s // tm) + i, 0)
    out_map = lambda b, i: (b, 0, i, 0)
    out_map_t = lambda b, i: (b, 0, 0, i)
    return pl.pallas_call(
        functools.partial(_mla_proj_kernel, scale=scale, row_chunk=row_chunk),
        grid=(bsz, s // tm),
        in_specs=[
            pl.BlockSpec((None, tm, d), lambda b, i: (b, i, 0)),
            pl.BlockSpec((None, None, 1, mods.shape[-1]), lambda b, i: (layer, b, 0, 0)),
            pl.BlockSpec((1, d), const),
            pl.BlockSpec((tm, cos_t.shape[-1]), tok),
            pl.BlockSpec((tm, sin_t.shape[-1]), tok),
            pl.BlockSpec(w_dq.shape, const),
            pl.BlockSpec((1, q_lora), const),
            pl.BlockSpec(wuqt_ext.shape, const),
            pl.BlockSpec(wdkv_ext.shape, const),
            pl.BlockSpec((1, KV_LORA), const),
            pl.BlockSpec(wuk.shape, const),
            pl.BlockSpec(wuvt.shape, const),
        ],
        out_specs=[
            pl.BlockSpec((None, hh, QK_PAD, tm), out_map_t),
            pl.BlockSpec((None, hh, tm, QK_PAD), out_map),
            pl.BlockSpec((None, hh, V_DIM, tm), out_map_t),
        ],
        out_shape=[
            jax.ShapeDtypeStruct((bsz, hh, QK_PAD, s), BF16),
            jax.ShapeDtypeStruct((bsz, hh, s, QK_PAD), BF16),
            jax.ShapeDtypeStruct((bsz, hh, V_DIM, s), BF16),
        ],
        compiler_params=_params("parallel", "parallel"),
        name="mla_proj",
    )(x, mods, g.reshape(1, d), cos_t, sin_t, w_dq.astype(BF16), g_q.reshape(1, q_lora),
      wuqt_ext, wdkv_ext, g_kv.reshape(1, KV_LORA), wuk, wuvt)


def _mla_flash_kernel(qt_ref, qn_ref, k_ref, vt_ref, o_ref, vblk_sc, sn_sc, s0_sc, s1_sc,
                      m_sc, acc_sc, *, tq):
    qi = pl.program_id(2)
    half = tq // 2

    @pl.when(qi == 0)
    def _():
        for t in range(vblk_sc.shape[0]):
            vblk_sc[t, 0:V_DIM, :] = vt_ref[:, t * tq:(t + 1) * tq]
            vblk_sc[t, V_DIM:, :] = jnp.ones((ONES_ROWS, tq), BF16)

    m_sc[...] = jnp.full(m_sc.shape, NEG_INF, F32)
    acc_sc[...] = jnp.zeros(acc_sc.shape, F32)

    def scores(t, s_ref):
        k = k_ref[pl.ds(pl.multiple_of(t * tq, tq), tq), :]
        s_ref[...] = _dot(k, qt_ref[...])

    def prefetch():
        sn_sc[...] = _dot(k_ref[0:tq, :], qn_ref[...])

    def scores_diag(s_ref):
        base = pl.multiple_of(qi * tq, tq)
        s_ref[0:half, :] = _dot(k_ref[pl.ds(base, half), :], qt_ref[...])
        s_ref[half:tq, half:tq] = _dot(k_ref[pl.ds(base + half, half), :], qt_ref[:, half:tq])

    def softmax_pv(s, vt, cols):
        m_prev = m_sc[:, cols]
        m_new = jnp.maximum(m_prev, jnp.max(s, axis=0, keepdims=True))
        alpha = jnp.exp2(m_prev - m_new)
        p = jnp.exp2(s - m_new).astype(BF16)
        acc_sc[:, cols] = alpha * acc_sc[:, cols] + _dot(vt, p)
        m_sc[:, cols] = m_new

    def update(t, s_ref):
        softmax_pv(s_ref[...], vblk_sc[t], slice(0, tq))

    def update_diag(s_ref):
        causal = (lax.broadcasted_iota(jnp.int32, (half, half), 0)
                  <= lax.broadcasted_iota(jnp.int32, (half, half), 1))
        vt = vblk_sc[qi]
        s_top = s_ref[0:half, :]
        s_top = jnp.concatenate([jnp.where(causal, s_top[:, 0:half], NEG_INF), s_top[:, half:tq]],
                                axis=1)
        softmax_pv(s_top, vt[:, 0:half], slice(0, tq))
        s_bot = jnp.where(causal, s_ref[half:tq, half:tq], NEG_INF)
        softmax_pv(s_bot, vt[:, half:tq], slice(half, tq))

    def pair(u, carry):
        t = 2 * u + 1
        scores(t + 1, s0_sc)
        update(t, s1_sc)
        scores(t + 2, s1_sc)
        update(t + 1, s0_sc)
        return carry

    @pl.when(qi == 0)
    def _():
        scores_diag(s0_sc)
        prefetch()
        update_diag(s0_sc)

    @pl.when(qi == 1)
    def _():
        scores_diag(s1_sc)
        update(0, sn_sc)
        prefetch()
        update_diag(s1_sc)

    @pl.when(qi >= 2)
    def _():
        scores(1, s1_sc)
        update(0, sn_sc)

    even = qi % 2 == 0
    n_pairs = jnp.where(qi >= 2, jnp.where(even, (qi - 2) // 2, (qi - 3) // 2), 0)
    lax.fori_loop(0, n_pairs, pair, 0)

    @pl.when(jnp.logical_and(qi >= 2, even))
    def _():
        scores_diag(s0_sc)
        update(qi - 1, s1_sc)
        prefetch()
        update_diag(s0_sc)

    @pl.when(jnp.logical_and(qi >= 2, jnp.logical_not(even)))
    def _():
        scores(qi - 1, s0_sc)
        update(qi - 2, s1_sc)
        scores_diag(s1_sc)
        update(qi - 1, s0_sc)
        prefetch()
        update_diag(s1_sc)

    acc = acc_sc[...]
    o_t = acc[0:V_DIM, :] / acc[V_DIM:V_DIM + 1, :]
    o_ref[...] = o_t.T.astype(o_ref.dtype)


def _mla_flash(qt, k, vt):
    bsz, hh, s, _ = k.shape
    tq = 1024
    last = s // tq - 1
    return pl.pallas_call(
        functools.partial(_mla_flash_kernel, tq=tq),
        grid=(bsz, hh, s // tq),
        in_specs=[
            pl.BlockSpec((None, None, QK_PAD, tq), lambda b, h, i: (b, h, 0, i)),
            pl.BlockSpec((None, None, QK_PAD, tq), lambda b, h, i: (b, h, 0, jnp.minimum(i + 1, last))),
            pl.BlockSpec((None, None, s, QK_PAD), lambda b, h, i: (b, h, 0, 0)),
            pl.BlockSpec((None, None, V_DIM, s), lambda b, h, i: (b, h, 0, 0)),
        ],
        out_specs=pl.BlockSpec((None, tq, V_DIM), lambda b, h, i: (b, i, h)),
        out_shape=jax.ShapeDtypeStruct((bsz, s, hh * V_DIM), BF16),
        scratch_shapes=[pltpu.VMEM((s // tq, V_DIM + ONES_ROWS, tq), BF16),
                        pltpu.VMEM((tq, tq), F32), pltpu.VMEM((tq, tq), F32),
                        pltpu.VMEM((tq, tq), F32),
                        pltpu.VMEM((1, tq), F32), pltpu.VMEM((V_DIM + ONES_ROWS, tq), F32)],
        compiler_params=_params("parallel", "parallel", "arbitrary"),
        name="mla_flash",
    )(qt, qt, k, vt)


def _post_mlp_kernel(x_ref, a_ref, mod_ref, wo_ref, bo_ref, g_ref, w1_ref, w2_ref, gf_ref,
                     out_ref, x1_sc, h_sc, acc_sc, *, final_norm, n_ff, row_chunk):
    j = pl.program_id(2)
    d = x_ref.shape[-1]
    tm = x_ref.shape[0]

    def step(first, last):
        mod = mod_ref[...]
        gt1, sh2, sc2 = mod[:, 2 * d:3 * d], mod[:, 3 * d:4 * d], mod[:, 4 * d:5 * d]
        gt2 = mod[:, 5 * d:6 * d]
        w1 = w1_ref[...].astype(BF16)
        w2 = w2_ref[...].astype(BF16)
        for r in range(tm // row_chunk):
            rows = slice(r * row_chunk, (r + 1) * row_chunk)
            if first:
                y = _dot(a_ref[rows, :], wo_ref[...]) + bo_ref[...]
                x1 = x_ref[rows, :] + gt1 * y
                h = (_rmsnorm(x1, g_ref[...]) * (1.0 + sc2) + sh2).astype(BF16)
                h_sc[rows, :] = h
                if not last:
                    x1_sc[rows, :] = x1
            else:
                h = h_sc[rows, :]
            u = jnp.square(jnp.maximum(_dot(h, w1), 0.0)).astype(BF16)
            acc = _dot(u, w2)
            if not first:
                acc = acc_sc[rows, :] + acc
            if last:
                x2 = (x1 if first else x1_sc[rows, :]) + gt2 * acc
                if final_norm:
                    x2 = _rmsnorm(x2, gf_ref[...])
                out_ref[rows, :] = x2
            else:
                acc_sc[rows, :] = acc

    pl.when(j == 0)(functools.partial(step, True, n_ff == 1))
    if n_ff > 1:
        pl.when(j == n_ff - 1)(functools.partial(step, False, True))
    if n_ff > 2:
        pl.when(jnp.logical_and(j > 0, j < n_ff - 1))(functools.partial(step, False, False))


def _post_mlp(x, attn, mods, layer, w_o, b_o, g_mlp, w_ff1, w_ff2, g_final, final_norm):
    bsz, s, d = x.shape
    dff = w_ff1.shape[-1]
    tm, tf, row_chunk = 1024, 1024, 1024
    const = lambda b, i, j: (0, 0)
    tok = lambda b, i, j: (b, i, 0)
    return pl.pallas_call(
        functools.partial(_post_mlp_kernel, final_norm=final_norm, n_ff=dff // tf,
                          row_chunk=row_chunk),
        grid=(bsz, s // tm, dff // tf),
        in_specs=[
            pl.BlockSpec((None, tm, d), tok),
            pl.BlockSpec((None, tm, attn.shape[-1]), tok),
            pl.BlockSpec((None, None, 1, mods.shape[-1]), lambda b, i, j: (layer, b, 0, 0)),
            pl.BlockSpec(w_o.shape, const),
            pl.BlockSpec((1, d), const),
            pl.BlockSpec((1, d), const),
            pl.BlockSpec((None, d, tf), lambda b, i, j: (layer, 0, j)),
            pl.BlockSpec((None, tf, d), lambda b, i, j: (layer, j, 0)),
            pl.BlockSpec((1, d), const),
        ],
        out_specs=pl.BlockSpec((None, tm, d), tok),
        out_shape=jax.ShapeDtypeStruct((bsz, s, d), F32),
        scratch_shapes=[pltpu.VMEM((tm, d), F32), pltpu.VMEM((tm, d), BF16),
                        pltpu.VMEM((tm, d), F32)],
        compiler_params=_params("parallel", "parallel", "arbitrary"),
        name="post_mlp_final" if final_norm else "post_mlp",
    )(x, attn, mods, w_o.astype(BF16), b_o.reshape(1, d), g_mlp.reshape(1, d),
      w_ff1, w_ff2, g_final.reshape(1, d))


def _swa_proj_kernel(x_ref, mod_ref, g_ref, w_ref, b_ref, o_ref, *, scale):
    d = x_ref.shape[-1]
    mod = mod_ref[...]
    sh1, sc1 = mod[:, 0:d], mod[:, d:2 * d]
    h = (_rmsnorm(x_ref[...], g_ref[...]) * (1.0 + sc1) + sh1).astype(BF16)
    qkv = _dot(h, w_ref[...]) + b_ref[...]
    nq = SWA_HEADS * SWA_HEAD_DIM
    o_ref[:, 0:nq] = (qkv[:, 0:nq] * scale).astype(BF16)
    o_ref[:, nq:] = qkv[:, nq:].astype(BF16)


def _swa_window_order(t, axis):
    group = SWA_HEADS // SWA_KV_HEADS
    shape = t.shape
    t = t.reshape(shape[:axis] + (SWA_KV_HEADS, group, SWA_HEAD_DIM) + shape[axis + 1:])
    return jnp.swapaxes(t, axis, axis + 1).reshape(shape)


def _swa_proj(x, mods, layer, g, w_qkv, b_qkv):
    bsz, s, d = x.shape
    n = w_qkv.shape[1]
    tm = 1024
    const = lambda b, i: (0, 0)
    return pl.pallas_call(
        functools.partial(_swa_proj_kernel,
                          scale=float(SWA_HEAD_DIM ** -0.5 * math.log2(math.e))),
        grid=(bsz, s // tm),
        in_specs=[
            pl.BlockSpec((None, tm, d), lambda b, i: (b, i, 0)),
            pl.BlockSpec((None, None, 1, mods.shape[-1]), lambda b, i: (layer, b, 0, 0)),
            pl.BlockSpec((1, d), const),
            pl.BlockSpec(w_qkv.shape, const),
            pl.BlockSpec((1, n), const),
        ],
        out_specs=pl.BlockSpec((None, tm, n), lambda b, i: (b, i, 0)),
        out_shape=jax.ShapeDtypeStruct((bsz, s, n), BF16),
        compiler_params=_params("parallel", "parallel"),
        name="swa_proj",
    )(x, mods, g.reshape(1, d), w_qkv.astype(BF16), b_qkv.reshape(1, n))


def _swa_attn_kernel(sink_ref, q_ref, kc_ref, vc_ref, kp_ref, vp_ref, o_ref, bias_sc, *, slopes):
    first_step = jnp.logical_and(pl.program_id(0) == 0, pl.program_id(1) == 0)
    w = WINDOW
    dh = SWA_HEAD_DIM
    n_kv = SWA_KV_HEADS
    group = SWA_HEADS // n_kv
    gw = n_kv * dh
    log2e = math.log2(math.e)

    @pl.when(first_step)
    def _():
        qpos = lax.broadcasted_iota(jnp.int32, (w, 2 * w), 0)
        kpos = lax.broadcasted_iota(jnp.int32, (w, 2 * w), 1)
        dist = w + qpos - kpos
        in_window = jnp.logical_and(dist >= 0, dist < w)
        dist_f = dist.astype(F32)
        for hd in range(SWA_HEADS):
            bias_sc[hd * w:(hd + 1) * w, :] = jnp.where(
                in_window, dist_f * (-slopes[hd] * log2e), NEG_INF)

    no_prev = jnp.where(pl.program_id(1) == 0, NEG_INF, 0.0)
    ones_cols = jnp.ones((2 * w, w), BF16)
    lane_group = lax.broadcasted_iota(jnp.int32, (w, n_kv * dh), 1) // dh
    low_half = lax.broadcasted_iota(jnp.int32, (w, w), 1) < dh
    half_rows = (n_kv // 2) * group * w
    sink = jnp.concatenate([jnp.full((w, w), sink_ref[hd] * log2e, F32)
                            for hd in range(SWA_HEADS)], axis=0)

    def band(cur_ref, prev_ref, n):
        if n == 0:
            return jnp.concatenate([prev_ref[...], cur_ref[0:w, :]], axis=0)
        return cur_ref[(n - 1) * w:(n + 1) * w, :]

    def stage_scores(n):
        lhs = []
        for g in range(n_kv):
            for a in range(group):
                qwin = q_ref[n * w:(n + 1) * w, a * gw:(a + 1) * gw]
                lhs.append(jnp.where(lane_group == g, qwin, jnp.zeros_like(qwin)))
        s = _dot_nt(jnp.concatenate(lhs, axis=0), band(kc_ref, kp_ref, n)) + bias_sc[...]
        if n == 0:
            s = jnp.concatenate([s[:, 0:w] + no_prev, s[:, w:2 * w]], axis=1)
        return s

    def stage_softmax(s):
        m = jnp.maximum(jnp.max(s, axis=-1, keepdims=True), sink)
        return jnp.exp2(s - jnp.tile(m, (1, 2))).astype(BF16), jnp.exp2(sink - m)

    def stage_out(n, p, e):
        vall = band(vc_ref, vp_ref, n)
        pv_a = _dot(p[0:half_rows], jnp.concatenate([vall[:, 0:w], ones_cols], axis=1))
        pv_b = _dot(p[half_rows:], jnp.concatenate([ones_cols, vall[:, w:2 * w]], axis=1))
        o_a = pv_a[:, 0:w] / (pv_a[:, w:2 * w] + e[0:half_rows])
        o_b = pv_b[:, w:2 * w] / (pv_b[:, 0:w] + e[half_rows:])
        for a in range(group):
            lo = jnp.where(low_half, o_a[a * w:(a + 1) * w], o_a[(group + a) * w:(group + a + 1) * w])
            hi = jnp.where(low_half, o_b[a * w:(a + 1) * w], o_b[(group + a) * w:(group + a + 1) * w])
            o_ref[n * w:(n + 1) * w, a * gw:(a + 1) * gw] = jnp.concatenate(
                [lo, hi], axis=1).astype(o_ref.dtype)

    n_blocks = q_ref.shape[0] // w
    scores, probs = {}, {}
    for step in range(n_blocks + 2):
        if step < n_blocks:
            scores[step] = stage_scores(step)
        if 1 <= step <= n_blocks:
            probs[step - 1] = stage_softmax(scores.pop(step - 1))
        if step >= 2:
            stage_out(step - 2, *probs.pop(step - 2))


def _swa_attn(qkv, sinks):
    bsz, s, _ = qkv.shape
    nq = SWA_HEADS * SWA_HEAD_DIM
    nkv = SWA_KV_HEADS * SWA_HEAD_DIM
    tm = 2048
    per = tm // WINDOW
    slopes = tuple(float(v) for v in 2.0 ** (-8.0 * np.arange(1, SWA_HEADS + 1) / SWA_HEADS))
    kcol, vcol = nq // nkv, nq // nkv + 1
    prev = lambda b, i: jnp.maximum(i * per - 1, 0)
    return pl.pallas_call(
        functools.partial(_swa_attn_kernel, slopes=slopes),
        grid=(bsz, s // tm),
        in_specs=[
            pl.BlockSpec(memory_space=pltpu.SMEM),
            pl.BlockSpec((None, tm, nq), lambda b, i: (b, i, 0)),
            pl.BlockSpec((None, tm, nkv), lambda b, i: (b, i, kcol)),
            pl.BlockSpec((None, tm, nkv), lambda b, i: (b, i, vcol)),
            pl.BlockSpec((None, WINDOW, nkv), lambda b, i: (b, prev(b, i), kcol)),
            pl.BlockSpec((None, WINDOW, nkv), lambda b, i: (b, prev(b, i), vcol)),
        ],
        out_specs=pl.BlockSpec((None, tm, nq), lambda b, i: (b, i, 0)),
        out_shape=jax.ShapeDtypeStruct((bsz, s, nq), BF16),
        scratch_shapes=[pltpu.VMEM((SWA_HEADS * WINDOW, 2 * WINDOW), F32)],
        compiler_params=_params("arbitrary", "arbitrary"),
        name="swa_attn",
    )(sinks.astype(F32), qkv, qkv, qkv, qkv, qkv)


def kernel(x, c, positions, w_ada, b_ada, g_mix, g_mlp, mla_w_dq, mla_g_q, mla_w_uq, mla_w_dkv,
           mla_g_kv, mla_w_ukv, mla_w_o, swa_w_qkv, swa_b_qkv, swa_sinks, swa_w_o, swa_b_o,
           w_ff1, w_ff2, g_final):
    depth = w_ada.shape[0]
    bsz, s, d = x.shape
    mods = _adaln(c, w_ada, b_ada).reshape(depth, bsz, 1, w_ada.shape[-1])
    cos_t, sin_t = _rope_tables(positions)
    zero_bias = jnp.zeros((d,), F32)
    for i in range(depth):
        j = i // 2
        last = i == depth - 1
        if i % 2 == 0:
            q, k, v = _mla_proj(x, mods, i, g_mix[i], cos_t, sin_t, mla_w_dq[j], mla_g_q[j],
                                mla_w_uq[j], mla_w_dkv[j], mla_g_kv[j], mla_w_ukv[j])
            attn = _mla_flash(q, k, v)
            w_o, b_o = mla_w_o[j], zero_bias
        else:
            nq = SWA_HEADS * SWA_HEAD_DIM
            w_qkv = jnp.concatenate([_swa_window_order(swa_w_qkv[j][:, :nq], 1),
                                     swa_w_qkv[j][:, nq:]], axis=1)
            b_qkv = jnp.concatenate([_swa_window_order(swa_b_qkv[j][:nq], 0), swa_b_qkv[j][nq:]])
            qkv = _swa_proj(x, mods, i, g_mix[i], w_qkv, b_qkv)
            attn = _swa_attn(qkv, swa_sinks[j])
            w_o, b_o = _swa_window_order(swa_w_o[j], 0), swa_b_o[j]
        x = _post_mlp(x, attn, mods, i, w_o, b_o, g_mlp[i], w_ff1, w_ff2, g_final, last)
    return x
```

```python
import functools
import math

import numpy as np
import jax
import jax.numpy as jnp
from jax import lax
from jax.experimental import pallas as pl
from jax.experimental.pallas import tpu as pltpu

F32 = jnp.float32
BF16 = jnp.bfloat16

EPS = 1e-6
ROPE_THETA = 10000.0
MLA_HEADS = 8
QK_NOPE = 128
QK_ROPE = 64
V_DIM = 128
KV_LORA = 256
SWA_HEADS = 16
SWA_KV_HEADS = 4
SWA_HEAD_DIM = 64
WINDOW = 128

LANES = 128
QK_PAD = 2 * LANES
ONES_ROWS = 16
VMEM_LIMIT = 56 * 1024 * 1024

NEG_INF = float("-inf")


def _params(*sem):
    return pltpu.CompilerParams(dimension_semantics=sem, vmem_limit_bytes=VMEM_LIMIT)


def _rmsnorm(x, g):
    y = x * lax.rsqrt(jnp.mean(x * x, axis=-1, keepdims=True) + EPS)
    return y * g


def _dot(a, b):
    return jnp.dot(a, b, preferred_element_type=F32)


def _dot_nt(a, b):
    return lax.dot_general(a, b, (((1,), (1,)), ((), ())), preferred_element_type=F32)


def _adaln_kernel(ct_ref, w_ref, b_ref, o_ref):
    ct = ct_ref[...]
    cond = ct * jax.nn.sigmoid(ct)
    w = w_ref[...]
    rows = [jnp.sum(w * cond[:, b:b + 1], axis=0, keepdims=True)
            for b in range(ct.shape[1])]
    o_ref[...] = jnp.concatenate(rows, axis=0) + b_ref[...]


def _adaln(c, w_ada, b_ada):
    depth, d, n = w_ada.shape
    bsz = c.shape[0]
    tn = 1024
    return pl.pallas_call(
        _adaln_kernel,
        grid=(depth, n // tn),
        in_specs=[
            pl.BlockSpec((d, bsz), lambda l, j: (0, 0)),
            pl.BlockSpec((None, d, tn), lambda l, j: (l, 0, j)),
            pl.BlockSpec((None, 1, tn), lambda l, j: (l, 0, j)),
        ],
        out_specs=pl.BlockSpec((None, bsz, tn), lambda l, j: (l, 0, j)),
        out_shape=jax.ShapeDtypeStruct((depth, bsz, n), F32),
        compiler_params=_params("parallel", "parallel"),
        name="adaln",
    )(c.T, w_ada, b_ada.reshape(depth, 1, n))


def _rope_table_kernel(pos_ref, inv_ref, cos_ref, sin_ref):
    ang = pos_ref[...].astype(F32) * inv_ref[...]
    cos_ref[...] = jnp.cos(ang)
    sin_ref[...] = jnp.sin(ang)


def _rope_tables(positions):
    n = positions.size
    half = QK_ROPE // 2
    per_row = LANES // half
    inv_freq = ROPE_THETA ** (-jnp.arange(half, dtype=F32) / half)
    inv = jnp.tile(inv_freq, per_row).reshape(1, LANES)
    pos_rows = jnp.repeat(positions.reshape(n // per_row, per_row), half, axis=1)
    rows = n // per_row
    tm = 512
    cos_t, sin_t = pl.pallas_call(
        _rope_table_kernel,
        grid=(rows // tm,),
        in_specs=[pl.BlockSpec((tm, LANES), lambda i: (i, 0)),
                  pl.BlockSpec((1, LANES), lambda i: (0, 0))],
        out_specs=[pl.BlockSpec((tm, LANES), lambda i: (i, 0)),
                   pl.BlockSpec((tm, LANES), lambda i: (i, 0))],
        out_shape=[jax.ShapeDtypeStruct((rows, LANES), F32)] * 2,
        compiler_params=_params("parallel"),
        name="rope_table",
    )(pos_rows, inv)
    return cos_t.reshape(n, half), sin_t.reshape(n, half)


def _mla_proj_kernel(x_ref, mod_ref, g_ref, cos_ref, sin_ref, wdq_ref, gq_ref, wuqt_ref,
                     wdkv_ref, gkv_ref, wuk_ref, wuvt_ref, qt_ref, k_ref, vt_ref, *, scale,
                     row_chunk):
    d = x_ref.shape[-1]
    mod = mod_ref[...]
    sh1, sc1 = mod[:, 0:d], mod[:, d:2 * d]
    reps = LANES // cos_ref.shape[-1]
    n_chunks = x_ref.shape[0] // row_chunk
    zero_rows = jnp.zeros((QK_PAD - QK_NOPE - QK_ROPE, row_chunk), BF16)

    def down(c):
        rows = slice(c * row_chunk, (c + 1) * row_chunk)
        h = (_rmsnorm(x_ref[rows, :], g_ref[...]) * (1.0 + sc1) + sh1).astype(BF16)
        return _dot(h, wdq_ref[...]), _dot(h, wdkv_ref[...])

    def up(cq_raw, dkv):
        cq = _rmsnorm(cq_raw, gq_ref[...]).astype(BF16)
        ckv = _rmsnorm(dkv[:, 0:KV_LORA], gkv_ref[...]).astype(BF16)
        return (_dot_nt(wuqt_ref[...], cq), _dot(ckv, wuk_ref[...]),
                _dot_nt(wuvt_ref[...], ckv))

    def emit(c, dkv, qt, kn, vt):
        rows = slice(c * row_chunk, (c + 1) * row_chunk)
        cos = jnp.tile(cos_ref[rows, :], (1, reps))
        sin = jnp.tile(sin_ref[rows, :], (1, reps))
        cos_t = cos.T[0:QK_ROPE, :]
        sin_t = sin.T[0:QK_ROPE, :]
        for hd in range(MLA_HEADS):
            base = hd * (QK_NOPE + QK_ROPE)
            qt_ref[hd, 0:QK_NOPE, rows] = (qt[base:base + QK_NOPE, :] * scale).astype(BF16)
            a = qt[base + QK_NOPE:base + QK_NOPE + QK_ROPE, :]
            b = jnp.concatenate([-a[QK_ROPE // 2:, :], a[0:QK_ROPE // 2, :]], axis=0)
            qt_ref[hd, QK_NOPE:QK_NOPE + QK_ROPE, rows] = (
                (a * cos_t + b * sin_t) * scale).astype(BF16)
            qt_ref[hd, QK_NOPE + QK_ROPE:QK_PAD, rows] = zero_rows
        kr = (dkv[:, KV_LORA:KV_LORA + LANES] * cos
              + dkv[:, KV_LORA + LANES:KV_LORA + 2 * LANES] * sin).astype(BF16)
        for hd in range(MLA_HEADS):
            k_ref[hd, rows, 0:LANES] = kn[:, hd * QK_NOPE:(hd + 1) * QK_NOPE].astype(BF16)
            k_ref[hd, rows, LANES:QK_PAD] = kr
            vt_ref[hd, :, rows] = vt[hd * V_DIM:(hd + 1) * V_DIM, :].astype(BF16)

    downs, ups = {}, {}
    for step in range(n_chunks + 2):
        if step < n_chunks:
            downs[step] = down(step)
        if 1 <= step <= n_chunks:
            ups[step - 1] = up(*downs[step - 1])
        if step >= 2:
            emit(step - 2, downs.pop(step - 2)[1], *ups.pop(step - 2))


def _rot_half(w):
    half = w.shape[-1] // 2
    return jnp.concatenate([-w[..., half:], w[..., :half]], axis=-1)


def _mla_proj(x, mods, layer, g, cos_t, sin_t, w_dq, g_q, w_uq, w_dkv, g_kv, w_ukv):
    bsz, s, d = x.shape
    hh = MLA_HEADS
    q_lora = w_dq.shape[1]
    tm, row_chunk = 1024, 512
    wuqt_ext = w_uq.T.astype(BF16)
    wkr = w_dkv[:, KV_LORA:]
    pad = jnp.zeros((d, LANES - QK_ROPE), w_dkv.dtype)
    wdkv_ext = jnp.concatenate([w_dkv[:, :KV_LORA], wkr, pad, _rot_half(wkr), pad],
                               axis=1).astype(BF16)
    wkv3 = w_ukv.reshape(KV_LORA, hh, QK_NOPE + V_DIM)
    wuk = wkv3[:, :, :QK_NOPE].reshape(KV_LORA, hh * QK_NOPE).astype(BF16)
    wuvt = wkv3[:, :, QK_NOPE:].reshape(KV_LORA, hh * V_DIM).T.astype(BF16)
    scale = float((QK_NOPE + QK_ROPE) ** -0.5 * math.log2(math.e))

    const = lambda b, i: (0, 0)
    tok = lambda b, i: (b * (s // tm) + i, 0)
    out_map = lambda b, i: (b, 0, i, 0)
    out_map_t = lambda b, i: (b, 0, 0, i)
    return pl.pallas_call(
        functools.partial(_mla_proj_kernel, scale=scale, row_chunk=row_chunk),
        grid=(bsz, s // tm),
        in_specs=[
            pl.BlockSpec((None, tm, d), lambda b, i: (b, i, 0)),
            pl.BlockSpec((None, None, 1, mods.shape[-1]), lambda b, i: (layer, b, 0, 0)),
            pl.BlockSpec((1, d), const),
            pl.BlockSpec((tm, cos_t.shape[-1]), tok),
            pl.BlockSpec((tm, sin_t.shape[-1]), tok),
            pl.BlockSpec(w_dq.shape, const),
            pl.BlockSpec((1, q_lora), const),
            pl.BlockSpec(wuqt_ext.shape, const),
            pl.BlockSpec(wdkv_ext.shape, const),
            pl.BlockSpec((1, KV_LORA), const),
            pl.BlockSpec(wuk.shape, const),
            pl.BlockSpec(wuvt.shape, const),
        ],
        out_specs=[
            pl.BlockSpec((None, hh, QK_PAD, tm), out_map_t),
            pl.BlockSpec((None, hh, tm, QK_PAD), out_map),
            pl.BlockSpec((None, hh, V_DIM, tm), out_map_t),
        ],
        out_shape=[
            jax.ShapeDtypeStruct((bsz, hh, QK_PAD, s), BF16),
            jax.ShapeDtypeStruct((bsz, hh, s, QK_PAD), BF16),
            jax.ShapeDtypeStruct((bsz, hh, V_DIM, s), BF16),
        ],
        compiler_params=_params("parallel", "parallel"),
        name="mla_proj",
    )(x, mods, g.reshape(1, d), cos_t, sin_t, w_dq.astype(BF16), g_q.reshape(1, q_lora),
      wuqt_ext, wdkv_ext, g_kv.reshape(1, KV_LORA), wuk, wuvt)


def _mla_flash_kernel(qt_ref, qn_ref, k_ref, vt_ref, o_ref, vblk_sc, sn_sc, s0_sc, s1_sc,
                      m_sc, acc_sc, *, tq):
    qi = pl.program_id(2)
    half = tq // 2

    @pl.when(qi == 0)
    def _():
        for t in range(vblk_sc.shape[0]):
            vblk_sc[t, 0:V_DIM, :] = vt_ref[:, t * tq:(t + 1) * tq]
            vblk_sc[t, V_DIM:, :] = jnp.ones((ONES_ROWS, tq), BF16)

    m_sc[...] = jnp.full(m_sc.shape, NEG_INF, F32)
    acc_sc[...] = jnp.zeros(acc_sc.shape, F32)

    def scores(t, s_ref):
        k = k_ref[pl.ds(pl.multiple_of(t * tq, tq), tq), :]
        s_ref[...] = _dot(k, qt_ref[...])

    def prefetch():
        sn_sc[...] = _dot(k_ref[0:tq, :], qn_ref[...])

    def scores_diag(s_ref):
        base = pl.multiple_of(qi * tq, tq)
        s_ref[0:half, :] = _dot(k_ref[pl.ds(base, half), :], qt_ref[...])
        s_ref[half:tq, half:tq] = _dot(k_ref[pl.ds(base + half, half), :], qt_ref[:, half:tq])

    def softmax_pv(s, vt, cols):
        m_prev = m_sc[:, cols]
        m_new = jnp.maximum(m_prev, jnp.max(s, axis=0, keepdims=True))
        alpha = jnp.exp2(m_prev - m_new)
        p = jnp.exp2(s - m_new).astype(BF16)
        acc_sc[:, cols] = alpha * acc_sc[:, cols] + _dot(vt, p)
        m_sc[:, cols] = m_new

    def update(t, s_ref):
        softmax_pv(s_ref[...], vblk_sc[t], slice(0, tq))

    def update_diag(s_ref):
        causal = (lax.broadcasted_iota(jnp.int32, (half, half), 0)
                  <= lax.broadcasted_iota(jnp.int32, (half, half), 1))
        vt = vblk_sc[qi]
        s_top = s_ref[0:half, :]
        s_top = jnp.concatenate([jnp.where(causal, s_top[:, 0:half], NEG_INF), s_top[:, half:tq]],
                                axis=1)
        softmax_pv(s_top, vt[:, 0:half], slice(0, tq))
        s_bot = jnp.where(causal, s_ref[half:tq, half:tq], NEG_INF)
        softmax_pv(s_bot, vt[:, half:tq], slice(half, tq))

    def pair(u, carry):
        t = 2 * u + 1
        scores(t + 1, s0_sc)
        update(t, s1_sc)
        scores(t + 2, s1_sc)
        update(t + 1, s0_sc)
        return carry

    @pl.when(qi == 0)
    def _():
        scores_diag(s0_sc)
        prefetch()
        update_diag(s0_sc)

    @pl.when(qi == 1)
    def _():
        scores_diag(s1_sc)
        update(0, sn_sc)
        prefetch()
        update_diag(s1_sc)

    @pl.when(qi >= 2)
    def _():
        scores(1, s1_sc)
        update(0, sn_sc)

    even = qi % 2 == 0
    n_pairs = jnp.where(qi >= 2, jnp.where(even, (qi - 2) // 2, (qi - 3) // 2), 0)
    lax.fori_loop(0, n_pairs, pair, 0)

    @pl.when(jnp.logical_and(qi >= 2, even))
    def _():
        scores_diag(s0_sc)
        update(qi - 1, s1_sc)
        prefetch()
        update_diag(s0_sc)

    @pl.when(jnp.logical_and(qi >= 2, jnp.logical_not(even)))
    def _():
        scores(qi - 1, s0_sc)
        update(qi - 2, s1_sc)
        scores_diag(s1_sc)
        update(qi - 1, s0_sc)
        prefetch()
        update_diag(s1_sc)

    acc = acc_sc[...]
    o_t = acc[0:V_DIM, :] / acc[V_DIM:V_DIM + 1, :]
    o_ref[...] = o_t.T.astype(o_ref.dtype)


def _mla_flash(qt, k, vt):
    bsz, hh, s, _ = k.shape
    tq = 1024
    last = s // tq - 1
    return pl.pallas_call(
        functools.partial(_mla_flash_kernel, tq=tq),
        grid=(bsz, hh, s // tq),
        in_specs=[
            pl.BlockSpec((None, None, QK_PAD, tq), lambda b, h, i: (b, h, 0, i)),
            pl.BlockSpec((None, None, QK_PAD, tq), lambda b, h, i: (b, h, 0, jnp.minimum(i + 1, last))),
            pl.BlockSpec((None, None, s, QK_PAD), lambda b, h, i: (b, h, 0, 0)),
            pl.BlockSpec((None, None, V_DIM, s), lambda b, h, i: (b, h, 0, 0)),
        ],
        out_specs=pl.BlockSpec((None, tq, V_DIM), lambda b, h, i: (b, i, h)),
        out_shape=jax.ShapeDtypeStruct((bsz, s, hh * V_DIM), BF16),
        scratch_shapes=[pltpu.VMEM((s // tq, V_DIM + ONES_ROWS, tq), BF16),
                        pltpu.VMEM((tq, tq), F32), pltpu.VMEM((tq, tq), F32),
                        pltpu.VMEM((tq, tq), F32),
                        pltpu.VMEM((1, tq), F32), pltpu.VMEM((V_DIM + ONES_ROWS, tq), F32)],
        compiler_params=_params("parallel", "parallel", "arbitrary"),
        name="mla_flash",
    )(qt, qt, k, vt)


def _post_mlp_kernel(x_ref, a_ref, mod_ref, wo_ref, bo_ref, g_ref, w1_ref, w2_ref, gf_ref,
                     out_ref, x1_sc, h_sc, acc_sc, *, final_norm, n_ff, row_chunk):
    j = pl.program_id(2)
    d = x_ref.shape[-1]
    tm = x_ref.shape[0]

    def step(first, last):
        mod = mod_ref[...]
        gt1, sh2, sc2 = mod[:, 2 * d:3 * d], mod[:, 3 * d:4 * d], mod[:, 4 * d:5 * d]
        gt2 = mod[:, 5 * d:6 * d]
        w1 = w1_ref[...].astype(BF16)
        w2 = w2_ref[...].astype(BF16)
        for r in range(tm // row_chunk):
            rows = slice(r * row_chunk, (r + 1) * row_chunk)
            if first:
                y = _dot(a_ref[rows, :], wo_ref[...]) + bo_ref[...]
                x1 = x_ref[rows, :] + gt1 * y
                h = (_rmsnorm(x1, g_ref[...]) * (1.0 + sc2) + sh2).astype(BF16)
                h_sc[rows, :] = h
                if not last:
                    x1_sc[rows, :] = x1
            else:
                h = h_sc[rows, :]
            u = jnp.square(jnp.maximum(_dot(h, w1), 0.0)).astype(BF16)
            acc = _dot(u, w2)
            if not first:
                acc = acc_sc[rows, :] + acc
            if last:
                x2 = (x1 if first else x1_sc[rows, :]) + gt2 * acc
                if final_norm:
                    x2 = _rmsnorm(x2, gf_ref[...])
                out_ref[rows, :] = x2
            else:
                acc_sc[rows, :] = acc

    pl.when(j == 0)(functools.partial(step, True, n_ff == 1))
    if n_ff > 1:
        pl.when(j == n_ff - 1)(functools.partial(step, False, True))
    if n_ff > 2:
        pl.when(jnp.logical_and(j > 0, j < n_ff - 1))(functools.partial(step, False, False))


def _post_mlp(x, attn, mods, layer, w_o, b_o, g_mlp, w_ff1, w_ff2, g_final, final_norm):
    bsz, s, d = x.shape
    dff = w_ff1.shape[-1]
    tm, tf, row_chunk = 1024, 1024, 1024
    const = lambda b, i, j: (0, 0)
    tok = lambda b, i, j: (b, i, 0)
    return pl.pallas_call(
        functools.partial(_post_mlp_kernel, final_norm=final_norm, n_ff=dff // tf,
                          row_chunk=row_chunk),
        grid=(bsz, s // tm, dff // tf),
        in_specs=[
            pl.BlockSpec((None, tm, d), tok),
            pl.BlockSpec((None, tm, attn.shape[-1]), tok),
            pl.BlockSpec((None, None, 1, mods.shape[-1]), lambda b, i, j: (layer, b, 0, 0)),
            pl.BlockSpec(w_o.shape, const),
            pl.BlockSpec((1, d), const),
            pl.BlockSpec((1, d), const),
            pl.BlockSpec((None, d, tf), lambda b, i, j: (layer, 0, j)),
            pl.BlockSpec((None, tf, d), lambda b, i, j: (layer, j, 0)),
            pl.BlockSpec((1, d), const),
        ],
        out_specs=pl.BlockSpec((None, tm, d), tok),
        out_shape=jax.ShapeDtypeStruct((bsz, s, d), F32),
        scratch_shapes=[pltpu.VMEM((tm, d), F32), pltpu.VMEM((tm, d), BF16),
                        pltpu.VMEM((tm, d), F32)],
        compiler_params=_params("parallel", "parallel", "arbitrary"),
        name="post_mlp_final" if final_norm else "post_mlp",
    )(x, attn, mods, w_o.astype(BF16), b_o.reshape(1, d), g_mlp.reshape(1, d),
      w_ff1, w_ff2, g_final.reshape(1, d))


def _swa_proj_kernel(x_ref, mod_ref, g_ref, w_ref, b_ref, o_ref, *, scale):
    d = x_ref.shape[-1]
    mod = mod_ref[...]
    sh1, sc1 = mod[:, 0:d], mod[:, d:2 * d]
    h = (_rmsnorm(x_ref[...], g_ref[...]) * (1.0 + sc1) + sh1).astype(BF16)
    qkv = _dot(h, w_ref[...]) + b_ref[...]
    nq = SWA_HEADS * SWA_HEAD_DIM
    o_ref[:, 0:nq] = (qkv[:, 0:nq] * scale).astype(BF16)
    o_ref[:, nq:] = qkv[:, nq:].astype(BF16)


def _swa_window_order(t, axis):
    group = SWA_HEADS // SWA_KV_HEADS
    shape = t.shape
    t = t.reshape(shape[:axis] + (SWA_KV_HEADS, group, SWA_HEAD_DIM) + shape[axis + 1:])
    return jnp.swapaxes(t, axis, axis + 1).reshape(shape)


def _swa_proj(x, mods, layer, g, w_qkv, b_qkv):
    bsz, s, d = x.shape
    n = w_qkv.shape[1]
    tm = 1024
    const = lambda b, i: (0, 0)
    return pl.pallas_call(
        functools.partial(_swa_proj_kernel,
                          scale=float(SWA_HEAD_DIM ** -0.5 * math.log2(math.e))),
        grid=(bsz, s // tm),
        in_specs=[
            pl.BlockSpec((None, tm, d), lambda b, i: (b, i, 0)),
            pl.BlockSpec((None, None, 1, mods.shape[-1]), lambda b, i: (layer, b, 0, 0)),
            pl.BlockSpec((1, d), const),
            pl.BlockSpec(w_qkv.shape, const),
            pl.BlockSpec((1, n), const),
        ],
        out_specs=pl.BlockSpec((None, tm, n), lambda b, i: (b, i, 0)),
        out_shape=jax.ShapeDtypeStruct((bsz, s, n), BF16),
        compiler_params=_params("parallel", "parallel"),
        name="swa_proj",
    )(x, mods, g.reshape(1, d), w_qkv.astype(BF16), b_qkv.reshape(1, n))


def _swa_attn_kernel(sink_ref, q_ref, kc_ref, vc_ref, kp_ref, vp_ref, o_ref, bias_sc, *, slopes):
    first_step = jnp.logical_and(pl.program_id(0) == 0, pl.program_id(1) == 0)
    w = WINDOW
    dh = SWA_HEAD_DIM
    n_kv = SWA_KV_HEADS
    group = SWA_HEADS // n_kv
    gw = n_kv * dh
    log2e = math.log2(math.e)

    @pl.when(first_step)
    def _():
        qpos = lax.broadcasted_iota(jnp.int32, (w, 2 * w), 0)
        kpos = lax.broadcasted_iota(jnp.int32, (w, 2 * w), 1)
        dist = w + qpos - kpos
        in_window = jnp.logical_and(dist >= 0, dist < w)
        dist_f = dist.astype(F32)
        for hd in range(SWA_HEADS):
            bias_sc[hd * w:(hd + 1) * w, :] = jnp.where(
                in_window, dist_f * (-slopes[hd] * log2e), NEG_INF)

    no_prev = jnp.where(pl.program_id(1) == 0, NEG_INF, 0.0)
    ones_cols = jnp.ones((2 * w, w), BF16)
    lane_group = lax.broadcasted_iota(jnp.int32, (w, n_kv * dh), 1) // dh
    low_half = lax.broadcasted_iota(jnp.int32, (w, w), 1) < dh
    half_rows = (n_kv // 2) * group * w
    sink = jnp.concatenate([jnp.full((w, w), sink_ref[hd] * log2e, F32)
                            for hd in range(SWA_HEADS)], axis=0)

    def band(cur_ref, prev_ref, n):
        if n == 0:
            return jnp.concatenate([prev_ref[...], cur_ref[0:w, :]], axis=0)
        return cur_ref[(n - 1) * w:(n + 1) * w, :]

    def stage_scores(n):
        lhs = []
        for g in range(n_kv):
            for a in range(group):
                qwin = q_ref[n * w:(n + 1) * w, a * gw:(a + 1) * gw]
                lhs.append(jnp.where(lane_group == g, qwin, jnp.zeros_like(qwin)))
        s = _dot_nt(jnp.concatenate(lhs, axis=0), band(kc_ref, kp_ref, n)) + bias_sc[...]
        if n == 0:
            s = jnp.concatenate([s[:, 0:w] + no_prev, s[:, w:2 * w]], axis=1)
        return s

    def stage_softmax(s):
        m = jnp.maximum(jnp.max(s, axis=-1, keepdims=True), sink)
        return jnp.exp2(s - jnp.tile(m, (1, 2))).astype(BF16), jnp.exp2(sink - m)

    def stage_out(n, p, e):
        vall = band(vc_ref, vp_ref, n)
        pv_a = _dot(p[0:half_rows], jnp.concatenate([vall[:, 0:w], ones_cols], axis=1))
        pv_b = _dot(p[half_rows:], jnp.concatenate([ones_cols, vall[:, w:2 * w]], axis=1))
        o_a = pv_a[:, 0:w] / (pv_a[:, w:2 * w] + e[0:half_rows])
        o_b = pv_b[:, w:2 * w] / (pv_b[:, 0:w] + e[half_rows:])
        for a in range(group):
            lo = jnp.where(low_half, o_a[a * w:(a + 1) * w], o_a[(group + a) * w:(group + a + 1) * w])
            hi = jnp.where(low_half, o_b[a * w:(a + 1) * w], o_b[(group + a) * w:(group + a + 1) * w])
            o_ref[n * w:(n + 1) * w, a * gw:(a + 1) * gw] = jnp.concatenate(
                [lo, hi], axis=1).astype(o_ref.dtype)

    n_blocks = q_ref.shape[0] // w
    scores, probs = {}, {}
    for step in range(n_blocks + 2):
        if step < n_blocks:
            scores[step] = stage_scores(step)
        if 1 <= step <= n_blocks:
            probs[step - 1] = stage_softmax(scores.pop(step - 1))
        if step >= 2:
            stage_out(step - 2, *probs.pop(step - 2))


def _swa_attn(qkv, sinks):
    bsz, s, _ = qkv.shape
    nq = SWA_HEADS * SWA_HEAD_DIM
    nkv = SWA_KV_HEADS * SWA_HEAD_DIM
    tm = 2048
    per = tm // WINDOW
    slopes = tuple(float(v) for v in 2.0 ** (-8.0 * np.arange(1, SWA_HEADS + 1) / SWA_HEADS))
    kcol, vcol = nq // nkv, nq // nkv + 1
    prev = lambda b, i: jnp.maximum(i * per - 1, 0)
    return pl.pallas_call(
        functools.partial(_swa_attn_kernel, slopes=slopes),
        grid=(bsz, s // tm),
        in_specs=[
            pl.BlockSpec(memory_space=pltpu.SMEM),
            pl.BlockSpec((None, tm, nq), lambda b, i: (b, i, 0)),
            pl.BlockSpec((None, tm, nkv), lambda b, i: (b, i, kcol)),
            pl.BlockSpec((None, tm, nkv), lambda b, i: (b, i, vcol)),
            pl.BlockSpec((None, WINDOW, nkv), lambda b, i: (b, prev(b, i), kcol)),
            pl.BlockSpec((None, WINDOW, nkv), lambda b, i: (b, prev(b, i), vcol)),
        ],
        out_specs=pl.BlockSpec((None, tm, nq), lambda b, i: (b, i, 0)),
        out_shape=jax.ShapeDtypeStruct((bsz, s, nq), BF16),
        scratch_shapes=[pltpu.VMEM((SWA_HEADS * WINDOW, 2 * WINDOW), F32)],
        compiler_params=_params("arbitrary", "arbitrary"),
        name="swa_attn",
    )(sinks.astype(F32), qkv, qkv, qkv, qkv, qkv)


def kernel(x, c, positions, w_ada, b_ada, g_mix, g_mlp, mla_w_dq, mla_g_q, mla_w_uq, mla_w_dkv,
           mla_g_kv, mla_w_ukv, mla_w_o, swa_w_qkv, swa_b_qkv, swa_sinks, swa_w_o, swa_b_o,
           w_ff1, w_ff2, g_final):
    depth = w_ada.shape[0]
    bsz, s, d = x.shape
    mods = _adaln(c, w_ada, b_ada).reshape(depth, bsz, 1, w_ada.shape[-1])
    cos_t, sin_t = _rope_tables(positions)
    zero_bias = jnp.zeros((d,), F32)
    for i in range(depth):
        j = i // 2
        last = i == depth - 1
        if i % 2 == 0:
            q, k, v = _mla_proj(x, mods, i, g_mix[i], cos_t, sin_t, mla_w_dq[j], mla_g_q[j],
                                mla_w_uq[j], mla_w_dkv[j], mla_g_kv[j], mla_w_ukv[j])
            attn = _mla_flash(q, k, v)
            w_o, b_o = mla_w_o[j], zero_bias
        else:
            nq = SWA_HEADS * SWA_HEAD_DIM
            w_qkv = swa_w_qkv[j].astype(BF16)
            w_qkv = jnp.concatenate([_swa_window_order(w_qkv[:, :nq], 1), w_qkv[:, nq:]], axis=1)
            b_qkv = jnp.concatenate([_swa_window_order(swa_b_qkv[j][:nq], 0), swa_b_qkv[j][nq:]])
            qkv = _swa_proj(x, mods, i, g_mix[i], w_qkv, b_qkv)
            attn = _swa_attn(qkv, swa_sinks[j])
            w_o, b_o = _swa_window_order(swa_w_o[j].astype(BF16), 0), swa_b_o[j]
        x = _post_mlp(x, attn, mods, i, w_o, b_o, g_mlp[i], w_ff1, w_ff2, g_final, last)
    return x
```

```python
import functools
import math
from typing import NamedTuple

import numpy as np
import jax
import jax.numpy as jnp
from jax import lax
from jax.experimental import pallas as pl
from jax.experimental.pallas import tpu as pltpu

F32 = jnp.float32
BF16 = jnp.bfloat16

EPS = 1e-6
ROPE_THETA = 10000.0
MLA_HEADS = 8
QK_NOPE = 128
QK_ROPE = 64
V_DIM = 128
KV_LORA = 256
SWA_HEADS = 16
SWA_KV_HEADS = 4
SWA_HEAD_DIM = 64
WINDOW = 128

LANES = 128
QK_PAD = 2 * LANES
ONES_ROWS = 16
VMEM_LIMIT = 56 * 1024 * 1024

NEG_INF = float("-inf")


class _Tiles(NamedTuple):
    adaln_cols: int = 1024
    rope_rows: int = 512
    mla_proj: int = 1024
    mla_proj_chunk: int = 1024
    flash: int = 1024
    mlp: int = 1024
    mlp_ff: int = 1024
    mlp_chunk: int = 1024
    swa_proj: int = 1024
    swa_attn: int = 2048


TILES = _Tiles()


def _params(*sem):
    return pltpu.CompilerParams(dimension_semantics=sem, vmem_limit_bytes=VMEM_LIMIT)


def _rmsnorm(x, g):
    y = x * lax.rsqrt(jnp.mean(x * x, axis=-1, keepdims=True) + EPS)
    return y * g


def _dot(a, b):
    return jnp.dot(a, b, preferred_element_type=F32)


def _dot_nt(a, b):
    return lax.dot_general(a, b, (((1,), (1,)), ((), ())), preferred_element_type=F32)


def _adaln_kernel(ct_ref, w_ref, b_ref, o_ref):
    ct = ct_ref[...]
    cond = ct * jax.nn.sigmoid(ct)
    w = w_ref[...]
    rows = [jnp.sum(w * cond[:, b:b + 1], axis=0, keepdims=True)
            for b in range(ct.shape[1])]
    o_ref[...] = jnp.concatenate(rows, axis=0) + b_ref[...]


def _adaln(c, w_ada, b_ada):
    depth, d, n = w_ada.shape
    bsz = c.shape[0]
    tn = TILES.adaln_cols
    return pl.pallas_call(
        _adaln_kernel,
        grid=(depth, n // tn),
        in_specs=[
            pl.BlockSpec((d, bsz), lambda l, j: (0, 0)),
            pl.BlockSpec((None, d, tn), lambda l, j: (l, 0, j)),
            pl.BlockSpec((None, 1, tn), lambda l, j: (l, 0, j)),
        ],
        out_specs=pl.BlockSpec((None, bsz, tn), lambda l, j: (l, 0, j)),
        out_shape=jax.ShapeDtypeStruct((depth, bsz, n), F32),
        compiler_params=_params("parallel", "parallel"),
        name="adaln",
    )(c.T, w_ada, b_ada.reshape(depth, 1, n))


def _rope_table_kernel(pos_ref, inv_ref, cos_ref, sin_ref):
    ang = pos_ref[...].astype(F32) * inv_ref[...]
    cos_ref[...] = jnp.cos(ang)
    sin_ref[...] = jnp.sin(ang)


def _rope_tables(positions):
    n = positions.size
    half = QK_ROPE // 2
    per_row = LANES // half
    inv_freq = ROPE_THETA ** (-jnp.arange(half, dtype=F32) / half)
    inv = jnp.tile(inv_freq, per_row).reshape(1, LANES)
    pos_rows = jnp.repeat(positions.reshape(n // per_row, per_row), half, axis=1)
    rows = n // per_row
    tm = TILES.rope_rows
    cos_t, sin_t = pl.pallas_call(
        _rope_table_kernel,
        grid=(rows // tm,),
        in_specs=[pl.BlockSpec((tm, LANES), lambda i: (i, 0)),
                  pl.BlockSpec((1, LANES), lambda i: (0, 0))],
        out_specs=[pl.BlockSpec((tm, LANES), lambda i: (i, 0)),
                   pl.BlockSpec((tm, LANES), lambda i: (i, 0))],
        out_shape=[jax.ShapeDtypeStruct((rows, LANES), F32)] * 2,
        compiler_params=_params("parallel"),
        name="rope_table",
    )(pos_rows, inv)
    return cos_t.reshape(n, half), sin_t.reshape(n, half)


def _mla_proj_kernel(x_ref, mod_ref, g_ref, cos_ref, sin_ref, wdq_ref, gq_ref, wuqt_ref,
                     wdkv_ref, gkv_ref, wuk_ref, wuvt_ref, qt_ref, k_ref, vt_ref, *, scale,
                     row_chunk):
    d = x_ref.shape[-1]
    mod = mod_ref[...]
    sh1, sc1 = mod[:, 0:d], mod[:, d:2 * d]
    reps = LANES // cos_ref.shape[-1]
    n_chunks = x_ref.shape[0] // row_chunk
    zero_rows = jnp.zeros((QK_PAD - QK_NOPE - QK_ROPE, row_chunk), BF16)

    def down(c):
        rows = slice(c * row_chunk, (c + 1) * row_chunk)
        h = (_rmsnorm(x_ref[rows, :], g_ref[...]) * (1.0 + sc1) + sh1).astype(BF16)
        return _dot(h, wdq_ref[...]), _dot(h, wdkv_ref[...])

    def up(cq_raw, dkv):
        cq = _rmsnorm(cq_raw, gq_ref[...]).astype(BF16)
        ckv = _rmsnorm(dkv[:, 0:KV_LORA], gkv_ref[...]).astype(BF16)
        return (_dot_nt(wuqt_ref[...], cq), _dot(ckv, wuk_ref[...]),
                _dot_nt(wuvt_ref[...], ckv))

    def emit(c, dkv, qt, kn, vt):
        rows = slice(c * row_chunk, (c + 1) * row_chunk)
        cos = jnp.tile(cos_ref[rows, :], (1, reps))
        sin = jnp.tile(sin_ref[rows, :], (1, reps))
        cos_t = cos.T[0:QK_ROPE, :]
        sin_t = sin.T[0:QK_ROPE, :]
        for hd in range(MLA_HEADS):
            base = hd * (QK_NOPE + QK_ROPE)
            qt_ref[hd, 0:QK_NOPE, rows] = (qt[base:base + QK_NOPE, :] * scale).astype(BF16)
            a = qt[base + QK_NOPE:base + QK_NOPE + QK_ROPE, :]
            b = jnp.concatenate([-a[QK_ROPE // 2:, :], a[0:QK_ROPE // 2, :]], axis=0)
            qt_ref[hd, QK_NOPE:QK_NOPE + QK_ROPE, rows] = (
                (a * cos_t + b * sin_t) * scale).astype(BF16)
            qt_ref[hd, QK_NOPE + QK_ROPE:QK_PAD, rows] = zero_rows
        kr = (dkv[:, KV_LORA:KV_LORA + LANES] * cos
              + dkv[:, KV_LORA + LANES:KV_LORA + 2 * LANES] * sin).astype(BF16)
        for hd in range(MLA_HEADS):
            k_ref[hd, rows, 0:LANES] = kn[:, hd * QK_NOPE:(hd + 1) * QK_NOPE].astype(BF16)
            k_ref[hd, rows, LANES:QK_PAD] = kr
            vt_ref[hd, :, rows] = vt[hd * V_DIM:(hd + 1) * V_DIM, :].astype(BF16)

    downs, ups = {}, {}
    for step in range(n_chunks + 2):
        if step < n_chunks:
            downs[step] = down(step)
        if 1 <= step <= n_chunks:
            ups[step - 1] = up(*downs[step - 1])
        if step >= 2:
            emit(step - 2, downs.pop(step - 2)[1], *ups.pop(step - 2))


def _rot_half(w):
    half = w.shape[-1] // 2
    return jnp.concatenate([-w[..., half:], w[..., :half]], axis=-1)


def _mla_proj(x, mods, layer, g, cos_t, sin_t, w_dq, g_q, w_uq, w_dkv, g_kv, w_ukv):
    bsz, s, d = x.shape
    hh = MLA_HEADS
    q_lora = w_dq.shape[1]
    tm, row_chunk = TILES.mla_proj, TILES.mla_proj_chunk
    wuqt_ext = w_uq.T.astype(BF16)
    wkr = w_dkv[:, KV_LORA:]
    pad = jnp.zeros((d, LANES - QK_ROPE), w_dkv.dtype)
    wdkv_ext = jnp.concatenate([w_dkv[:, :KV_LORA], wkr, pad, _rot_half(wkr), pad],
                               axis=1).astype(BF16)
    wkv3 = w_ukv.reshape(KV_LORA, hh, QK_NOPE + V_DIM)
    wuk = wkv3[:, :, :QK_NOPE].reshape(KV_LORA, hh * QK_NOPE).astype(BF16)
    wuvt = wkv3[:, :, QK_NOPE:].reshape(KV_LORA, hh * V_DIM).T.astype(BF16)
    scale = float((QK_NOPE + QK_ROPE) ** -0.5 * math.log2(math.e))

    const = lambda b, i: (0, 0)
    tok = lambda b, i: (b * (s // tm) + i, 0)
    out_map = lambda b, i: (b, 0, i, 0)
    out_map_t = lambda b, i: (b, 0, 0, i)
    return pl.pallas_call(
        functools.partial(_mla_proj_kernel, scale=scale, row_chunk=row_chunk),
        grid=(bsz, s // tm),
        in_specs=[
            pl.BlockSpec((None, tm, d), lambda b, i: (b, i, 0)),
            pl.BlockSpec((None, None, 1, mods.shape[-1]), lambda b, i: (layer, b, 0, 0)),
            pl.BlockSpec((1, d), const),
            pl.BlockSpec((tm, cos_t.shape[-1]), tok),
            pl.BlockSpec((tm, sin_t.shape[-1]), tok),
            pl.BlockSpec(w_dq.shape, const),
            pl.BlockSpec((1, q_lora), const),
            pl.BlockSpec(wuqt_ext.shape, const),
            pl.BlockSpec(wdkv_ext.shape, const),
            pl.BlockSpec((1, KV_LORA), const),
            pl.BlockSpec(wuk.shape, const),
            pl.BlockSpec(wuvt.shape, const),
        ],
        out_specs=[
            pl.BlockSpec((None, hh, QK_PAD, tm), out_map_t),
            pl.BlockSpec((None, hh, tm, QK_PAD), out_map),
            pl.BlockSpec((None, hh, V_DIM, tm), out_map_t),
        ],
        out_shape=[
            jax.ShapeDtypeStruct((bsz, hh, QK_PAD, s), BF16),
            jax.ShapeDtypeStruct((bsz, hh, s, QK_PAD), BF16),
            jax.ShapeDtypeStruct((bsz, hh, V_DIM, s), BF16),
        ],
        compiler_params=_params("parallel", "parallel"),
        name="mla_proj",
    )(x, mods, g.reshape(1, d), cos_t, sin_t, w_dq.astype(BF16), g_q.reshape(1, q_lora),
      wuqt_ext, wdkv_ext, g_kv.reshape(1, KV_LORA), wuk, wuvt)


def _mla_flash_kernel(qt_ref, qn_ref, k_ref, vt_ref, o_ref, vblk_sc, sn_sc, s0_sc, s1_sc,
                      m_sc, acc_sc, *, tq):
    qi = pl.program_id(2)
    half = tq // 2

    @pl.when(qi == 0)
    def _():
        for t in range(vblk_sc.shape[0]):
            vblk_sc[t, 0:V_DIM, :] = vt_ref[:, t * tq:(t + 1) * tq]
            vblk_sc[t, V_DIM:, :] = jnp.ones((ONES_ROWS, tq), BF16)

    m_sc[...] = jnp.full(m_sc.shape, NEG_INF, F32)
    acc_sc[...] = jnp.zeros(acc_sc.shape, F32)

    def scores(t, s_ref):
        k = k_ref[pl.ds(pl.multiple_of(t * tq, tq), tq), :]
        s_ref[...] = _dot(k, qt_ref[...])

    def prefetch():
        sn_sc[...] = _dot(k_ref[0:tq, :], qn_ref[...])

    def scores_diag(s_ref):
        base = pl.multiple_of(qi * tq, tq)
        s_ref[0:half, :] = _dot(k_ref[pl.ds(base, half), :], qt_ref[...])
        s_ref[half:tq, half:tq] = _dot(k_ref[pl.ds(base + half, half), :], qt_ref[:, half:tq])

    def softmax_pv(s, vt, cols):
        m_prev = m_sc[:, cols]
        m_new = jnp.maximum(m_prev, jnp.max(s, axis=0, keepdims=True))
        alpha = jnp.exp2(m_prev - m_new)
        p = jnp.exp2(s - m_new).astype(BF16)
        acc_sc[:, cols] = alpha * acc_sc[:, cols] + _dot(vt, p)
        m_sc[:, cols] = m_new

    def update(t, s_ref):
        softmax_pv(s_ref[...], vblk_sc[t], slice(0, tq))

    def update_diag(s_ref):
        causal = (lax.broadcasted_iota(jnp.int32, (half, half), 0)
                  <= lax.broadcasted_iota(jnp.int32, (half, half), 1))
        vt = vblk_sc[qi]
        s_top = s_ref[0:half, :]
        s_top = jnp.concatenate([jnp.where(causal, s_top[:, 0:half], NEG_INF), s_top[:, half:tq]],
                                axis=1)
        softmax_pv(s_top, vt[:, 0:half], slice(0, tq))
        s_bot = jnp.where(causal, s_ref[half:tq, half:tq], NEG_INF)
        softmax_pv(s_bot, vt[:, half:tq], slice(half, tq))

    def pair(u, carry):
        t = 2 * u + 1
        scores(t + 1, s0_sc)
        update(t, s1_sc)
        scores(t + 2, s1_sc)
        update(t + 1, s0_sc)
        return carry

    @pl.when(qi == 0)
    def _():
        scores_diag(s0_sc)
        prefetch()
        update_diag(s0_sc)

    @pl.when(qi == 1)
    def _():
        scores_diag(s1_sc)
        update(0, sn_sc)
        prefetch()
        update_diag(s1_sc)

    @pl.when(qi >= 2)
    def _():
        scores(1, s1_sc)
        update(0, sn_sc)

    even = qi % 2 == 0
    n_pairs = jnp.where(qi >= 2, jnp.where(even, (qi - 2) // 2, (qi - 3) // 2), 0)
    lax.fori_loop(0, n_pairs, pair, 0)

    @pl.when(jnp.logical_and(qi >= 2, even))
    def _():
        scores_diag(s0_sc)
        update(qi - 1, s1_sc)
        prefetch()
        update_diag(s0_sc)

    @pl.when(jnp.logical_and(qi >= 2, jnp.logical_not(even)))
    def _():
        scores(qi - 1, s0_sc)
        update(qi - 2, s1_sc)
        scores_diag(s1_sc)
        update(qi - 1, s0_sc)
        prefetch()
        update_diag(s1_sc)

    acc = acc_sc[...]
    o_t = acc[0:V_DIM, :] / acc[V_DIM:V_DIM + 1, :]
    o_ref[...] = o_t.T.astype(o_ref.dtype)


def _mla_flash(qt, k, vt):
    bsz, hh, s, _ = k.shape
    tq = TILES.flash
    last = s // tq - 1
    return pl.pallas_call(
        functools.partial(_mla_flash_kernel, tq=tq),
        grid=(bsz, hh, s // tq),
        in_specs=[
            pl.BlockSpec((None, None, QK_PAD, tq), lambda b, h, i: (b, h, 0, i)),
            pl.BlockSpec((None, None, QK_PAD, tq), lambda b, h, i: (b, h, 0, jnp.minimum(i + 1, last))),
            pl.BlockSpec((None, None, s, QK_PAD), lambda b, h, i: (b, h, 0, 0)),
            pl.BlockSpec((None, None, V_DIM, s), lambda b, h, i: (b, h, 0, 0)),
        ],
        out_specs=pl.BlockSpec((None, tq, V_DIM), lambda b, h, i: (b, i, h)),
        out_shape=jax.ShapeDtypeStruct((bsz, s, hh * V_DIM), BF16),
        scratch_shapes=[pltpu.VMEM((s // tq, V_DIM + ONES_ROWS, tq), BF16),
                        pltpu.VMEM((tq, tq), F32), pltpu.VMEM((tq, tq), F32),
                        pltpu.VMEM((tq, tq), F32),
                        pltpu.VMEM((1, tq), F32), pltpu.VMEM((V_DIM + ONES_ROWS, tq), F32)],
        compiler_params=_params("parallel", "parallel", "arbitrary"),
        name="mla_flash",
    )(qt, qt, k, vt)


def _post_mlp_kernel(x_ref, a_ref, mod_ref, wo_ref, bo_ref, g_ref, w1_ref, w2_ref, gf_ref,
                     out_ref, x1_sc, h_sc, acc_sc, *, final_norm, n_ff, row_chunk):
    j = pl.program_id(2)
    d = x_ref.shape[-1]
    tm = x_ref.shape[0]

    def step(first, last):
        mod = mod_ref[...]
        gt1, sh2, sc2 = mod[:, 2 * d:3 * d], mod[:, 3 * d:4 * d], mod[:, 4 * d:5 * d]
        gt2 = mod[:, 5 * d:6 * d]
        w1 = w1_ref[...].astype(BF16)
        w2 = w2_ref[...].astype(BF16)
        for r in range(tm // row_chunk):
            rows = slice(r * row_chunk, (r + 1) * row_chunk)
            if first:
                y = _dot(a_ref[rows, :], wo_ref[...]) + bo_ref[...]
                x1 = x_ref[rows, :] + gt1 * y
                h = (_rmsnorm(x1, g_ref[...]) * (1.0 + sc2) + sh2).astype(BF16)
                h_sc[rows, :] = h
                if not last:
                    x1_sc[rows, :] = x1
            else:
                h = h_sc[rows, :]
            u = jnp.square(jnp.maximum(_dot(h, w1), 0.0)).astype(BF16)
            acc = _dot(u, w2)
            if not first:
                acc = acc_sc[rows, :] + acc
            if last:
                x2 = (x1 if first else x1_sc[rows, :]) + gt2 * acc
                if final_norm:
                    x2 = _rmsnorm(x2, gf_ref[...])
                out_ref[rows, :] = x2
            else:
                acc_sc[rows, :] = acc

    pl.when(j == 0)(functools.partial(step, True, n_ff == 1))
    if n_ff > 1:
        pl.when(j == n_ff - 1)(functools.partial(step, False, True))
    if n_ff > 2:
        pl.when(jnp.logical_and(j > 0, j < n_ff - 1))(functools.partial(step, False, False))


def _post_mlp(x, attn, mods, layer, w_o, b_o, g_mlp, w_ff1, w_ff2, g_final, final_norm):
    bsz, s, d = x.shape
    dff = w_ff1.shape[-1]
    tm, tf, row_chunk = TILES.mlp, TILES.mlp_ff, TILES.mlp_chunk
    const = lambda b, i, j: (0, 0)
    tok = lambda b, i, j: (b, i, 0)
    return pl.pallas_call(
        functools.partial(_post_mlp_kernel, final_norm=final_norm, n_ff=dff // tf,
                          row_chunk=row_chunk),
        grid=(bsz, s // tm, dff // tf),
        in_specs=[
            pl.BlockSpec((None, tm, d), tok),
            pl.BlockSpec((None, tm, attn.shape[-1]), tok),
            pl.BlockSpec((None, None, 1, mods.shape[-1]), lambda b, i, j: (layer, b, 0, 0)),
            pl.BlockSpec(w_o.shape, const),
            pl.BlockSpec((1, d), const),
            pl.BlockSpec((1, d), const),
            pl.BlockSpec((None, d, tf), lambda b, i, j: (layer, 0, j)),
            pl.BlockSpec((None, tf, d), lambda b, i, j: (layer, j, 0)),
            pl.BlockSpec((1, d), const),
        ],
        out_specs=pl.BlockSpec((None, tm, d), tok),
        out_shape=jax.ShapeDtypeStruct((bsz, s, d), F32),
        scratch_shapes=[pltpu.VMEM((tm, d), F32), pltpu.VMEM((tm, d), BF16),
                        pltpu.VMEM((tm, d), F32)],
        compiler_params=_params("parallel", "parallel", "arbitrary"),
        name="post_mlp_final" if final_norm else "post_mlp",
    )(x, attn, mods, w_o.astype(BF16), b_o.reshape(1, d), g_mlp.reshape(1, d),
      w_ff1, w_ff2, g_final.reshape(1, d))


def _swa_proj_kernel(x_ref, mod_ref, g_ref, w_ref, b_ref, o_ref, *, scale):
    d = x_ref.shape[-1]
    mod = mod_ref[...]
    sh1, sc1 = mod[:, 0:d], mod[:, d:2 * d]
    h = (_rmsnorm(x_ref[...], g_ref[...]) * (1.0 + sc1) + sh1).astype(BF16)
    qkv = _dot(h, w_ref[...]) + b_ref[...]
    nq = SWA_HEADS * SWA_HEAD_DIM
    o_ref[:, 0:nq] = (qkv[:, 0:nq] * scale).astype(BF16)
    o_ref[:, nq:] = qkv[:, nq:].astype(BF16)


def _swa_window_order(t, axis):
    group = SWA_HEADS // SWA_KV_HEADS
    shape = t.shape
    t = t.reshape(shape[:axis] + (SWA_KV_HEADS, group, SWA_HEAD_DIM) + shape[axis + 1:])
    return jnp.swapaxes(t, axis, axis + 1).reshape(shape)


def _swa_proj(x, mods, layer, g, w_qkv, b_qkv):
    bsz, s, d = x.shape
    n = w_qkv.shape[1]
    tm = TILES.swa_proj
    const = lambda b, i: (0, 0)
    return pl.pallas_call(
        functools.partial(_swa_proj_kernel,
                          scale=float(SWA_HEAD_DIM ** -0.5 * math.log2(math.e))),
        grid=(bsz, s // tm),
        in_specs=[
            pl.BlockSpec((None, tm, d), lambda b, i: (b, i, 0)),
            pl.BlockSpec((None, None, 1, mods.shape[-1]), lambda b, i: (layer, b, 0, 0)),
            pl.BlockSpec((1, d), const),
            pl.BlockSpec(w_qkv.shape, const),
            pl.BlockSpec((1, n), const),
        ],
        out_specs=pl.BlockSpec((None, tm, n), lambda b, i: (b, i, 0)),
        out_shape=jax.ShapeDtypeStruct((bsz, s, n), BF16),
        compiler_params=_params("parallel", "parallel"),
        name="swa_proj",
    )(x, mods, g.reshape(1, d), w_qkv.astype(BF16), b_qkv.reshape(1, n))


def _swa_attn_kernel(sink_ref, q_ref, kc_ref, vc_ref, kp_ref, vp_ref, o_ref, bias_sc, *, slopes):
    first_step = jnp.logical_and(pl.program_id(0) == 0, pl.program_id(1) == 0)
    w = WINDOW
    dh = SWA_HEAD_DIM
    n_kv = SWA_KV_HEADS
    group = SWA_HEADS // n_kv
    gw = n_kv * dh
    log2e = math.log2(math.e)

    @pl.when(first_step)
    def _():
        qpos = lax.broadcasted_iota(jnp.int32, (w, 2 * w), 0)
        kpos = lax.broadcasted_iota(jnp.int32, (w, 2 * w), 1)
        dist = w + qpos - kpos
        in_window = jnp.logical_and(dist >= 0, dist < w)
        dist_f = dist.astype(F32)
        for hd in range(SWA_HEADS):
            bias_sc[hd * w:(hd + 1) * w, :] = jnp.where(
                in_window, dist_f * (-slopes[hd] * log2e), NEG_INF)

    no_prev = jnp.where(pl.program_id(1) == 0, NEG_INF, 0.0)
    ones_cols = jnp.ones((2 * w, w), BF16)
    lane_group = lax.broadcasted_iota(jnp.int32, (w, n_kv * dh), 1) // dh
    low_half = lax.broadcasted_iota(jnp.int32, (w, w), 1) < dh
    half_rows = (n_kv // 2) * group * w
    sink = jnp.concatenate([jnp.full((w, w), sink_ref[hd] * log2e, F32)
                            for hd in range(SWA_HEADS)], axis=0)

    def band(cur_ref, prev_ref, n):
        if n == 0:
            return jnp.concatenate([prev_ref[...], cur_ref[0:w, :]], axis=0)
        return cur_ref[(n - 1) * w:(n + 1) * w, :]

    def stage_scores(n):
        lhs = []
        for g in range(n_kv):
            for a in range(group):
                qwin = q_ref[n * w:(n + 1) * w, a * gw:(a + 1) * gw]
                lhs.append(jnp.where(lane_group == g, qwin, jnp.zeros_like(qwin)))
        s = _dot_nt(jnp.concatenate(lhs, axis=0), band(kc_ref, kp_ref, n)) + bias_sc[...]
        if n == 0:
            s = jnp.concatenate([s[:, 0:w] + no_prev, s[:, w:2 * w]], axis=1)
        return s

    def stage_softmax(s):
        m = jnp.maximum(jnp.max(s, axis=-1, keepdims=True), sink)
        return jnp.exp2(s - jnp.tile(m, (1, 2))).astype(BF16), jnp.exp2(sink - m)

    def stage_out(n, p, e):
        vall = band(vc_ref, vp_ref, n)
        pv_a = _dot(p[0:half_rows], jnp.concatenate([vall[:, 0:w], ones_cols], axis=1))
        pv_b = _dot(p[half_rows:], jnp.concatenate([ones_cols, vall[:, w:2 * w]], axis=1))
        o_a = pv_a[:, 0:w] / (pv_a[:, w:2 * w] + e[0:half_rows])
        o_b = pv_b[:, w:2 * w] / (pv_b[:, 0:w] + e[half_rows:])
        for a in range(group):
            lo = jnp.where(low_half, o_a[a * w:(a + 1) * w], o_a[(group + a) * w:(group + a + 1) * w])
            hi = jnp.where(low_half, o_b[a * w:(a + 1) * w], o_b[(group + a) * w:(group + a + 1) * w])
            o_ref[n * w:(n + 1) * w, a * gw:(a + 1) * gw] = jnp.concatenate(
                [lo, hi], axis=1).astype(o_ref.dtype)

    n_blocks = q_ref.shape[0] // w
    scores, probs = {}, {}
    for step in range(n_blocks + 2):
        if step < n_blocks:
            scores[step] = stage_scores(step)
        if 1 <= step <= n_blocks:
            probs[step - 1] = stage_softmax(scores.pop(step - 1))
        if step >= 2:
            stage_out(step - 2, *probs.pop(step - 2))


def _swa_attn(qkv, sinks):
    bsz, s, _ = qkv.shape
    nq = SWA_HEADS * SWA_HEAD_DIM
    nkv = SWA_KV_HEADS * SWA_HEAD_DIM
    tm = TILES.swa_attn
    per = tm // WINDOW
    slopes = tuple(float(v) for v in 2.0 ** (-8.0 * np.arange(1, SWA_HEADS + 1) / SWA_HEADS))
    kcol, vcol = nq // nkv, nq // nkv + 1
    prev = lambda b, i: jnp.maximum(i * per - 1, 0)
    return pl.pallas_call(
        functools.partial(_swa_attn_kernel, slopes=slopes),
        grid=(bsz, s // tm),
        in_specs=[
            pl.BlockSpec(memory_space=pltpu.SMEM),
            pl.BlockSpec((None, tm, nq), lambda b, i: (b, i, 0)),
            pl.BlockSpec((None, tm, nkv), lambda b, i: (b, i, kcol)),
            pl.BlockSpec((None, tm, nkv), lambda b, i: (b, i, vcol)),
            pl.BlockSpec((None, WINDOW, nkv), lambda b, i: (b, prev(b, i), kcol)),
            pl.BlockSpec((None, WINDOW, nkv), lambda b, i: (b, prev(b, i), vcol)),
        ],
        out_specs=pl.BlockSpec((None, tm, nq), lambda b, i: (b, i, 0)),
        out_shape=jax.ShapeDtypeStruct((bsz, s, nq), BF16),
        scratch_shapes=[pltpu.VMEM((SWA_HEADS * WINDOW, 2 * WINDOW), F32)],
        compiler_params=_params("arbitrary", "arbitrary"),
        name="swa_attn",
    )(sinks.astype(F32), qkv, qkv, qkv, qkv, qkv)


def kernel(x, c, positions, w_ada, b_ada, g_mix, g_mlp, mla_w_dq, mla_g_q, mla_w_uq, mla_w_dkv,
           mla_g_kv, mla_w_ukv, mla_w_o, swa_w_qkv, swa_b_qkv, swa_sinks, swa_w_o, swa_b_o,
           w_ff1, w_ff2, g_final):
    depth = w_ada.shape[0]
    bsz, s, d = x.shape
    mods = _adaln(c, w_ada, b_ada).reshape(depth, bsz, 1, w_ada.shape[-1])
    cos_t, sin_t = _rope_tables(positions)
    zero_bias = jnp.zeros((d,), F32)
    for i in range(depth):
        j = i // 2
        last = i == depth - 1
        if i % 2 == 0:
            q, k, v = _mla_proj(x, mods, i, g_mix[i], cos_t, sin_t, mla_w_dq[j], mla_g_q[j],
                                mla_w_uq[j], mla_w_dkv[j], mla_g_kv[j], mla_w_ukv[j])
            attn = _mla_flash(q, k, v)
            w_o, b_o = mla_w_o[j], zero_bias
        else:
            nq = SWA_HEADS * SWA_HEAD_DIM
            w_qkv = swa_w_qkv[j].astype(BF16)
            w_qkv = jnp.concatenate([_swa_window_order(w_qkv[:, :nq], 1), w_qkv[:, nq:]], axis=1)
            b_qkv = jnp.concatenate([_swa_window_order(swa_b_qkv[j][:nq], 0), swa_b_qkv[j][nq:]])
            qkv = _swa_proj(x, mods, i, g_mix[i], w_qkv, b_qkv)
            attn = _swa_attn(qkv, swa_sinks[j])
            w_o, b_o = _swa_window_order(swa_w_o[j].astype(BF16), 0), swa_b_o[j]
        x = _post_mlp(x, attn, mods, i, w_o, b_o, g_mlp[i], w_ff1, w_ff2, g_final, last)
    return x
```

```python
import functools
import math
from typing import NamedTuple

import numpy as np
import jax
import jax.numpy as jnp
from jax import lax
from jax.experimental import pallas as pl
from jax.experimental.pallas import tpu as pltpu

F32 = jnp.float32
BF16 = jnp.bfloat16

EPS = 1e-6
ROPE_THETA = 10000.0
MLA_HEADS = 8
QK_NOPE = 128
QK_ROPE = 64
V_DIM = 128
KV_LORA = 256
SWA_HEADS = 16
SWA_KV_HEADS = 4
SWA_HEAD_DIM = 64
WINDOW = 128

LANES = 128
QK_PAD = 2 * LANES
ONES_ROWS = 16
VMEM_LIMIT = 56 * 1024 * 1024

NEG_INF = float("-inf")


class _Tiles(NamedTuple):
    adaln_cols: int = 1024
    rope_tokens: int = 2048
    mla_proj: int = 1024
    mla_proj_chunk: int = 1024
    mlp: int = 1024
    mlp_ff: int = 1024
    mlp_chunk: int = 1024
    swa_proj: int = 1024
    swa_attn: int = 2048


TILES = _Tiles()


def _params(*sem):
    return pltpu.CompilerParams(dimension_semantics=sem, vmem_limit_bytes=VMEM_LIMIT)


def _rmsnorm(x, g):
    y = x * lax.rsqrt(jnp.mean(x * x, axis=-1, keepdims=True) + EPS)
    return y * g


def _dot(a, b):
    return jnp.dot(a, b, preferred_element_type=F32)


def _dot_nt(a, b):
    return lax.dot_general(a, b, (((1,), (1,)), ((), ())), preferred_element_type=F32)


def _adaln_kernel(ct_ref, w_ref, b_ref, o_ref):
    ct = ct_ref[...]
    cond = ct * jax.nn.sigmoid(ct)
    w = w_ref[...]
    rows = [jnp.sum(w * cond[:, b:b + 1], axis=0, keepdims=True)
            for b in range(ct.shape[1])]
    o_ref[...] = jnp.concatenate(rows, axis=0) + b_ref[...]


def _adaln(c, w_ada, b_ada):
    depth, d, n = w_ada.shape
    bsz = c.shape[0]
    tn = TILES.adaln_cols
    return pl.pallas_call(
        _adaln_kernel,
        grid=(depth, n // tn),
        in_specs=[
            pl.BlockSpec((d, bsz), lambda l, j: (0, 0)),
            pl.BlockSpec((None, d, tn), lambda l, j: (l, 0, j)),
            pl.BlockSpec((None, 1, tn), lambda l, j: (l, 0, j)),
        ],
        out_specs=pl.BlockSpec((None, bsz, tn), lambda l, j: (l, 0, j)),
        out_shape=jax.ShapeDtypeStruct((depth, bsz, n), F32),
        compiler_params=_params("parallel", "parallel"),
        name="adaln",
    )(c.T, w_ada, b_ada.reshape(depth, 1, n))


def _rope_table_kernel(pos_ref, inv_ref, cos_ref, sin_ref):
    ang = inv_ref[...] * pos_ref[...].astype(F32)
    cos_ref[...] = jnp.cos(ang)
    sin_ref[...] = jnp.sin(ang)


def _rope_tables(positions):
    n = positions.size
    half = QK_ROPE // 2
    inv_freq = ROPE_THETA ** (-jnp.arange(half, dtype=F32) / half)
    tm = TILES.rope_tokens
    return pl.pallas_call(
        _rope_table_kernel,
        grid=(n // tm,),
        in_specs=[pl.BlockSpec((1, tm), lambda i: (0, i)),
                  pl.BlockSpec((half, 1), lambda i: (0, 0))],
        out_specs=[pl.BlockSpec((half, tm), lambda i: (0, i)),
                   pl.BlockSpec((half, tm), lambda i: (0, i))],
        out_shape=[jax.ShapeDtypeStruct((half, n), F32)] * 2,
        compiler_params=_params("parallel"),
        name="rope_table",
    )(positions.reshape(1, n), inv_freq.reshape(half, 1))


def _mla_proj_kernel(x_ref, mod_ref, g_ref, cos_ref, sin_ref, wdq_ref, gq_ref, wuqt_ref,
                     wdkv_ref, gkv_ref, wuk_ref, wuvt_ref, qt_ref, k_ref, vt_ref, *, scale,
                     row_chunk):
    d = x_ref.shape[-1]
    mod = mod_ref[...]
    sh1, sc1 = mod[:, 0:d], mod[:, d:2 * d]
    reps = LANES // cos_ref.shape[0]
    n_chunks = x_ref.shape[0] // row_chunk
    zero_rows = jnp.zeros((QK_PAD - QK_NOPE - QK_ROPE, row_chunk), BF16)

    def down(c):
        rows = slice(c * row_chunk, (c + 1) * row_chunk)
        h = (_rmsnorm(x_ref[rows, :], g_ref[...]) * (1.0 + sc1) + sh1).astype(BF16)
        return _dot(h, wdq_ref[...]), _dot(h, wdkv_ref[...])

    def up(cq_raw, dkv):
        cq = _rmsnorm(cq_raw, gq_ref[...]).astype(BF16)
        ckv = _rmsnorm(dkv[:, 0:KV_LORA], gkv_ref[...]).astype(BF16)
        return (_dot_nt(wuqt_ref[...], cq), _dot(ckv, wuk_ref[...]),
                _dot_nt(wuvt_ref[...], ckv))

    def emit(c, dkv, qt, kn, vt):
        rows = slice(c * row_chunk, (c + 1) * row_chunk)
        cos_rep = jnp.tile(cos_ref[:, rows], (reps, 1))
        sin_rep = jnp.tile(sin_ref[:, rows], (reps, 1))
        cos_t, sin_t = cos_rep[0:QK_ROPE, :], sin_rep[0:QK_ROPE, :]
        cos, sin = cos_rep.T, sin_rep.T
        for hd in range(MLA_HEADS):
            base = hd * (QK_NOPE + QK_ROPE)
            qt_ref[hd, 0:QK_NOPE, rows] = (qt[base:base + QK_NOPE, :] * scale).astype(BF16)
            a = qt[base + QK_NOPE:base + QK_NOPE + QK_ROPE, :]
            b = jnp.concatenate([-a[QK_ROPE // 2:, :], a[0:QK_ROPE // 2, :]], axis=0)
            qt_ref[hd, QK_NOPE:QK_NOPE + QK_ROPE, rows] = (
                (a * cos_t + b * sin_t) * scale).astype(BF16)
            qt_ref[hd, QK_NOPE + QK_ROPE:QK_PAD, rows] = zero_rows
        kr = (dkv[:, KV_LORA:KV_LORA + LANES] * cos
              + dkv[:, KV_LORA + LANES:KV_LORA + 2 * LANES] * sin).astype(BF16)
        for hd in range(MLA_HEADS):
            k_ref[hd, rows, 0:LANES] = kn[:, hd * QK_NOPE:(hd + 1) * QK_NOPE].astype(BF16)
            k_ref[hd, rows, LANES:QK_PAD] = kr
            vt_ref[hd, :, rows] = vt[hd * V_DIM:(hd + 1) * V_DIM, :].astype(BF16)

    downs, ups = {}, {}
    for step in range(n_chunks + 2):
        if step < n_chunks:
            downs[step] = down(step)
        if 1 <= step <= n_chunks:
            ups[step - 1] = up(*downs[step - 1])
        if step >= 2:
            emit(step - 2, downs.pop(step - 2)[1], *ups.pop(step - 2))


def _rot_half(w):
    half = w.shape[-1] // 2
    return jnp.concatenate([-w[..., half:], w[..., :half]], axis=-1)


def _mla_proj(x, mods, layer, g, cos_t, sin_t, w_dq, g_q, w_uq, w_dkv, g_kv, w_ukv):
    bsz, s, d = x.shape
    hh = MLA_HEADS
    q_lora = w_dq.shape[1]
    tm, row_chunk = TILES.mla_proj, TILES.mla_proj_chunk
    wuqt_ext = w_uq.astype(BF16).T
    wkr = w_dkv[:, KV_LORA:]
    pad = jnp.zeros((d, LANES - QK_ROPE), w_dkv.dtype)
    wdkv_ext = jnp.concatenate([w_dkv[:, :KV_LORA], wkr, pad, _rot_half(wkr), pad],
                               axis=1).astype(BF16)
    wkv3 = w_ukv.astype(BF16).reshape(KV_LORA, hh, QK_NOPE + V_DIM)
    wuk = wkv3[:, :, :QK_NOPE].reshape(KV_LORA, hh * QK_NOPE)
    wuvt = wkv3[:, :, QK_NOPE:].reshape(KV_LORA, hh * V_DIM).T
    scale = float((QK_NOPE + QK_ROPE) ** -0.5 * math.log2(math.e))

    const = lambda b, i: (0, 0)
    tok = lambda b, i: (0, b * (s // tm) + i)
    out_map = lambda b, i: (b, 0, i, 0)
    out_map_t = lambda b, i: (b, 0, 0, i)
    return pl.pallas_call(
        functools.partial(_mla_proj_kernel, scale=scale, row_chunk=row_chunk),
        grid=(bsz, s // tm),
        in_specs=[
            pl.BlockSpec((None, tm, d), lambda b, i: (b, i, 0)),
            pl.BlockSpec((None, None, 1, mods.shape[-1]), lambda b, i: (layer, b, 0, 0)),
            pl.BlockSpec((1, d), const),
            pl.BlockSpec((cos_t.shape[0], tm), tok),
            pl.BlockSpec((sin_t.shape[0], tm), tok),
            pl.BlockSpec(w_dq.shape, const),
            pl.BlockSpec((1, q_lora), const),
            pl.BlockSpec(wuqt_ext.shape, const),
            pl.BlockSpec(wdkv_ext.shape, const),
            pl.BlockSpec((1, KV_LORA), const),
            pl.BlockSpec(wuk.shape, const),
            pl.BlockSpec(wuvt.shape, const),
        ],
        out_specs=[
            pl.BlockSpec((None, hh, None, QK_PAD, tm), lambda b, i: (b, 0, i, 0, 0)),
            pl.BlockSpec((None, hh, tm, QK_PAD), out_map),
            pl.BlockSpec((None, hh, V_DIM, tm), out_map_t),
        ],
        out_shape=[
            jax.ShapeDtypeStruct((bsz, hh, s // tm, QK_PAD, tm), BF16),
            jax.ShapeDtypeStruct((bsz, hh, s, QK_PAD), BF16),
            jax.ShapeDtypeStruct((bsz, hh, V_DIM, s), BF16),
        ],
        compiler_params=_params("parallel", "parallel"),
        name="mla_proj",
    )(x, mods, g.reshape(1, d), cos_t, sin_t, w_dq.astype(BF16), g_q.reshape(1, q_lora),
      wuqt_ext, wdkv_ext, g_kv.reshape(1, KV_LORA), wuk, wuvt)


def _mla_flash_kernel(qt_ref, k_ref, vt_ref, o_ref, vblk_sc, sn_sc, s0_sc, s1_sc, m_sc, acc_sc,
                      *, tq):
    half = tq // 2
    n_tiles = qt_ref.shape[0]

    for t in range(n_tiles):
        vblk_sc[t, 0:V_DIM, :] = vt_ref[:, t * tq:(t + 1) * tq]
        vblk_sc[t, V_DIM:, :] = jnp.ones((ONES_ROWS, tq), BF16)

    def tile(qi, carry):
        m_sc[...] = jnp.full(m_sc.shape, NEG_INF, F32)
        acc_sc[...] = jnp.zeros(acc_sc.shape, F32)

        def scores(t, s_ref):
            k = k_ref[pl.ds(pl.multiple_of(t * tq, tq), tq), :]
            s_ref[...] = _dot(k, qt_ref[qi])

        def prefetch():
            sn_sc[...] = _dot(k_ref[0:tq, :], qt_ref[jnp.minimum(qi + 1, n_tiles - 1)])

        def scores_diag(s_ref):
            base = pl.multiple_of(qi * tq, tq)
            s_ref[0:half, :] = _dot(k_ref[pl.ds(base, half), :], qt_ref[qi])
            s_ref[half:tq, half:tq] = _dot(k_ref[pl.ds(base + half, half), :],
                                           qt_ref[qi, :, half:tq])

        def softmax_pv(s, vt, cols):
            m_prev = m_sc[:, cols]
            m_new = jnp.maximum(m_prev, jnp.max(s, axis=0, keepdims=True))
            alpha = jnp.exp2(m_prev - m_new)
            p = jnp.exp2(s - m_new).astype(BF16)
            acc_sc[:, cols] = alpha * acc_sc[:, cols] + _dot(vt, p)
            m_sc[:, cols] = m_new

        def update(t, s_ref):
            softmax_pv(s_ref[...], vblk_sc[t], slice(0, tq))

        def update_diag(s_ref):
            causal = (lax.broadcasted_iota(jnp.int32, (half, half), 0)
                      <= lax.broadcasted_iota(jnp.int32, (half, half), 1))
            vt = vblk_sc[qi]
            s_top = s_ref[0:half, :]
            s_top = jnp.concatenate(
                [jnp.where(causal, s_top[:, 0:half], NEG_INF), s_top[:, half:tq]], axis=1)
            softmax_pv(s_top, vt[:, 0:half], slice(0, tq))
            s_bot = jnp.where(causal, s_ref[half:tq, half:tq], NEG_INF)
            softmax_pv(s_bot, vt[:, half:tq], slice(half, tq))

        def pair(u, c):
            t = 2 * u + 1
            scores(t + 1, s0_sc)
            update(t, s1_sc)
            scores(t + 2, s1_sc)
            update(t + 1, s0_sc)
            return c

        @pl.when(qi == 0)
        def _():
            scores_diag(s0_sc)
            prefetch()
            update_diag(s0_sc)

        @pl.when(qi == 1)
        def _():
            scores_diag(s1_sc)
            update(0, sn_sc)
            prefetch()
            update_diag(s1_sc)

        @pl.when(qi >= 2)
        def _():
            scores(1, s1_sc)
            update(0, sn_sc)

        even = qi % 2 == 0
        n_pairs = jnp.where(qi >= 2, jnp.where(even, (qi - 2) // 2, (qi - 3) // 2), 0)
        lax.fori_loop(0, n_pairs, pair, 0)

        @pl.when(jnp.logical_and(qi >= 2, even))
        def _():
            scores_diag(s0_sc)
            update(qi - 1, s1_sc)
            prefetch()
            update_diag(s0_sc)

        @pl.when(jnp.logical_and(qi >= 2, jnp.logical_not(even)))
        def _():
            scores(qi - 1, s0_sc)
            update(qi - 2, s1_sc)
            scores_diag(s1_sc)
            update(qi - 1, s0_sc)
            prefetch()
            update_diag(s1_sc)

        acc = acc_sc[...]
        o_t = acc[0:V_DIM, :] / acc[V_DIM:V_DIM + 1, :]
        o_ref[pl.ds(pl.multiple_of(qi * tq, tq), tq), :] = o_t.T.astype(o_ref.dtype)
        return carry

    lax.fori_loop(0, n_tiles, tile, 0)


def _mla_flash(qt, k, vt):
    bsz, hh, s, _ = k.shape
    tq = qt.shape[-1]
    return pl.pallas_call(
        functools.partial(_mla_flash_kernel, tq=tq),
        grid=(bsz, hh),
        in_specs=[
            pl.BlockSpec((None, None) + qt.shape[2:], lambda b, h: (b, h, 0, 0, 0)),
            pl.BlockSpec((None, None, s, QK_PAD), lambda b, h: (b, h, 0, 0)),
            pl.BlockSpec((None, None, V_DIM, s), lambda b, h: (b, h, 0, 0)),
        ],
        out_specs=pl.BlockSpec((None, s, V_DIM), lambda b, h: (b, 0, h)),
        out_shape=jax.ShapeDtypeStruct((bsz, s, hh * V_DIM), BF16),
        scratch_shapes=[pltpu.VMEM((s // tq, V_DIM + ONES_ROWS, tq), BF16),
                        pltpu.VMEM((tq, tq), F32), pltpu.VMEM((tq, tq), F32),
                        pltpu.VMEM((tq, tq), F32),
                        pltpu.VMEM((1, tq), F32), pltpu.VMEM((V_DIM + ONES_ROWS, tq), F32)],
        compiler_params=_params("parallel", "parallel"),
        name="mla_flash",
    )(qt, k, vt)


def _post_mlp_kernel(x_ref, a_ref, mod_ref, wo_ref, bo_ref, g_ref, w1_ref, w2_ref, gf_ref,
                     out_ref, x1_sc, h_sc, acc_sc, *, final_norm, n_ff, row_chunk):
    j = pl.program_id(2)
    d = x_ref.shape[-1]
    tm = x_ref.shape[0]

    def step(first, last):
        mod = mod_ref[...]
        gt1, sh2, sc2 = mod[:, 2 * d:3 * d], mod[:, 3 * d:4 * d], mod[:, 4 * d:5 * d]
        gt2 = mod[:, 5 * d:6 * d]
        w1 = w1_ref[...].astype(BF16)
        w2 = w2_ref[...].astype(BF16)
        for r in range(tm // row_chunk):
            rows = slice(r * row_chunk, (r + 1) * row_chunk)
            if first:
                y = _dot(a_ref[rows, :], wo_ref[...]) + bo_ref[...]
                x1 = x_ref[rows, :] + gt1 * y
                h = (_rmsnorm(x1, g_ref[...]) * (1.0 + sc2) + sh2).astype(BF16)
                h_sc[rows, :] = h
                if not last:
                    x1_sc[rows, :] = x1
            else:
                h = h_sc[rows, :]
            u = jnp.square(jnp.maximum(_dot(h, w1), 0.0)).astype(BF16)
            acc = _dot(u, w2)
            if not first:
                acc = acc_sc[rows, :] + acc
            if last:
                x2 = (x1 if first else x1_sc[rows, :]) + gt2 * acc
                if final_norm:
                    x2 = _rmsnorm(x2, gf_ref[...])
                out_ref[rows, :] = x2
            else:
                acc_sc[rows, :] = acc

    pl.when(j == 0)(functools.partial(step, True, n_ff == 1))
    if n_ff > 1:
        pl.when(j == n_ff - 1)(functools.partial(step, False, True))
    if n_ff > 2:
        pl.when(jnp.logical_and(j > 0, j < n_ff - 1))(functools.partial(step, False, False))


def _post_mlp(x, attn, mods, layer, w_o, b_o, g_mlp, w_ff1, w_ff2, g_final, final_norm):
    bsz, s, d = x.shape
    dff = w_ff1.shape[-1]
    tm, tf, row_chunk = TILES.mlp, TILES.mlp_ff, TILES.mlp_chunk
    const = lambda b, i, j: (0, 0)
    tok = lambda b, i, j: (b, i, 0)
    return pl.pallas_call(
        functools.partial(_post_mlp_kernel, final_norm=final_norm, n_ff=dff // tf,
                          row_chunk=row_chunk),
        grid=(bsz, s // tm, dff // tf),
        in_specs=[
            pl.BlockSpec((None, tm, d), tok),
            pl.BlockSpec((None, tm, attn.shape[-1]), tok),
            pl.BlockSpec((None, None, 1, mods.shape[-1]), lambda b, i, j: (layer, b, 0, 0)),
            pl.BlockSpec(w_o.shape, const),
            pl.BlockSpec((1, d), const),
            pl.BlockSpec((1, d), const),
            pl.BlockSpec((None, d, tf), lambda b, i, j: (layer, 0, j)),
            pl.BlockSpec((None, tf, d), lambda b, i, j: (layer, j, 0)),
            pl.BlockSpec((1, d), const),
        ],
        out_specs=pl.BlockSpec((None, tm, d), tok),
        out_shape=jax.ShapeDtypeStruct((bsz, s, d), F32),
        scratch_shapes=[pltpu.VMEM((tm, d), F32), pltpu.VMEM((tm, d), BF16),
                        pltpu.VMEM((tm, d), F32)],
        compiler_params=_params("parallel", "parallel", "arbitrary"),
        name="post_mlp_final" if final_norm else "post_mlp",
    )(x, attn, mods, w_o.astype(BF16), b_o.reshape(1, d), g_mlp.reshape(1, d),
      w_ff1, w_ff2, g_final.reshape(1, d))


def _swa_proj_kernel(x_ref, mod_ref, g_ref, w_ref, b_ref, o_ref, *, scale):
    d = x_ref.shape[-1]
    mod = mod_ref[...]
    sh1, sc1 = mod[:, 0:d], mod[:, d:2 * d]
    h = (_rmsnorm(x_ref[...], g_ref[...]) * (1.0 + sc1) + sh1).astype(BF16)
    qkv = _dot(h, w_ref[...]) + b_ref[...]
    nq = SWA_HEADS * SWA_HEAD_DIM
    o_ref[:, 0:nq] = (qkv[:, 0:nq] * scale).astype(BF16)
    o_ref[:, nq:] = qkv[:, nq:].astype(BF16)


def _swa_window_order(t, axis):
    group = SWA_HEADS // SWA_KV_HEADS
    shape = t.shape
    t = t.reshape(shape[:axis] + (SWA_KV_HEADS, group, SWA_HEAD_DIM) + shape[axis + 1:])
    return jnp.swapaxes(t, axis, axis + 1).reshape(shape)


def _swa_proj(x, mods, layer, g, w_qkv, b_qkv):
    bsz, s, d = x.shape
    n = w_qkv.shape[1]
    tm = TILES.swa_proj
    const = lambda b, i: (0, 0)
    return pl.pallas_call(
        functools.partial(_swa_proj_kernel,
                          scale=float(SWA_HEAD_DIM ** -0.5 * math.log2(math.e))),
        grid=(bsz, s // tm),
        in_specs=[
            pl.BlockSpec((None, tm, d), lambda b, i: (b, i, 0)),
            pl.BlockSpec((None, None, 1, mods.shape[-1]), lambda b, i: (layer, b, 0, 0)),
            pl.BlockSpec((1, d), const),
            pl.BlockSpec(w_qkv.shape, const),
            pl.BlockSpec((1, n), const),
        ],
        out_specs=pl.BlockSpec((None, tm, n), lambda b, i: (b, i, 0)),
        out_shape=jax.ShapeDtypeStruct((bsz, s, n), BF16),
        compiler_params=_params("parallel", "parallel"),
        name="swa_proj",
    )(x, mods, g.reshape(1, d), w_qkv.astype(BF16), b_qkv.reshape(1, n))


def _swa_attn_kernel(sink_ref, q_ref, kc_ref, vc_ref, kp_ref, vp_ref, o_ref, bias_sc, *, slopes):
    first_step = jnp.logical_and(pl.program_id(0) == 0, pl.program_id(1) == 0)
    w = WINDOW
    dh = SWA_HEAD_DIM
    n_kv = SWA_KV_HEADS
    group = SWA_HEADS // n_kv
    gw = n_kv * dh
    log2e = math.log2(math.e)

    @pl.when(first_step)
    def _():
        qpos = lax.broadcasted_iota(jnp.int32, (w, 2 * w), 0)
        kpos = lax.broadcasted_iota(jnp.int32, (w, 2 * w), 1)
        dist = w + qpos - kpos
        in_window = jnp.logical_and(dist >= 0, dist < w)
        dist_f = dist.astype(F32)
        for hd in range(SWA_HEADS):
            bias_sc[hd * w:(hd + 1) * w, :] = jnp.where(
                in_window, dist_f * (-slopes[hd] * log2e), NEG_INF)

    no_prev = jnp.where(pl.program_id(1) == 0, NEG_INF, 0.0)
    ones_cols = jnp.ones((2 * w, w), BF16)
    lane_group = lax.broadcasted_iota(jnp.int32, (w, n_kv * dh), 1) // dh
    low_half = lax.broadcasted_iota(jnp.int32, (w, w), 1) < dh
    half_rows = (n_kv // 2) * group * w
    sink = jnp.concatenate([jnp.full((w, w), sink_ref[hd] * log2e, F32)
                            for hd in range(SWA_HEADS)], axis=0)

    def band(cur_ref, prev_ref, n):
        if n == 0:
            return jnp.concatenate([prev_ref[...], cur_ref[0:w, :]], axis=0)
        return cur_ref[(n - 1) * w:(n + 1) * w, :]

    def stage_scores(n):
        lhs = []
        for g in range(n_kv):
            for a in range(group):
                qwin = q_ref[n * w:(n + 1) * w, a * gw:(a + 1) * gw]
                lhs.append(jnp.where(lane_group == g, qwin, jnp.zeros_like(qwin)))
        s = _dot_nt(jnp.concatenate(lhs, axis=0), band(kc_ref, kp_ref, n)) + bias_sc[...]
        if n == 0:
            s = jnp.concatenate([s[:, 0:w] + no_prev, s[:, w:2 * w]], axis=1)
        return s

    def stage_softmax(s):
        m = jnp.maximum(jnp.max(s, axis=-1, keepdims=True), sink)
        return jnp.exp2(s - jnp.tile(m, (1, 2))).astype(BF16), jnp.exp2(sink - m)

    def stage_out(n, p, e):
        vall = band(vc_ref, vp_ref, n)
        pv_a = _dot(p[0:half_rows], jnp.concatenate([vall[:, 0:w], ones_cols], axis=1))
        pv_b = _dot(p[half_rows:], jnp.concatenate([ones_cols, vall[:, w:2 * w]], axis=1))
        o_a = pv_a[:, 0:w] / (pv_a[:, w:2 * w] + e[0:half_rows])
        o_b = pv_b[:, w:2 * w] / (pv_b[:, 0:w] + e[half_rows:])
        for a in range(group):
            lo = jnp.where(low_half, o_a[a * w:(a + 1) * w], o_a[(group + a) * w:(group + a + 1) * w])
            hi = jnp.where(low_half, o_b[a * w:(a + 1) * w], o_b[(group + a) * w:(group + a + 1) * w])
            o_ref[n * w:(n + 1) * w, a * gw:(a + 1) * gw] = jnp.concatenate(
                [lo, hi], axis=1).astype(o_ref.dtype)

    n_blocks = q_ref.shape[0] // w
    scores, probs = {}, {}
    for step in range(n_blocks + 2):
        if step < n_blocks:
            scores[step] = stage_scores(step)
        if 1 <= step <= n_blocks:
            probs[step - 1] = stage_softmax(scores.pop(step - 1))
        if step >= 2:
            stage_out(step - 2, *probs.pop(step - 2))


def _swa_attn(qkv, sinks):
    bsz, s, _ = qkv.shape
    nq = SWA_HEADS * SWA_HEAD_DIM
    nkv = SWA_KV_HEADS * SWA_HEAD_DIM
    tm = TILES.swa_attn
    per = tm // WINDOW
    slopes = tuple(float(v) for v in 2.0 ** (-8.0 * np.arange(1, SWA_HEADS + 1) / SWA_HEADS))
    kcol, vcol = nq // nkv, nq // nkv + 1
    prev = lambda b, i: jnp.maximum(i * per - 1, 0)
    return pl.pallas_call(
        functools.partial(_swa_attn_kernel, slopes=slopes),
        grid=(bsz, s // tm),
        in_specs=[
            pl.BlockSpec(memory_space=pltpu.SMEM),
            pl.BlockSpec((None, tm, nq), lambda b, i: (b, i, 0)),
            pl.BlockSpec((None, tm, nkv), lambda b, i: (b, i, kcol)),
            pl.BlockSpec((None, tm, nkv), lambda b, i: (b, i, vcol)),
            pl.BlockSpec((None, WINDOW, nkv), lambda b, i: (b, prev(b, i), kcol)),
            pl.BlockSpec((None, WINDOW, nkv), lambda b, i: (b, prev(b, i), vcol)),
        ],
        out_specs=pl.BlockSpec((None, tm, nq), lambda b, i: (b, i, 0)),
        out_shape=jax.ShapeDtypeStruct((bsz, s, nq), BF16),
        scratch_shapes=[pltpu.VMEM((SWA_HEADS * WINDOW, 2 * WINDOW), F32)],
        compiler_params=_params("arbitrary", "arbitrary"),
        name="swa_attn",
    )(sinks.astype(F32), qkv, qkv, qkv, qkv, qkv)


def kernel(x, c, positions, w_ada, b_ada, g_mix, g_mlp, mla_w_dq, mla_g_q, mla_w_uq, mla_w_dkv,
           mla_g_kv, mla_w_ukv, mla_w_o, swa_w_qkv, swa_b_qkv, swa_sinks, swa_w_o, swa_b_o,
           w_ff1, w_ff2, g_final):
    depth = w_ada.shape[0]
    bsz, s, d = x.shape
    mods = _adaln(c, w_ada, b_ada).reshape(depth, bsz, 1, w_ada.shape[-1])
    cos_t, sin_t = _rope_tables(positions)
    zero_bias = jnp.zeros((d,), F32)
    for i in range(depth):
        j = i // 2
        last = i == depth - 1
        if i % 2 == 0:
            q, k, v = _mla_proj(x, mods, i, g_mix[i], cos_t, sin_t, mla_w_dq[j], mla_g_q[j],
                                mla_w_uq[j], mla_w_dkv[j], mla_g_kv[j], mla_w_ukv[j])
            attn = _mla_flash(q, k, v)
            w_o, b_o = mla_w_o[j], zero_bias
        else:
            nq = SWA_HEADS * SWA_HEAD_DIM
            w_qkv = swa_w_qkv[j].astype(BF16)
            w_qkv = jnp.concatenate([_swa_window_order(w_qkv[:, :nq], 1), w_qkv[:, nq:]], axis=1)
            b_qkv = jnp.concatenate([_swa_window_order(swa_b_qkv[j][:nq], 0), swa_b_qkv[j][nq:]])
            qkv = _swa_proj(x, mods, i, g_mix[i], w_qkv, b_qkv)
            attn = _swa_attn(qkv, swa_sinks[j])
            w_o, b_o = _swa_window_order(swa_w_o[j].astype(BF16), 0), swa_b_o[j]
        x = _post_mlp(x, attn, mods, i, w_o, b_o, g_mlp[i], w_ff1, w_ff2, g_final, last)
    return x
```

```python
import functools
import math
from typing import NamedTuple

import numpy as np
import jax
import jax.numpy as jnp
from jax import lax
from jax.experimental import pallas as pl
from jax.experimental.pallas import tpu as pltpu

F32 = jnp.float32
BF16 = jnp.bfloat16

EPS = 1e-6
ROPE_THETA = 10000.0
MLA_HEADS = 8
QK_NOPE = 128
QK_ROPE = 64
V_DIM = 128
KV_LORA = 256
SWA_HEADS = 16
SWA_KV_HEADS = 4
SWA_HEAD_DIM = 64
WINDOW = 128

LANES = 128
QK_PAD = 2 * LANES
ONES_ROWS = 16
VMEM_LIMIT = 56 * 1024 * 1024

NEG_INF = float("-inf")


class _Tiles(NamedTuple):
    adaln_rows: int = 256
    rope_tokens: int = 2048
    mla_proj: int = 1024
    mla_proj_chunk: int = 1024
    mlp: int = 1024
    mlp_ff: int = 1024
    mlp_chunk: int = 1024
    swa_proj: int = 1024
    swa_attn: int = 2048


TILES = _Tiles()


def _params(*sem):
    return pltpu.CompilerParams(dimension_semantics=sem, vmem_limit_bytes=VMEM_LIMIT)


def _rmsnorm(x, g):
    y = x * lax.rsqrt(jnp.mean(x * x, axis=-1, keepdims=True) + EPS)
    return y * g


def _dot(a, b):
    return jnp.dot(a, b, preferred_element_type=F32)


def _dot_nt(a, b):
    return lax.dot_general(a, b, (((1,), (1,)), ((), ())), preferred_element_type=F32)


def _adaln_kernel(ct_ref, w_ref, b_ref, o_ref):
    k = pl.program_id(1)
    ct = ct_ref[...]
    cond = ct * jax.nn.sigmoid(ct)
    w = w_ref[...]
    part = jnp.concatenate([jnp.sum(w * cond[:, b:b + 1], axis=0, keepdims=True)
                            for b in range(ct.shape[1])], axis=0)

    @pl.when(k == 0)
    def _():
        o_ref[...] = part + b_ref[...]

    @pl.when(k > 0)
    def _():
        o_ref[...] += part


def _adaln(c, w_ada, b_ada):
    depth, d, n = w_ada.shape
    bsz = c.shape[0]
    tk = TILES.adaln_rows
    return pl.pallas_call(
        _adaln_kernel,
        grid=(depth, d // tk),
        in_specs=[
            pl.BlockSpec((tk, bsz), lambda l, k: (k, 0)),
            pl.BlockSpec((None, tk, n), lambda l, k: (l, k, 0)),
            pl.BlockSpec((None, 1, n), lambda l, k: (l, 0, 0)),
        ],
        out_specs=pl.BlockSpec((None, bsz, n), lambda l, k: (l, 0, 0)),
        out_shape=jax.ShapeDtypeStruct((depth, bsz, n), F32),
        compiler_params=_params("parallel", "arbitrary"),
        name="adaln",
    )(c.T, w_ada, b_ada.reshape(depth, 1, n))


def _rope_table_kernel(pos_ref, inv_ref, cos_ref, sin_ref):
    ang = inv_ref[...] * pos_ref[...].astype(F32)
    cos_ref[...] = jnp.cos(ang)
    sin_ref[...] = jnp.sin(ang)


def _rope_tables(positions):
    n = positions.size
    half = QK_ROPE // 2
    inv_freq = ROPE_THETA ** (-jnp.arange(half, dtype=F32) / half)
    tm = TILES.rope_tokens
    return pl.pallas_call(
        _rope_table_kernel,
        grid=(n // tm,),
        in_specs=[pl.BlockSpec((1, tm), lambda i: (0, i)),
                  pl.BlockSpec((half, 1), lambda i: (0, 0))],
        out_specs=[pl.BlockSpec((half, tm), lambda i: (0, i)),
                   pl.BlockSpec((half, tm), lambda i: (0, i))],
        out_shape=[jax.ShapeDtypeStruct((half, n), F32)] * 2,
        compiler_params=_params("parallel"),
        name="rope_table",
    )(positions.reshape(1, n), inv_freq.reshape(half, 1))


def _mla_proj_kernel(x_ref, mod_ref, g_ref, cos_ref, sin_ref, wdq_ref, gq_ref, wuqt_ref,
                     wdkv_ref, gkv_ref, wuk_ref, wuvt_ref, qt_ref, k_ref, vt_ref, *, scale,
                     row_chunk):
    d = x_ref.shape[-1]
    mod = mod_ref[...]
    sh1, sc1 = mod[:, 0:d], mod[:, d:2 * d]
    reps = LANES // cos_ref.shape[0]
    n_chunks = x_ref.shape[0] // row_chunk
    zero_rows = jnp.zeros((QK_PAD - QK_NOPE - QK_ROPE, row_chunk), BF16)

    def down(c):
        rows = slice(c * row_chunk, (c + 1) * row_chunk)
        h = (_rmsnorm(x_ref[rows, :], g_ref[...]) * (1.0 + sc1) + sh1).astype(BF16)
        return _dot(h, wdq_ref[...]), _dot(h, wdkv_ref[...])

    def up(cq_raw, dkv):
        cq = _rmsnorm(cq_raw, gq_ref[...]).astype(BF16)
        ckv = _rmsnorm(dkv[:, 0:KV_LORA], gkv_ref[...]).astype(BF16)
        return (_dot_nt(wuqt_ref[...], cq), _dot(ckv, wuk_ref[...]),
                _dot_nt(wuvt_ref[...], ckv))

    def emit(c, dkv, qt, kn, vt):
        rows = slice(c * row_chunk, (c + 1) * row_chunk)
        cos_rep = jnp.tile(cos_ref[:, rows], (reps, 1))
        sin_rep = jnp.tile(sin_ref[:, rows], (reps, 1))
        cos_t, sin_t = cos_rep[0:QK_ROPE, :], sin_rep[0:QK_ROPE, :]
        cos, sin = cos_rep.T, sin_rep.T
        for hd in range(MLA_HEADS):
            base = hd * (QK_NOPE + QK_ROPE)
            qt_ref[hd, 0:QK_NOPE, rows] = (qt[base:base + QK_NOPE, :] * scale).astype(BF16)
            a = qt[base + QK_NOPE:base + QK_NOPE + QK_ROPE, :]
            b = jnp.concatenate([-a[QK_ROPE // 2:, :], a[0:QK_ROPE // 2, :]], axis=0)
            qt_ref[hd, QK_NOPE:QK_NOPE + QK_ROPE, rows] = (
                (a * cos_t + b * sin_t) * scale).astype(BF16)
            qt_ref[hd, QK_NOPE + QK_ROPE:QK_PAD, rows] = zero_rows
        kr = (dkv[:, KV_LORA:KV_LORA + LANES] * cos
              + dkv[:, KV_LORA + LANES:KV_LORA + 2 * LANES] * sin).astype(BF16)
        for hd in range(MLA_HEADS):
            k_ref[hd, rows, 0:LANES] = kn[:, hd * QK_NOPE:(hd + 1) * QK_NOPE].astype(BF16)
            k_ref[hd, rows, LANES:QK_PAD] = kr
            vt_ref[hd, :, rows] = vt[hd * V_DIM:(hd + 1) * V_DIM, :].astype(BF16)

    downs, ups = {}, {}
    for step in range(n_chunks + 2):
        if step < n_chunks:
            downs[step] = down(step)
        if 1 <= step <= n_chunks:
            ups[step - 1] = up(*downs[step - 1])
        if step >= 2:
            emit(step - 2, downs.pop(step - 2)[1], *ups.pop(step - 2))


def _rot_half(w):
    half = w.shape[-1] // 2
    return jnp.concatenate([-w[..., half:], w[..., :half]], axis=-1)


def _mla_proj(x, mods, layer, g, cos_t, sin_t, w_dq, g_q, w_uq, w_dkv, g_kv, w_ukv):
    bsz, s, d = x.shape
    hh = MLA_HEADS
    q_lora = w_dq.shape[1]
    tm, row_chunk = TILES.mla_proj, TILES.mla_proj_chunk
    wuqt_ext = w_uq.astype(BF16).T
    wkr = w_dkv[:, KV_LORA:]
    pad = jnp.zeros((d, LANES - QK_ROPE), w_dkv.dtype)
    wdkv_ext = jnp.concatenate([w_dkv[:, :KV_LORA], wkr, pad, _rot_half(wkr), pad],
                               axis=1).astype(BF16)
    wkv3 = w_ukv.astype(BF16).reshape(KV_LORA, hh, QK_NOPE + V_DIM)
    wuk = wkv3[:, :, :QK_NOPE].reshape(KV_LORA, hh * QK_NOPE)
    wuvt = wkv3[:, :, QK_NOPE:].reshape(KV_LORA, hh * V_DIM).T
    scale = float((QK_NOPE + QK_ROPE) ** -0.5 * math.log2(math.e))

    const = lambda b, i: (0, 0)
    tok = lambda b, i: (0, b * (s // tm) + i)
    out_map = lambda b, i: (b, 0, i, 0)
    out_map_t = lambda b, i: (b, 0, 0, i)
    return pl.pallas_call(
        functools.partial(_mla_proj_kernel, scale=scale, row_chunk=row_chunk),
        grid=(bsz, s // tm),
        in_specs=[
            pl.BlockSpec((None, tm, d), lambda b, i: (b, i, 0)),
            pl.BlockSpec((None, None, 1, mods.shape[-1]), lambda b, i: (layer, b, 0, 0)),
            pl.BlockSpec((1, d), const),
            pl.BlockSpec((cos_t.shape[0], tm), tok),
            pl.BlockSpec((sin_t.shape[0], tm), tok),
            pl.BlockSpec(w_dq.shape, const),
            pl.BlockSpec((1, q_lora), const),
            pl.BlockSpec(wuqt_ext.shape, const),
            pl.BlockSpec(wdkv_ext.shape, const),
            pl.BlockSpec((1, KV_LORA), const),
            pl.BlockSpec(wuk.shape, const),
            pl.BlockSpec(wuvt.shape, const),
        ],
        out_specs=[
            pl.BlockSpec((None, hh, None, QK_PAD, tm), lambda b, i: (b, 0, i, 0, 0)),
            pl.BlockSpec((None, hh, tm, QK_PAD), out_map),
            pl.BlockSpec((None, hh, V_DIM, tm), out_map_t),
        ],
        out_shape=[
            jax.ShapeDtypeStruct((bsz, hh, s // tm, QK_PAD, tm), BF16),
            jax.ShapeDtypeStruct((bsz, hh, s, QK_PAD), BF16),
            jax.ShapeDtypeStruct((bsz, hh, V_DIM, s), BF16),
        ],
        compiler_params=_params("parallel", "parallel"),
        name="mla_proj",
    )(x, mods, g.reshape(1, d), cos_t, sin_t, w_dq.astype(BF16), g_q.reshape(1, q_lora),
      wuqt_ext, wdkv_ext, g_kv.reshape(1, KV_LORA), wuk, wuvt)


def _mla_flash_kernel(qt_ref, k_ref, vt_ref, o_ref, vblk_sc, sn_sc, s0_sc, s1_sc, m_sc, acc_sc,
                      *, tq):
    half = tq // 2
    n_tiles = qt_ref.shape[0]

    for t in range(n_tiles):
        vblk_sc[t, 0:V_DIM, :] = vt_ref[:, t * tq:(t + 1) * tq]
        vblk_sc[t, V_DIM:, :] = jnp.ones((ONES_ROWS, tq), BF16)

    def tile(qi, carry):
        m_sc[...] = jnp.full(m_sc.shape, NEG_INF, F32)
        acc_sc[...] = jnp.zeros(acc_sc.shape, F32)

        def scores(t, s_ref):
            k = k_ref[pl.ds(pl.multiple_of(t * tq, tq), tq), :]
            s_ref[...] = _dot(k, qt_ref[qi])

        def prefetch():
            sn_sc[...] = _dot(k_ref[0:tq, :], qt_ref[jnp.minimum(qi + 1, n_tiles - 1)])

        def scores_diag(s_ref):
            base = pl.multiple_of(qi * tq, tq)
            s_ref[0:half, :] = _dot(k_ref[pl.ds(base, half), :], qt_ref[qi])
            s_ref[half:tq, half:tq] = _dot(k_ref[pl.ds(base + half, half), :],
                                           qt_ref[qi, :, half:tq])

        def softmax_pv(s, vt, cols):
            m_prev = m_sc[:, cols]
            m_new = jnp.maximum(m_prev, jnp.max(s, axis=0, keepdims=True))
            alpha = jnp.exp2(m_prev - m_new)
            p = jnp.exp2(s - m_new).astype(BF16)
            acc_sc[:, cols] = alpha * acc_sc[:, cols] + _dot(vt, p)
            m_sc[:, cols] = m_new

        def update(t, s_ref):
            softmax_pv(s_ref[...], vblk_sc[t], slice(0, tq))

        def update_diag(s_ref):
            causal = (lax.broadcasted_iota(jnp.int32, (half, half), 0)
                      <= lax.broadcasted_iota(jnp.int32, (half, half), 1))
            vt = vblk_sc[qi]
            s_top = s_ref[0:half, :]
            s_top = jnp.concatenate(
                [jnp.where(causal, s_top[:, 0:half], NEG_INF), s_top[:, half:tq]], axis=1)
            softmax_pv(s_top, vt[:, 0:half], slice(0, tq))
            s_bot = jnp.where(causal, s_ref[half:tq, half:tq], NEG_INF)
            softmax_pv(s_bot, vt[:, half:tq], slice(half, tq))

        def pair(u, c):
            t = 2 * u + 1
            scores(t + 1, s0_sc)
            update(t, s1_sc)
            scores(t + 2, s1_sc)
            update(t + 1, s0_sc)
            return c

        @pl.when(qi == 0)
        def _():
            scores_diag(s0_sc)
            prefetch()
            update_diag(s0_sc)

        @pl.when(qi == 1)
        def _():
            scores_diag(s1_sc)
            update(0, sn_sc)
            prefetch()
            update_diag(s1_sc)

        @pl.when(qi >= 2)
        def _():
            scores(1, s1_sc)
            update(0, sn_sc)

        even = qi % 2 == 0
        n_pairs = jnp.where(qi >= 2, jnp.where(even, (qi - 2) // 2, (qi - 3) // 2), 0)
        lax.fori_loop(0, n_pairs, pair, 0)

        @pl.when(jnp.logical_and(qi >= 2, even))
        def _():
            scores_diag(s0_sc)
            update(qi - 1, s1_sc)
            prefetch()
            update_diag(s0_sc)

        @pl.when(jnp.logical_and(qi >= 2, jnp.logical_not(even)))
        def _():
            scores(qi - 1, s0_sc)
            update(qi - 2, s1_sc)
            scores_diag(s1_sc)
            update(qi - 1, s0_sc)
            prefetch()
            update_diag(s1_sc)

        acc = acc_sc[...]
        o_t = acc[0:V_DIM, :] / acc[V_DIM:V_DIM + 1, :]
        o_ref[pl.ds(pl.multiple_of(qi * tq, tq), tq), :] = o_t.T.astype(o_ref.dtype)
        return carry

    lax.fori_loop(0, n_tiles, tile, 0)


def _mla_flash(qt, k, vt):
    bsz, hh, s, _ = k.shape
    tq = qt.shape[-1]
    return pl.pallas_call(
        functools.partial(_mla_flash_kernel, tq=tq),
        grid=(bsz, hh),
        in_specs=[
            pl.BlockSpec((None, None) + qt.shape[2:], lambda b, h: (b, h, 0, 0, 0)),
            pl.BlockSpec((None, None, s, QK_PAD), lambda b, h: (b, h, 0, 0)),
            pl.BlockSpec((None, None, V_DIM, s), lambda b, h: (b, h, 0, 0)),
        ],
        out_specs=pl.BlockSpec((None, s, V_DIM), lambda b, h: (b, 0, h)),
        out_shape=jax.ShapeDtypeStruct((bsz, s, hh * V_DIM), BF16),
        scratch_shapes=[pltpu.VMEM((s // tq, V_DIM + ONES_ROWS, tq), BF16),
                        pltpu.VMEM((tq, tq), F32), pltpu.VMEM((tq, tq), F32),
                        pltpu.VMEM((tq, tq), F32),
                        pltpu.VMEM((1, tq), F32), pltpu.VMEM((V_DIM + ONES_ROWS, tq), F32)],
        compiler_params=_params("parallel", "parallel"),
        name="mla_flash",
    )(qt, k, vt)


def _post_mlp_kernel(x_ref, a_ref, mod_ref, wo_ref, bo_ref, g_ref, w1_ref, w2_ref, gf_ref,
                     out_ref, x1_sc, h_sc, acc_sc, *, final_norm, n_ff, row_chunk):
    j = pl.program_id(2)
    d = x_ref.shape[-1]
    tm = x_ref.shape[0]

    def step(first, last):
        mod = mod_ref[...]
        gt1, sh2, sc2 = mod[:, 2 * d:3 * d], mod[:, 3 * d:4 * d], mod[:, 4 * d:5 * d]
        gt2 = mod[:, 5 * d:6 * d]
        w1 = w1_ref[...].astype(BF16)
        w2 = w2_ref[...].astype(BF16)
        for r in range(tm // row_chunk):
            rows = slice(r * row_chunk, (r + 1) * row_chunk)
            if first:
                y = _dot(a_ref[rows, :], wo_ref[...].astype(BF16)) + bo_ref[...]
                x1 = x_ref[rows, :] + gt1 * y
                h = (_rmsnorm(x1, g_ref[...]) * (1.0 + sc2) + sh2).astype(BF16)
                h_sc[rows, :] = h
                if not last:
                    x1_sc[rows, :] = x1
            else:
                h = h_sc[rows, :]
            u = jnp.square(jnp.maximum(_dot(h, w1), 0.0)).astype(BF16)
            acc = _dot(u, w2)
            if not first:
                acc = acc_sc[rows, :] + acc
            if last:
                x2 = (x1 if first else x1_sc[rows, :]) + gt2 * acc
                if final_norm:
                    x2 = _rmsnorm(x2, gf_ref[...])
                out_ref[rows, :] = x2
            else:
                acc_sc[rows, :] = acc

    pl.when(j == 0)(functools.partial(step, True, n_ff == 1))
    if n_ff > 1:
        pl.when(j == n_ff - 1)(functools.partial(step, False, True))
    if n_ff > 2:
        pl.when(jnp.logical_and(j > 0, j < n_ff - 1))(functools.partial(step, False, False))


def _post_mlp(x, attn, mods, layer, w_o, b_o, g_mlp, w_ff1, w_ff2, g_final, final_norm):
    bsz, s, d = x.shape
    dff = w_ff1.shape[-1]
    tm, tf, row_chunk = TILES.mlp, TILES.mlp_ff, TILES.mlp_chunk
    const = lambda b, i, j: (0, 0)
    tok = lambda b, i, j: (b, i, 0)
    return pl.pallas_call(
        functools.partial(_post_mlp_kernel, final_norm=final_norm, n_ff=dff // tf,
                          row_chunk=row_chunk),
        grid=(bsz, s // tm, dff // tf),
        in_specs=[
            pl.BlockSpec((None, tm, d), tok),
            pl.BlockSpec((None, tm, attn.shape[-1]), tok),
            pl.BlockSpec((None, None, 1, mods.shape[-1]), lambda b, i, j: (layer, b, 0, 0)),
            pl.BlockSpec(w_o.shape, const),
            pl.BlockSpec((1, d), const),
            pl.BlockSpec((1, d), const),
            pl.BlockSpec((None, d, tf), lambda b, i, j: (layer, 0, j)),
            pl.BlockSpec((None, tf, d), lambda b, i, j: (layer, j, 0)),
            pl.BlockSpec((1, d), const),
        ],
        out_specs=pl.BlockSpec((None, tm, d), tok),
        out_shape=jax.ShapeDtypeStruct((bsz, s, d), F32),
        scratch_shapes=[pltpu.VMEM((tm, d), F32), pltpu.VMEM((tm, d), BF16),
                        pltpu.VMEM((tm, d), F32)],
        compiler_params=_params("parallel", "parallel", "arbitrary"),
        name="post_mlp_final" if final_norm else "post_mlp",
    )(x, attn, mods, w_o, b_o.reshape(1, d), g_mlp.reshape(1, d),
      w_ff1, w_ff2, g_final.reshape(1, d))


def _swa_proj_kernel(x_ref, mod_ref, g_ref, w_ref, b_ref, o_ref, *, scale):
    d = x_ref.shape[-1]
    mod = mod_ref[...]
    sh1, sc1 = mod[:, 0:d], mod[:, d:2 * d]
    h = (_rmsnorm(x_ref[...], g_ref[...]) * (1.0 + sc1) + sh1).astype(BF16)
    qkv = _dot(h, w_ref[...]) + b_ref[...]
    nq = SWA_HEADS * SWA_HEAD_DIM
    o_ref[:, 0:nq] = (qkv[:, 0:nq] * scale).astype(BF16)
    o_ref[:, nq:] = qkv[:, nq:].astype(BF16)


def _swa_window_order(t, axis):
    group = SWA_HEADS // SWA_KV_HEADS
    shape = t.shape
    t = t.reshape(shape[:axis] + (SWA_KV_HEADS, group, SWA_HEAD_DIM) + shape[axis + 1:])
    return jnp.swapaxes(t, axis, axis + 1).reshape(shape)


def _swa_proj(x, mods, layer, g, w_qkv, b_qkv):
    bsz, s, d = x.shape
    n = w_qkv.shape[1]
    tm = TILES.swa_proj
    const = lambda b, i: (0, 0)
    return pl.pallas_call(
        functools.partial(_swa_proj_kernel,
                          scale=float(SWA_HEAD_DIM ** -0.5 * math.log2(math.e))),
        grid=(bsz, s // tm),
        in_specs=[
            pl.BlockSpec((None, tm, d), lambda b, i: (b, i, 0)),
            pl.BlockSpec((None, None, 1, mods.shape[-1]), lambda b, i: (layer, b, 0, 0)),
            pl.BlockSpec((1, d), const),
            pl.BlockSpec(w_qkv.shape, const),
            pl.BlockSpec((1, n), const),
        ],
        out_specs=pl.BlockSpec((None, tm, n), lambda b, i: (b, i, 0)),
        out_shape=jax.ShapeDtypeStruct((bsz, s, n), BF16),
        compiler_params=_params("parallel", "parallel"),
        name="swa_proj",
    )(x, mods, g.reshape(1, d), w_qkv.astype(BF16), b_qkv.reshape(1, n))


def _swa_attn_kernel(sink_ref, q_ref, kc_ref, vc_ref, kp_ref, vp_ref, o_ref, bias_sc, *, slopes):
    first_step = jnp.logical_and(pl.program_id(0) == 0, pl.program_id(1) == 0)
    w = WINDOW
    dh = SWA_HEAD_DIM
    n_kv = SWA_KV_HEADS
    group = SWA_HEADS // n_kv
    gw = n_kv * dh
    log2e = math.log2(math.e)

    @pl.when(first_step)
    def _():
        qpos = lax.broadcasted_iota(jnp.int32, (w, 2 * w), 0)
        kpos = lax.broadcasted_iota(jnp.int32, (w, 2 * w), 1)
        dist = w + qpos - kpos
        in_window = jnp.logical_and(dist >= 0, dist < w)
        dist_f = dist.astype(F32)
        for hd in range(SWA_HEADS):
            bias_sc[hd * w:(hd + 1) * w, :] = jnp.where(
                in_window, dist_f * (-slopes[hd] * log2e), NEG_INF)

    no_prev = jnp.where(pl.program_id(1) == 0, NEG_INF, 0.0)
    ones_cols = jnp.ones((2 * w, w), BF16)
    lane_group = lax.broadcasted_iota(jnp.int32, (w, n_kv * dh), 1) // dh
    low_half = lax.broadcasted_iota(jnp.int32, (w, w), 1) < dh
    half_rows = (n_kv // 2) * group * w
    sink = jnp.concatenate([jnp.full((w, w), sink_ref[hd] * log2e, F32)
                            for hd in range(SWA_HEADS)], axis=0)

    def band(cur_ref, prev_ref, n):
        if n == 0:
            return jnp.concatenate([prev_ref[...], cur_ref[0:w, :]], axis=0)
        return cur_ref[(n - 1) * w:(n + 1) * w, :]

    def stage_scores(n):
        lhs = []
        for g in range(n_kv):
            for a in range(group):
                qwin = q_ref[n * w:(n + 1) * w, a * gw:(a + 1) * gw]
                lhs.append(jnp.where(lane_group == g, qwin, jnp.zeros_like(qwin)))
        s = _dot_nt(jnp.concatenate(lhs, axis=0), band(kc_ref, kp_ref, n)) + bias_sc[...]
        if n == 0:
            s = jnp.concatenate([s[:, 0:w] + no_prev, s[:, w:2 * w]], axis=1)
        return s

    def stage_softmax(s):
        m = jnp.maximum(jnp.max(s, axis=-1, keepdims=True), sink)
        return jnp.exp2(s - jnp.tile(m, (1, 2))).astype(BF16), m

    def stage_out(n, p, m):
        vall = band(vc_ref, vp_ref, n)
        pv_a = _dot(p[0:half_rows], jnp.concatenate([vall[:, 0:w], ones_cols], axis=1))
        pv_b = _dot(p[half_rows:], jnp.concatenate([ones_cols, vall[:, w:2 * w]], axis=1))
        o_a, sum_a = pv_a[:, 0:w], pv_a[:, w:2 * w]
        o_b, sum_b = pv_b[:, w:2 * w], pv_b[:, 0:w]

        def pick(t, a):
            return jnp.where(low_half, t[a * w:(a + 1) * w], t[(group + a) * w:(group + a + 1) * w])

        def merged(o, psum, sink_rows, m_rows, a):
            sink_term = jnp.exp2(pick(sink_rows, a) - pick(m_rows, a))
            return pick(o, a) / (pick(psum, a) + sink_term)

        for a in range(group):
            lo = merged(o_a, sum_a, sink[0:half_rows], m[0:half_rows], a)
            hi = merged(o_b, sum_b, sink[half_rows:], m[half_rows:], a)
            o_ref[n * w:(n + 1) * w, a * gw:(a + 1) * gw] = jnp.concatenate(
                [lo, hi], axis=1).astype(o_ref.dtype)

    n_blocks = q_ref.shape[0] // w
    scores, probs = {}, {}
    for step in range(n_blocks + 2):
        if step < n_blocks:
            scores[step] = stage_scores(step)
        if 1 <= step <= n_blocks:
            probs[step - 1] = stage_softmax(scores.pop(step - 1))
        if step >= 2:
            stage_out(step - 2, *probs.pop(step - 2))


def _swa_attn(qkv, sinks):
    bsz, s, _ = qkv.shape
    nq = SWA_HEADS * SWA_HEAD_DIM
    nkv = SWA_KV_HEADS * SWA_HEAD_DIM
    tm = TILES.swa_attn
    per = tm // WINDOW
    slopes = tuple(float(v) for v in 2.0 ** (-8.0 * np.arange(1, SWA_HEADS + 1) / SWA_HEADS))
    kcol, vcol = nq // nkv, nq // nkv + 1
    prev = lambda b, i: jnp.maximum(i * per - 1, 0)
    return pl.pallas_call(
        functools.partial(_swa_attn_kernel, slopes=slopes),
        grid=(bsz, s // tm),
        in_specs=[
            pl.BlockSpec(memory_space=pltpu.SMEM),
            pl.BlockSpec((None, tm, nq), lambda b, i: (b, i, 0)),
            pl.BlockSpec((None, tm, nkv), lambda b, i: (b, i, kcol)),
            pl.BlockSpec((None, tm, nkv), lambda b, i: (b, i, vcol)),
            pl.BlockSpec((None, WINDOW, nkv), lambda b, i: (b, prev(b, i), kcol)),
            pl.BlockSpec((None, WINDOW, nkv), lambda b, i: (b, prev(b, i), vcol)),
        ],
        out_specs=pl.BlockSpec((None, tm, nq), lambda b, i: (b, i, 0)),
        out_shape=jax.ShapeDtypeStruct((bsz, s, nq), BF16),
        scratch_shapes=[pltpu.VMEM((SWA_HEADS * WINDOW, 2 * WINDOW), F32)],
        compiler_params=_params("arbitrary", "arbitrary"),
        name="swa_attn",
    )(sinks.astype(F32), qkv, qkv, qkv, qkv, qkv)


def kernel(x, c, positions, w_ada, b_ada, g_mix, g_mlp, mla_w_dq, mla_g_q, mla_w_uq, mla_w_dkv,
           mla_g_kv, mla_w_ukv, mla_w_o, swa_w_qkv, swa_b_qkv, swa_sinks, swa_w_o, swa_b_o,
           w_ff1, w_ff2, g_final):
    depth = w_ada.shape[0]
    bsz, s, d = x.shape
    mods = _adaln(c, w_ada, b_ada).reshape(depth, bsz, 1, w_ada.shape[-1])
    cos_t, sin_t = _rope_tables(positions)
    zero_bias = jnp.zeros((d,), F32)
    for i in range(depth):
        j = i // 2
        last = i == depth - 1
        if i % 2 == 0:
            q, k, v = _mla_proj(x, mods, i, g_mix[i], cos_t, sin_t, mla_w_dq[j], mla_g_q[j],
                                mla_w_uq[j], mla_w_dkv[j], mla_g_kv[j], mla_w_ukv[j])
            attn = _mla_flash(q, k, v)
            w_o, b_o = mla_w_o[j], zero_bias
        else:
            nq = SWA_HEADS * SWA_HEAD_DIM
            w_qkv = swa_w_qkv[j].astype(BF16)
            w_qkv = jnp.concatenate([_swa_window_order(w_qkv[:, :nq], 1), w_qkv[:, nq:]], axis=1)
            b_qkv = jnp.concatenate([_swa_window_order(swa_b_qkv[j][:nq], 0), swa_b_qkv[j][nq:]])
            qkv = _swa_proj(x, mods, i, g_mix[i], w_qkv, b_qkv)
            attn = _swa_attn(qkv, swa_sinks[j])
            w_o, b_o = _swa_window_order(swa_w_o[j].astype(BF16), 0), swa_b_o[j]
        x = _post_mlp(x, attn, mods, i, w_o, b_o, g_mlp[i], w_ff1, w_ff2, g_final, last)
    return x
```

```python
import functools
import math
from typing import NamedTuple

import numpy as np
import jax
import jax.numpy as jnp
from jax import lax
from jax.experimental import pallas as pl
from jax.experimental.pallas import tpu as pltpu

F32 = jnp.float32
BF16 = jnp.bfloat16

EPS = 1e-6
ROPE_THETA = 10000.0
MLA_HEADS = 8
QK_NOPE = 128
QK_ROPE = 64
V_DIM = 128
KV_LORA = 256
SWA_HEADS = 16
SWA_KV_HEADS = 4
SWA_HEAD_DIM = 64
WINDOW = 128

LANES = 128
QK_PAD = 2 * LANES
ONES_ROWS = 16
VMEM_LIMIT = 56 * 1024 * 1024

NEG_INF = float("-inf")


class _Tiles(NamedTuple):
    adaln_cols: int = 2048
    rope_tokens: int = 2048
    mla_proj: int = 1024
    mla_proj_chunk: int = 1024
    mlp: int = 1024
    mlp_ff: int = 1024
    mlp_chunk: int = 1024
    swa_proj: int = 1024
    swa_attn: int = 2048


TILES = _Tiles()


def _params(*sem):
    return pltpu.CompilerParams(dimension_semantics=sem, vmem_limit_bytes=VMEM_LIMIT)


def _rmsnorm(x, g):
    y = x * lax.rsqrt(jnp.mean(x * x, axis=-1, keepdims=True) + EPS)
    return y * g


def _dot(a, b):
    return jnp.dot(a, b, preferred_element_type=F32)


def _dot_nt(a, b):
    return lax.dot_general(a, b, (((1,), (1,)), ((), ())), preferred_element_type=F32)


def _adaln_kernel(ct_ref, w_ref, b_ref, o_ref):
    ct = ct_ref[...]
    cond = ct * jax.nn.sigmoid(ct)
    w = w_ref[...]
    rows = [jnp.sum(w * cond[:, b:b + 1], axis=0, keepdims=True)
            for b in range(ct.shape[1])]
    o_ref[...] = jnp.concatenate(rows, axis=0) + b_ref[...]


def _adaln(c, w_ada, b_ada):
    depth, d, n = w_ada.shape
    bsz = c.shape[0]
    tn = TILES.adaln_cols
    return pl.pallas_call(
        _adaln_kernel,
        grid=(depth, n // tn),
        in_specs=[
            pl.BlockSpec((d, bsz), lambda l, j: (0, 0)),
            pl.BlockSpec((None, d, tn), lambda l, j: (l, 0, j)),
            pl.BlockSpec((None, 1, tn), lambda l, j: (l, 0, j)),
        ],
        out_specs=pl.BlockSpec((None, bsz, tn), lambda l, j: (l, 0, j)),
        out_shape=jax.ShapeDtypeStruct((depth, bsz, n), F32),
        compiler_params=_params("parallel", "parallel"),
        name="adaln",
    )(c.T, w_ada, b_ada.reshape(depth, 1, n))


def _rope_table_kernel(pos_ref, inv_ref, cos_ref, sin_ref):
    ang = inv_ref[...] * pos_ref[...].astype(F32)
    cos_ref[...] = jnp.cos(ang)
    sin_ref[...] = jnp.sin(ang)


def _rope_tables(positions):
    n = positions.size
    half = QK_ROPE // 2
    inv_freq = ROPE_THETA ** (-jnp.arange(half, dtype=F32) / half)
    tm = TILES.rope_tokens
    return pl.pallas_call(
        _rope_table_kernel,
        grid=(n // tm,),
        in_specs=[pl.BlockSpec((1, tm), lambda i: (0, i)),
                  pl.BlockSpec((half, 1), lambda i: (0, 0))],
        out_specs=[pl.BlockSpec((half, tm), lambda i: (0, i)),
                   pl.BlockSpec((half, tm), lambda i: (0, i))],
        out_shape=[jax.ShapeDtypeStruct((half, n), F32)] * 2,
        compiler_params=_params("parallel"),
        name="rope_table",
    )(positions.reshape(1, n), inv_freq.reshape(half, 1))


def _mla_proj_kernel(x_ref, mod_ref, g_ref, cos_ref, sin_ref, wdq_ref, gq_ref, wuqt_ref,
                     wdkv_ref, gkv_ref, wuk_ref, wuvt_ref, qt_ref, k_ref, vt_ref, *, scale,
                     row_chunk):
    d = x_ref.shape[-1]
    mod = mod_ref[...]
    sh1, sc1 = mod[:, 0:d], mod[:, d:2 * d]
    reps = LANES // cos_ref.shape[0]
    n_chunks = x_ref.shape[0] // row_chunk
    zero_rows = jnp.zeros((QK_PAD - QK_NOPE - QK_ROPE, row_chunk), BF16)

    def down(c):
        rows = slice(c * row_chunk, (c + 1) * row_chunk)
        h = (_rmsnorm(x_ref[rows, :], g_ref[...]) * (1.0 + sc1) + sh1).astype(BF16)
        return _dot(h, wdq_ref[...]), _dot(h, wdkv_ref[...])

    def up(cq_raw, dkv):
        cq = _rmsnorm(cq_raw, gq_ref[...]).astype(BF16)
        ckv = _rmsnorm(dkv[:, 0:KV_LORA], gkv_ref[...]).astype(BF16)
        return (_dot_nt(wuqt_ref[...], cq), _dot(ckv, wuk_ref[...]),
                _dot_nt(wuvt_ref[...], ckv))

    def emit(c, dkv, qt, kn, vt):
        rows = slice(c * row_chunk, (c + 1) * row_chunk)
        cos_rep = jnp.tile(cos_ref[:, rows], (reps, 1))
        sin_rep = jnp.tile(sin_ref[:, rows], (reps, 1))
        cos_t, sin_t = cos_rep[0:QK_ROPE, :], sin_rep[0:QK_ROPE, :]
        cos, sin = cos_rep.T, sin_rep.T
        for hd in range(MLA_HEADS):
            base = hd * (QK_NOPE + QK_ROPE)
            qt_ref[hd, 0:QK_NOPE, rows] = (qt[base:base + QK_NOPE, :] * scale).astype(BF16)
            a = qt[base + QK_NOPE:base + QK_NOPE + QK_ROPE, :]
            b = jnp.concatenate([-a[QK_ROPE // 2:, :], a[0:QK_ROPE // 2, :]], axis=0)
            qt_ref[hd, QK_NOPE:QK_NOPE + QK_ROPE, rows] = (
                (a * cos_t + b * sin_t) * scale).astype(BF16)
            qt_ref[hd, QK_NOPE + QK_ROPE:QK_PAD, rows] = zero_rows
        kr = (dkv[:, KV_LORA:KV_LORA + LANES] * cos
              + dkv[:, KV_LORA + LANES:KV_LORA + 2 * LANES] * sin).astype(BF16)
        for hd in range(MLA_HEADS):
            k_ref[hd, rows, 0:LANES] = kn[:, hd * QK_NOPE:(hd + 1) * QK_NOPE].astype(BF16)
            k_ref[hd, rows, LANES:QK_PAD] = kr
            vt_ref[hd, :, rows] = vt[hd * V_DIM:(hd + 1) * V_DIM, :].astype(BF16)

    downs, ups = {}, {}
    for step in range(n_chunks + 2):
        if step < n_chunks:
            downs[step] = down(step)
        if 1 <= step <= n_chunks:
            ups[step - 1] = up(*downs[step - 1])
        if step >= 2:
            emit(step - 2, downs.pop(step - 2)[1], *ups.pop(step - 2))


def _rot_half(w):
    half = w.shape[-1] // 2
    return jnp.concatenate([-w[..., half:], w[..., :half]], axis=-1)


def _mla_proj(x, mods, layer, g, cos_t, sin_t, w_dq, g_q, w_uq, w_dkv, g_kv, w_ukv):
    bsz, s, d = x.shape
    hh = MLA_HEADS
    q_lora = w_dq.shape[1]
    tm, row_chunk = TILES.mla_proj, TILES.mla_proj_chunk
    wuqt_ext = w_uq.astype(BF16).T
    wkr = w_dkv[:, KV_LORA:]
    pad = jnp.zeros((d, LANES - QK_ROPE), w_dkv.dtype)
    wdkv_ext = jnp.concatenate([w_dkv[:, :KV_LORA], wkr, pad, _rot_half(wkr), pad],
                               axis=1).astype(BF16)
    wkv3 = w_ukv.astype(BF16).reshape(KV_LORA, hh, QK_NOPE + V_DIM)
    wuk = wkv3[:, :, :QK_NOPE].reshape(KV_LORA, hh * QK_NOPE)
    wuvt = wkv3[:, :, QK_NOPE:].reshape(KV_LORA, hh * V_DIM).T
    scale = float((QK_NOPE + QK_ROPE) ** -0.5 * math.log2(math.e))

    const = lambda b, i: (0, 0)
    tok = lambda b, i: (0, b * (s // tm) + i)
    out_map = lambda b, i: (b, 0, i, 0)
    out_map_t = lambda b, i: (b, 0, 0, i)
    return pl.pallas_call(
        functools.partial(_mla_proj_kernel, scale=scale, row_chunk=row_chunk),
        grid=(bsz, s // tm),
        in_specs=[
            pl.BlockSpec((None, tm, d), lambda b, i: (b, i, 0)),
            pl.BlockSpec((None, None, 1, mods.shape[-1]), lambda b, i: (layer, b, 0, 0)),
            pl.BlockSpec((1, d), const),
            pl.BlockSpec((cos_t.shape[0], tm), tok),
            pl.BlockSpec((sin_t.shape[0], tm), tok),
            pl.BlockSpec(w_dq.shape, const),
            pl.BlockSpec((1, q_lora), const),
            pl.BlockSpec(wuqt_ext.shape, const),
            pl.BlockSpec(wdkv_ext.shape, const),
            pl.BlockSpec((1, KV_LORA), const),
            pl.BlockSpec(wuk.shape, const),
            pl.BlockSpec(wuvt.shape, const),
        ],
        out_specs=[
            pl.BlockSpec((None, hh, None, QK_PAD, tm), lambda b, i: (b, 0, i, 0, 0)),
            pl.BlockSpec((None, hh, tm, QK_PAD), out_map),
            pl.BlockSpec((None, hh, V_DIM, tm), out_map_t),
        ],
        out_shape=[
            jax.ShapeDtypeStruct((bsz, hh, s // tm, QK_PAD, tm), BF16),
            jax.ShapeDtypeStruct((bsz, hh, s, QK_PAD), BF16),
            jax.ShapeDtypeStruct((bsz, hh, V_DIM, s), BF16),
        ],
        compiler_params=_params("parallel", "parallel"),
        name="mla_proj",
    )(x, mods, g.reshape(1, d), cos_t, sin_t, w_dq.astype(BF16), g_q.reshape(1, q_lora),
      wuqt_ext, wdkv_ext, g_kv.reshape(1, KV_LORA), wuk, wuvt)


def _mla_flash_kernel(qt_ref, k_ref, vt_ref, o_ref, vblk_sc, sn_sc, s0_sc, s1_sc, m_sc, acc_sc,
                      *, tq):
    half = tq // 2
    n_tiles = qt_ref.shape[0]

    for t in range(n_tiles):
        vblk_sc[t, 0:V_DIM, :] = vt_ref[:, t * tq:(t + 1) * tq]
        vblk_sc[t, V_DIM:, :] = jnp.ones((ONES_ROWS, tq), BF16)

    def tile(qi, carry):
        m_sc[...] = jnp.full(m_sc.shape, NEG_INF, F32)
        acc_sc[...] = jnp.zeros(acc_sc.shape, F32)

        def scores(t, s_ref):
            k = k_ref[pl.ds(pl.multiple_of(t * tq, tq), tq), :]
            s_ref[...] = _dot(k, qt_ref[qi])

        def prefetch():
            sn_sc[...] = _dot(k_ref[0:tq, :], qt_ref[jnp.minimum(qi + 1, n_tiles - 1)])

        def scores_diag(s_ref):
            base = pl.multiple_of(qi * tq, tq)
            s_ref[0:half, :] = _dot(k_ref[pl.ds(base, half), :], qt_ref[qi])
            s_ref[half:tq, half:tq] = _dot(k_ref[pl.ds(base + half, half), :],
                                           qt_ref[qi, :, half:tq])

        def softmax_pv(s, vt, cols):
            m_prev = m_sc[:, cols]
            m_new = jnp.maximum(m_prev, jnp.max(s, axis=0, keepdims=True))
            alpha = jnp.exp2(m_prev - m_new)
            p = jnp.exp2(s - m_new).astype(BF16)
            acc_sc[:, cols] = alpha * acc_sc[:, cols] + _dot(vt, p)
            m_sc[:, cols] = m_new

        def update(t, s_ref):
            softmax_pv(s_ref[...], vblk_sc[t], slice(0, tq))

        def update_diag(s_ref):
            causal = (lax.broadcasted_iota(jnp.int32, (half, half), 0)
                      <= lax.broadcasted_iota(jnp.int32, (half, half), 1))
            vt = vblk_sc[qi]
            s_top = s_ref[0:half, :]
            s_top = jnp.concatenate(
                [jnp.where(causal, s_top[:, 0:half], NEG_INF), s_top[:, half:tq]], axis=1)
            softmax_pv(s_top, vt[:, 0:half], slice(0, tq))
            s_bot = jnp.where(causal, s_ref[half:tq, half:tq], NEG_INF)
            softmax_pv(s_bot, vt[:, half:tq], slice(half, tq))

        def pair(u, c):
            t = 2 * u + 1
            scores(t + 1, s0_sc)
            update(t, s1_sc)
            scores(t + 2, s1_sc)
            update(t + 1, s0_sc)
            return c

        @pl.when(qi == 0)
        def _():
            scores_diag(s0_sc)
            prefetch()
            update_diag(s0_sc)

        @pl.when(qi == 1)
        def _():
            scores_diag(s1_sc)
            update(0, sn_sc)
            prefetch()
            update_diag(s1_sc)

        @pl.when(qi >= 2)
        def _():
            scores(1, s1_sc)
            update(0, sn_sc)

        even = qi % 2 == 0
        n_pairs = jnp.where(qi >= 2, jnp.where(even, (qi - 2) // 2, (qi - 3) // 2), 0)
        lax.fori_loop(0, n_pairs, pair, 0)

        @pl.when(jnp.logical_and(qi >= 2, even))
        def _():
            scores_diag(s0_sc)
            update(qi - 1, s1_sc)
            prefetch()
            update_diag(s0_sc)

        @pl.when(jnp.logical_and(qi >= 2, jnp.logical_not(even)))
        def _():
            scores(qi - 1, s0_sc)
            update(qi - 2, s1_sc)
            scores_diag(s1_sc)
            update(qi - 1, s0_sc)
            prefetch()
            update_diag(s1_sc)

        acc = acc_sc[...]
        o_t = acc[0:V_DIM, :] / acc[V_DIM:V_DIM + 1, :]
        o_ref[pl.ds(pl.multiple_of(qi * tq, tq), tq), :] = o_t.T.astype(o_ref.dtype)
        return carry

    lax.fori_loop(0, n_tiles, tile, 0)


def _mla_flash(qt, k, vt):
    bsz, hh, s, _ = k.shape
    tq = qt.shape[-1]
    return pl.pallas_call(
        functools.partial(_mla_flash_kernel, tq=tq),
        grid=(bsz, hh),
        in_specs=[
            pl.BlockSpec((None, None) + qt.shape[2:], lambda b, h: (b, h, 0, 0, 0)),
            pl.BlockSpec((None, None, s, QK_PAD), lambda b, h: (b, h, 0, 0)),
            pl.BlockSpec((None, None, V_DIM, s), lambda b, h: (b, h, 0, 0)),
        ],
        out_specs=pl.BlockSpec((None, s, V_DIM), lambda b, h: (b, 0, h)),
        out_shape=jax.ShapeDtypeStruct((bsz, s, hh * V_DIM), BF16),
        scratch_shapes=[pltpu.VMEM((s // tq, V_DIM + ONES_ROWS, tq), BF16),
                        pltpu.VMEM((tq, tq), F32), pltpu.VMEM((tq, tq), F32),
                        pltpu.VMEM((tq, tq), F32),
                        pltpu.VMEM((1, tq), F32), pltpu.VMEM((V_DIM + ONES_ROWS, tq), F32)],
        compiler_params=_params("parallel", "parallel"),
        name="mla_flash",
    )(qt, k, vt)


def _post_mlp_kernel(x_ref, a_ref, mod_ref, wo_ref, bo_ref, g_ref, w1_ref, w2_ref, gf_ref,
                     out_ref, x1_sc, h_sc, acc_sc, *, final_norm, n_ff, row_chunk):
    j = pl.program_id(2)
    d = x_ref.shape[-1]
    tm = x_ref.shape[0]

    def step(first, last):
        mod = mod_ref[...]
        gt1, sh2, sc2 = mod[:, 2 * d:3 * d], mod[:, 3 * d:4 * d], mod[:, 4 * d:5 * d]
        gt2 = mod[:, 5 * d:6 * d]
        w1 = w1_ref[...].astype(BF16)
        w2 = w2_ref[...].astype(BF16)
        for r in range(tm // row_chunk):
            rows = slice(r * row_chunk, (r + 1) * row_chunk)
            if first:
                y = _dot(a_ref[rows, :], wo_ref[...].astype(BF16)) + bo_ref[...]
                x1 = x_ref[rows, :] + gt1 * y
                h = (_rmsnorm(x1, g_ref[...]) * (1.0 + sc2) + sh2).astype(BF16)
                h_sc[rows, :] = h
                if not last:
                    x1_sc[rows, :] = x1
            else:
                h = h_sc[rows, :]
            u = jnp.square(jnp.maximum(_dot(h, w1), 0.0)).astype(BF16)
            acc = _dot(u, w2)
            if not first:
                acc = acc_sc[rows, :] + acc
            if last:
                x2 = (x1 if first else x1_sc[rows, :]) + gt2 * acc
                if final_norm:
                    x2 = _rmsnorm(x2, gf_ref[...])
                out_ref[rows, :] = x2
            else:
                acc_sc[rows, :] = acc

    pl.when(j == 0)(functools.partial(step, True, n_ff == 1))
    if n_ff > 1:
        pl.when(j == n_ff - 1)(functools.partial(step, False, True))
    if n_ff > 2:
        pl.when(jnp.logical_and(j > 0, j < n_ff - 1))(functools.partial(step, False, False))


def _post_mlp(x, attn, mods, layer, w_o, b_o, g_mlp, w_ff1, w_ff2, g_final, final_norm):
    bsz, s, d = x.shape
    dff = w_ff1.shape[-1]
    tm, tf, row_chunk = TILES.mlp, TILES.mlp_ff, TILES.mlp_chunk
    const = lambda b, i, j: (0, 0)
    tok = lambda b, i, j: (b, i, 0)
    return pl.pallas_call(
        functools.partial(_post_mlp_kernel, final_norm=final_norm, n_ff=dff // tf,
                          row_chunk=row_chunk),
        grid=(bsz, s // tm, dff // tf),
        in_specs=[
            pl.BlockSpec((None, tm, d), tok),
            pl.BlockSpec((None, tm, attn.shape[-1]), tok),
            pl.BlockSpec((None, None, 1, mods.shape[-1]), lambda b, i, j: (layer, b, 0, 0)),
            pl.BlockSpec(w_o.shape, const),
            pl.BlockSpec((1, d), const),
            pl.BlockSpec((1, d), const),
            pl.BlockSpec((None, d, tf), lambda b, i, j: (layer, 0, j)),
            pl.BlockSpec((None, tf, d), lambda b, i, j: (layer, j, 0)),
            pl.BlockSpec((1, d), const),
        ],
        out_specs=pl.BlockSpec((None, tm, d), tok),
        out_shape=jax.ShapeDtypeStruct((bsz, s, d), F32),
        scratch_shapes=[pltpu.VMEM((tm, d), F32), pltpu.VMEM((tm, d), BF16),
                        pltpu.VMEM((tm, d), F32)],
        compiler_params=_params("parallel", "parallel", "arbitrary"),
        name="post_mlp_final" if final_norm else "post_mlp",
    )(x, attn, mods, w_o, b_o.reshape(1, d), g_mlp.reshape(1, d),
      w_ff1, w_ff2, g_final.reshape(1, d))


def _swa_proj_kernel(x_ref, mod_ref, g_ref, w_ref, b_ref, o_ref, *, scale):
    d = x_ref.shape[-1]
    mod = mod_ref[...]
    sh1, sc1 = mod[:, 0:d], mod[:, d:2 * d]
    h = (_rmsnorm(x_ref[...], g_ref[...]) * (1.0 + sc1) + sh1).astype(BF16)
    qkv = _dot(h, w_ref[...]) + b_ref[...]
    nq = SWA_HEADS * SWA_HEAD_DIM
    o_ref[:, 0:nq] = (qkv[:, 0:nq] * scale).astype(BF16)
    o_ref[:, nq:] = qkv[:, nq:].astype(BF16)


def _swa_window_order(t, axis):
    group = SWA_HEADS // SWA_KV_HEADS
    shape = t.shape
    t = t.reshape(shape[:axis] + (SWA_KV_HEADS, group, SWA_HEAD_DIM) + shape[axis + 1:])
    return jnp.swapaxes(t, axis, axis + 1).reshape(shape)


def _swa_proj(x, mods, layer, g, w_qkv, b_qkv):
    bsz, s, d = x.shape
    n = w_qkv.shape[1]
    tm = TILES.swa_proj
    const = lambda b, i: (0, 0)
    return pl.pallas_call(
        functools.partial(_swa_proj_kernel,
                          scale=float(SWA_HEAD_DIM ** -0.5 * math.log2(math.e))),
        grid=(bsz, s // tm),
        in_specs=[
            pl.BlockSpec((None, tm, d), lambda b, i: (b, i, 0)),
            pl.BlockSpec((None, None, 1, mods.shape[-1]), lambda b, i: (layer, b, 0, 0)),
            pl.BlockSpec((1, d), const),
            pl.BlockSpec(w_qkv.shape, const),
            pl.BlockSpec((1, n), const),
        ],
        out_specs=pl.BlockSpec((None, tm, n), lambda b, i: (b, i, 0)),
        out_shape=jax.ShapeDtypeStruct((bsz, s, n), BF16),
        compiler_params=_params("parallel", "parallel"),
        name="swa_proj",
    )(x, mods, g.reshape(1, d), w_qkv.astype(BF16), b_qkv.reshape(1, n))


def _swa_attn_kernel(sink_ref, q_ref, kc_ref, vc_ref, kp_ref, vp_ref, o_ref, bias_sc, *, slopes):
    first_step = jnp.logical_and(pl.program_id(0) == 0, pl.program_id(1) == 0)
    w = WINDOW
    dh = SWA_HEAD_DIM
    n_kv = SWA_KV_HEADS
    group = SWA_HEADS // n_kv
    gw = n_kv * dh
    log2e = math.log2(math.e)

    @pl.when(first_step)
    def _():
        qpos = lax.broadcasted_iota(jnp.int32, (w, 2 * w), 0)
        kpos = lax.broadcasted_iota(jnp.int32, (w, 2 * w), 1)
        dist = w + qpos - kpos
        in_window = jnp.logical_and(dist >= 0, dist < w)
        dist_f = dist.astype(F32)
        for hd in range(SWA_HEADS):
            bias_sc[hd * w:(hd + 1) * w, :] = jnp.where(
                in_window, dist_f * (-slopes[hd] * log2e), NEG_INF)

    no_prev = jnp.where(pl.program_id(1) == 0, NEG_INF, 0.0)
    ones_cols = jnp.ones((2 * w, w), BF16)
    lane_group = lax.broadcasted_iota(jnp.int32, (w, n_kv * dh), 1) // dh
    low_half = lax.broadcasted_iota(jnp.int32, (w, w), 1) < dh
    half_rows = (n_kv // 2) * group * w
    sink = jnp.concatenate([jnp.full((w, w), sink_ref[hd] * log2e, F32)
                            for hd in range(SWA_HEADS)], axis=0)

    def band(cur_ref, prev_ref, n):
        if n == 0:
            return jnp.concatenate([prev_ref[...], cur_ref[0:w, :]], axis=0)
        return cur_ref[(n - 1) * w:(n + 1) * w, :]

    def stage_scores(n):
        lhs = []
        for g in range(n_kv):
            for a in range(group):
                qwin = q_ref[n * w:(n + 1) * w, a * gw:(a + 1) * gw]
                lhs.append(jnp.where(lane_group == g, qwin, jnp.zeros_like(qwin)))
        s = _dot_nt(jnp.concatenate(lhs, axis=0), band(kc_ref, kp_ref, n)) + bias_sc[...]
        if n == 0:
            s = jnp.concatenate([s[:, 0:w] + no_prev, s[:, w:2 * w]], axis=1)
        return s

    def stage_softmax(s):
        m = jnp.maximum(jnp.max(s, axis=-1, keepdims=True), sink)
        return jnp.exp2(s - jnp.tile(m, (1, 2))).astype(BF16), m

    def stage_out(n, p, m):
        vall = band(vc_ref, vp_ref, n)
        pv_a = _dot(p[0:half_rows], jnp.concatenate([vall[:, 0:w], ones_cols], axis=1))
        pv_b = _dot(p[half_rows:], jnp.concatenate([ones_cols, vall[:, w:2 * w]], axis=1))
        o_a, sum_a = pv_a[:, 0:w], pv_a[:, w:2 * w]
        o_b, sum_b = pv_b[:, w:2 * w], pv_b[:, 0:w]

        def pick(t, a):
            return jnp.where(low_half, t[a * w:(a + 1) * w], t[(group + a) * w:(group + a + 1) * w])

        def merged(o, psum, sink_rows, m_rows, a):
            sink_term = jnp.exp2(pick(sink_rows, a) - pick(m_rows, a))
            return pick(o, a) / (pick(psum, a) + sink_term)

        for a in range(group):
            lo = merged(o_a, sum_a, sink[0:half_rows], m[0:half_rows], a)
            hi = merged(o_b, sum_b, sink[half_rows:], m[half_rows:], a)
            o_ref[n * w:(n + 1) * w, a * gw:(a + 1) * gw] = jnp.concatenate(
                [lo, hi], axis=1).astype(o_ref.dtype)

    n_blocks = q_ref.shape[0] // w
    scores, probs = {}, {}
    for step in range(n_blocks + 2):
        if step < n_blocks:
            scores[step] = stage_scores(step)
        if 1 <= step <= n_blocks:
            probs[step - 1] = stage_softmax(scores.pop(step - 1))
        if step >= 2:
            stage_out(step - 2, *probs.pop(step - 2))


def _swa_attn(qkv, sinks):
    bsz, s, _ = qkv.shape
    nq = SWA_HEADS * SWA_HEAD_DIM
    nkv = SWA_KV_HEADS * SWA_HEAD_DIM
    tm = TILES.swa_attn
    per = tm // WINDOW
    slopes = tuple(float(v) for v in 2.0 ** (-8.0 * np.arange(1, SWA_HEADS + 1) / SWA_HEADS))
    kcol, vcol = nq // nkv, nq // nkv + 1
    prev = lambda b, i: jnp.maximum(i * per - 1, 0)
    return pl.pallas_call(
        functools.partial(_swa_attn_kernel, slopes=slopes),
        grid=(bsz, s // tm),
        in_specs=[
            pl.BlockSpec(memory_space=pltpu.SMEM),
            pl.BlockSpec((None, tm, nq), lambda b, i: (b, i, 0)),
            pl.BlockSpec((None, tm, nkv), lambda b, i: (b, i, kcol)),
            pl.BlockSpec((None, tm, nkv), lambda b, i: (b, i, vcol)),
            pl.BlockSpec((None, WINDOW, nkv), lambda b, i: (b, prev(b, i), kcol)),
            pl.BlockSpec((None, WINDOW, nkv), lambda b, i: (b, prev(b, i), vcol)),
        ],
        out_specs=pl.BlockSpec((None, tm, nq), lambda b, i: (b, i, 0)),
        out_shape=jax.ShapeDtypeStruct((bsz, s, nq), BF16),
        scratch_shapes=[pltpu.VMEM((SWA_HEADS * WINDOW, 2 * WINDOW), F32)],
        compiler_params=_params("arbitrary", "arbitrary"),
        name="swa_attn",
    )(sinks.astype(F32), qkv, qkv, qkv, qkv, qkv)


def kernel(x, c, positions, w_ada, b_ada, g_mix, g_mlp, mla_w_dq, mla_g_q, mla_w_uq, mla_w_dkv,
           mla_g_kv, mla_w_ukv, mla_w_o, swa_w_qkv, swa_b_qkv, swa_sinks, swa_w_o, swa_b_o,
           w_ff1, w_ff2, g_final):
    depth = w_ada.shape[0]
    bsz, s, d = x.shape
    token_tiles = (TILES.rope_tokens, TILES.mla_proj, TILES.mlp, TILES.swa_proj, TILES.swa_attn)
    assert all(s % t == 0 for t in token_tiles), "sequence length must be a multiple of the tiles"
    assert (bsz * s) % TILES.rope_tokens == 0 and w_ff1.shape[-1] % TILES.mlp_ff == 0
    assert mla_w_ukv.shape[-1] == MLA_HEADS * (QK_NOPE + V_DIM) and d % LANES == 0
    mods = _adaln(c, w_ada, b_ada).reshape(depth, bsz, 1, w_ada.shape[-1])
    cos_t, sin_t = _rope_tables(positions)
    zero_bias = jnp.zeros((d,), F32)
    for i in range(depth):
        j = i // 2
        last = i == depth - 1
        if i % 2 == 0:
            q, k, v = _mla_proj(x, mods, i, g_mix[i], cos_t, sin_t, mla_w_dq[j], mla_g_q[j],
                                mla_w_uq[j], mla_w_dkv[j], mla_g_kv[j], mla_w_ukv[j])
            attn = _mla_flash(q, k, v)
            w_o, b_o = mla_w_o[j], zero_bias
        else:
            nq = SWA_HEADS * SWA_HEAD_DIM
            w_qkv = swa_w_qkv[j].astype(BF16)
            w_qkv = jnp.concatenate([_swa_window_order(w_qkv[:, :nq], 1), w_qkv[:, nq:]], axis=1)
            b_qkv = jnp.concatenate([_swa_window_order(swa_b_qkv[j][:nq], 0), swa_b_qkv[j][nq:]])
            qkv = _swa_proj(x, mods, i, g_mix[i], w_qkv, b_qkv)
            attn = _swa_attn(qkv, swa_sinks[j])
            w_o, b_o = _swa_window_order(swa_w_o[j].astype(BF16), 0), swa_b_o[j]
        x = _post_mlp(x, attn, mods, i, w_o, b_o, g_mlp[i], w_ff1, w_ff2, g_final, last)
    return x
```

```python
import functools
import math
from typing import NamedTuple

import numpy as np
import jax
import jax.numpy as jnp
from jax import lax
from jax.experimental import pallas as pl
from jax.experimental.pallas import tpu as pltpu

F32 = jnp.float32
BF16 = jnp.bfloat16

EPS = 1e-6
ROPE_THETA = 10000.0
MLA_HEADS = 8
QK_NOPE = 128
QK_ROPE = 64
V_DIM = 128
KV_LORA = 256
SWA_HEADS = 16
SWA_KV_HEADS = 4
SWA_HEAD_DIM = 64
WINDOW = 128

LANES = 128
QK_PAD = 2 * LANES
ONES_ROWS = 16
VMEM_LIMIT = 56 * 1024 * 1024

NEG_INF = float("-inf")


class _Tiles(NamedTuple):
    adaln_cols: int = 2048
    rope_tokens: int = 2048
    mla_proj: int = 1024
    mla_proj_chunk: int = 1024
    mlp: int = 1024
    mlp_ff: int = 1024
    mlp_chunk: int = 1024
    swa_proj: int = 1024
    swa_attn: int = 2048


TILES = _Tiles()


def _params(*sem):
    return pltpu.CompilerParams(dimension_semantics=sem, vmem_limit_bytes=VMEM_LIMIT)


def _rmsnorm(x, g):
    y = x * lax.rsqrt(jnp.mean(x * x, axis=-1, keepdims=True) + EPS)
    return y * g


def _dot(a, b):
    return jnp.dot(a, b, preferred_element_type=F32)


def _dot_nt(a, b):
    return lax.dot_general(a, b, (((1,), (1,)), ((), ())), preferred_element_type=F32)


def _adaln_kernel(ct_ref, w_ref, b_ref, o_ref):
    ct = ct_ref[...]
    cond = ct * jax.nn.sigmoid(ct)
    w = w_ref[...]
    rows = [jnp.sum(w * cond[:, b:b + 1], axis=0, keepdims=True)
            for b in range(ct.shape[1])]
    o_ref[...] = jnp.concatenate(rows, axis=0) + b_ref[...]


def _adaln(c, w_ada, b_ada):
    depth, d, n = w_ada.shape
    bsz = c.shape[0]
    tn = TILES.adaln_cols
    return pl.pallas_call(
        _adaln_kernel,
        grid=(depth, n // tn),
        in_specs=[
            pl.BlockSpec((d, bsz), lambda l, j: (0, 0)),
            pl.BlockSpec((None, d, tn), lambda l, j: (l, 0, j)),
            pl.BlockSpec((None, 1, tn), lambda l, j: (l, 0, j)),
        ],
        out_specs=pl.BlockSpec((None, bsz, tn), lambda l, j: (l, 0, j)),
        out_shape=jax.ShapeDtypeStruct((depth, bsz, n), F32),
        compiler_params=_params("parallel", "parallel"),
        name="adaln",
    )(c.T, w_ada, b_ada.reshape(depth, 1, n))


def _rope_table_kernel(pos_ref, inv_ref, cos_ref, sin_ref):
    ang = inv_ref[...] * pos_ref[...].astype(F32)
    cos_ref[...] = jnp.cos(ang)
    sin_ref[...] = jnp.sin(ang)


def _rope_tables(positions):
    n = positions.size
    half = QK_ROPE // 2
    inv_freq = ROPE_THETA ** (-jnp.arange(half, dtype=F32) / half)
    tm = TILES.rope_tokens
    return pl.pallas_call(
        _rope_table_kernel,
        grid=(n // tm,),
        in_specs=[pl.BlockSpec((1, tm), lambda i: (0, i)),
                  pl.BlockSpec((half, 1), lambda i: (0, 0))],
        out_specs=[pl.BlockSpec((half, tm), lambda i: (0, i)),
                   pl.BlockSpec((half, tm), lambda i: (0, i))],
        out_shape=[jax.ShapeDtypeStruct((half, n), F32)] * 2,
        compiler_params=_params("parallel"),
        name="rope_table",
    )(positions.reshape(1, n), inv_freq.reshape(half, 1))


def _mla_proj_kernel(x_ref, mod_ref, g_ref, cos_ref, sin_ref, wdq_ref, gq_ref, wuqt_ref,
                     wdkv_ref, gkv_ref, wuk_ref, wuvt_ref, qt_ref, k_ref, vt_ref, *, scale,
                     row_chunk):
    d = x_ref.shape[-1]
    mod = mod_ref[...]
    sh1, sc1 = mod[:, 0:d], mod[:, d:2 * d]
    reps = LANES // cos_ref.shape[0]
    n_chunks = x_ref.shape[0] // row_chunk
    zero_rows = jnp.zeros((QK_PAD - QK_NOPE - QK_ROPE, row_chunk), BF16)

    def down(c):
        rows = slice(c * row_chunk, (c + 1) * row_chunk)
        h = (_rmsnorm(x_ref[rows, :], g_ref[...]) * (1.0 + sc1) + sh1).astype(BF16)
        return _dot(h, wdq_ref[...]), _dot(h, wdkv_ref[...])

    def up(cq_raw, dkv):
        cq = _rmsnorm(cq_raw, gq_ref[...]).astype(BF16)
        ckv = _rmsnorm(dkv[:, 0:KV_LORA], gkv_ref[...]).astype(BF16)
        return (_dot_nt(wuqt_ref[...], cq), _dot(ckv, wuk_ref[...]),
                _dot_nt(wuvt_ref[...], ckv))

    def emit(c, dkv, qt, kn, vt):
        rows = slice(c * row_chunk, (c + 1) * row_chunk)
        cos_rep = jnp.tile(cos_ref[:, rows], (reps, 1))
        sin_rep = jnp.tile(sin_ref[:, rows], (reps, 1))
        cos_t, sin_t = cos_rep[0:QK_ROPE, :], sin_rep[0:QK_ROPE, :]
        cos, sin = cos_rep.T, sin_rep.T
        for hd in range(MLA_HEADS):
            base = hd * (QK_NOPE + QK_ROPE)
            qt_ref[hd, 0:QK_NOPE, rows] = (qt[base:base + QK_NOPE, :] * scale).astype(BF16)
            a = qt[base + QK_NOPE:base + QK_NOPE + QK_ROPE, :]
            b = jnp.concatenate([-a[QK_ROPE // 2:, :], a[0:QK_ROPE // 2, :]], axis=0)
            qt_ref[hd, QK_NOPE:QK_NOPE + QK_ROPE, rows] = (
                (a * cos_t + b * sin_t) * scale).astype(BF16)
            qt_ref[hd, QK_NOPE + QK_ROPE:QK_PAD, rows] = zero_rows
        kr = (dkv[:, KV_LORA:KV_LORA + LANES] * cos
              + dkv[:, KV_LORA + LANES:KV_LORA + 2 * LANES] * sin).astype(BF16)
        for hd in range(MLA_HEADS):
            k_ref[hd, rows, 0:LANES] = kn[:, hd * QK_NOPE:(hd + 1) * QK_NOPE].astype(BF16)
            k_ref[hd, rows, LANES:QK_PAD] = kr
            vt_ref[hd, :, rows] = vt[hd * V_DIM:(hd + 1) * V_DIM, :].astype(BF16)

    downs, ups = {}, {}
    for step in range(n_chunks + 2):
        if step < n_chunks:
            downs[step] = down(step)
        if 1 <= step <= n_chunks:
            ups[step - 1] = up(*downs[step - 1])
        if step >= 2:
            emit(step - 2, downs.pop(step - 2)[1], *ups.pop(step - 2))


def _rot_half(w):
    half = w.shape[-1] // 2
    return jnp.concatenate([-w[..., half:], w[..., :half]], axis=-1)


def _mla_proj(x, mods, layer, g, cos_t, sin_t, w_dq, g_q, w_uq, w_dkv, g_kv, w_ukv):
    bsz, s, d = x.shape
    hh = MLA_HEADS
    q_lora = w_dq.shape[1]
    tm, row_chunk = TILES.mla_proj, TILES.mla_proj_chunk
    wuqt_ext = w_uq.astype(BF16).T
    wkr = w_dkv[:, KV_LORA:]
    pad = jnp.zeros((d, LANES - QK_ROPE), w_dkv.dtype)
    wdkv_ext = jnp.concatenate([w_dkv[:, :KV_LORA], wkr, pad, _rot_half(wkr), pad],
                               axis=1).astype(BF16)
    wkv3 = w_ukv.astype(BF16).reshape(KV_LORA, hh, QK_NOPE + V_DIM)
    wuk = wkv3[:, :, :QK_NOPE].reshape(KV_LORA, hh * QK_NOPE)
    wuvt = wkv3[:, :, QK_NOPE:].reshape(KV_LORA, hh * V_DIM).T
    scale = float((QK_NOPE + QK_ROPE) ** -0.5 * math.log2(math.e))

    const = lambda b, i: (0, 0)
    tok = lambda b, i: (0, b * (s // tm) + i)
    out_map = lambda b, i: (b, 0, i, 0)
    out_map_t = lambda b, i: (b, 0, 0, i)
    return pl.pallas_call(
        functools.partial(_mla_proj_kernel, scale=scale, row_chunk=row_chunk),
        grid=(bsz, s // tm),
        in_specs=[
            pl.BlockSpec((None, tm, d), lambda b, i: (b, i, 0)),
            pl.BlockSpec((None, None, 1, mods.shape[-1]), lambda b, i: (layer, b, 0, 0)),
            pl.BlockSpec((1, d), const),
            pl.BlockSpec((cos_t.shape[0], tm), tok),
            pl.BlockSpec((sin_t.shape[0], tm), tok),
            pl.BlockSpec(w_dq.shape, const),
            pl.BlockSpec((1, q_lora), const),
            pl.BlockSpec(wuqt_ext.shape, const),
            pl.BlockSpec(wdkv_ext.shape, const),
            pl.BlockSpec((1, KV_LORA), const),
            pl.BlockSpec(wuk.shape, const),
            pl.BlockSpec(wuvt.shape, const),
        ],
        out_specs=[
            pl.BlockSpec((None, hh, None, QK_PAD, tm), lambda b, i: (b, 0, i, 0, 0)),
            pl.BlockSpec((None, hh, tm, QK_PAD), out_map),
            pl.BlockSpec((None, hh, V_DIM, tm), out_map_t),
        ],
        out_shape=[
            jax.ShapeDtypeStruct((bsz, hh, s // tm, QK_PAD, tm), BF16),
            jax.ShapeDtypeStruct((bsz, hh, s, QK_PAD), BF16),
            jax.ShapeDtypeStruct((bsz, hh, V_DIM, s), BF16),
        ],
        compiler_params=pltpu.CompilerParams(
            dimension_semantics=("parallel", "parallel"), vmem_limit_bytes=VMEM_LIMIT,
            allow_input_fusion=[i in (5, 7, 8, 10, 11) for i in range(12)]),
        name="mla_proj",
    )(x, mods, g.reshape(1, d), cos_t, sin_t, w_dq.astype(BF16), g_q.reshape(1, q_lora),
      wuqt_ext, wdkv_ext, g_kv.reshape(1, KV_LORA), wuk, wuvt)


def _mla_flash_kernel(qt_ref, k_ref, vt_ref, o_ref, vblk_sc, sn_sc, s0_sc, s1_sc, m_sc, acc_sc,
                      *, tq):
    half = tq // 2
    n_tiles = qt_ref.shape[0]

    for t in range(n_tiles):
        vblk_sc[t, 0:V_DIM, :] = vt_ref[:, t * tq:(t + 1) * tq]
        vblk_sc[t, V_DIM:, :] = jnp.ones((ONES_ROWS, tq), BF16)

    def tile(qi, carry):
        m_sc[...] = jnp.full(m_sc.shape, NEG_INF, F32)
        acc_sc[...] = jnp.zeros(acc_sc.shape, F32)

        def scores(t, s_ref):
            k = k_ref[pl.ds(pl.multiple_of(t * tq, tq), tq), :]
            s_ref[...] = _dot(k, qt_ref[qi])

        def prefetch():
            sn_sc[...] = _dot(k_ref[0:tq, :], qt_ref[jnp.minimum(qi + 1, n_tiles - 1)])

        def scores_diag(s_ref):
            base = pl.multiple_of(qi * tq, tq)
            s_ref[0:half, :] = _dot(k_ref[pl.ds(base, half), :], qt_ref[qi])
            s_ref[half:tq, half:tq] = _dot(k_ref[pl.ds(base + half, half), :],
                                           qt_ref[qi, :, half:tq])

        def softmax_pv(s, vt, cols):
            m_prev = m_sc[:, cols]
            m_new = jnp.maximum(m_prev, jnp.max(s, axis=0, keepdims=True))
            alpha = jnp.exp2(m_prev - m_new)
            p = jnp.exp2(s - m_new).astype(BF16)
            acc_sc[:, cols] = alpha * acc_sc[:, cols] + _dot(vt, p)
            m_sc[:, cols] = m_new

        def update(t, s_ref):
            softmax_pv(s_ref[...], vblk_sc[t], slice(0, tq))

        def update_diag(s_ref):
            causal = (lax.broadcasted_iota(jnp.int32, (half, half), 0)
                      <= lax.broadcasted_iota(jnp.int32, (half, half), 1))
            vt = vblk_sc[qi]
            s_top = s_ref[0:half, :]
            s_top = jnp.concatenate(
                [jnp.where(causal, s_top[:, 0:half], NEG_INF), s_top[:, half:tq]], axis=1)
            softmax_pv(s_top, vt[:, 0:half], slice(0, tq))
            s_bot = jnp.where(causal, s_ref[half:tq, half:tq], NEG_INF)
            softmax_pv(s_bot, vt[:, half:tq], slice(half, tq))

        def pair(u, c):
            t = 2 * u + 1
            scores(t + 1, s0_sc)
            update(t, s1_sc)
            scores(t + 2, s1_sc)
            update(t + 1, s0_sc)
            return c

        @pl.when(qi == 0)
        def _():
            scores_diag(s0_sc)
            prefetch()
            update_diag(s0_sc)

        @pl.when(qi == 1)
        def _():
            scores_diag(s1_sc)
            update(0, sn_sc)
            prefetch()
            update_diag(s1_sc)

        @pl.when(qi >= 2)
        def _():
            scores(1, s1_sc)
            update(0, sn_sc)

        even = qi % 2 == 0
        n_pairs = jnp.where(qi >= 2, jnp.where(even, (qi - 2) // 2, (qi - 3) // 2), 0)
        lax.fori_loop(0, n_pairs, pair, 0)

        @pl.when(jnp.logical_and(qi >= 2, even))
        def _():
            scores_diag(s0_sc)
            update(qi - 1, s1_sc)
            prefetch()
            update_diag(s0_sc)

        @pl.when(jnp.logical_and(qi >= 2, jnp.logical_not(even)))
        def _():
            scores(qi - 1, s0_sc)
            update(qi - 2, s1_sc)
            scores_diag(s1_sc)
            update(qi - 1, s0_sc)
            prefetch()
            update_diag(s1_sc)

        acc = acc_sc[...]
        o_t = acc[0:V_DIM, :] / acc[V_DIM:V_DIM + 1, :]
        o_ref[pl.ds(pl.multiple_of(qi * tq, tq), tq), :] = o_t.T.astype(o_ref.dtype)
        return carry

    lax.fori_loop(0, n_tiles, tile, 0)


def _mla_flash(qt, k, vt):
    bsz, hh, s, _ = k.shape
    tq = qt.shape[-1]
    return pl.pallas_call(
        functools.partial(_mla_flash_kernel, tq=tq),
        grid=(bsz, hh),
        in_specs=[
            pl.BlockSpec((None, None) + qt.shape[2:], lambda b, h: (b, h, 0, 0, 0)),
            pl.BlockSpec((None, None, s, QK_PAD), lambda b, h: (b, h, 0, 0)),
            pl.BlockSpec((None, None, V_DIM, s), lambda b, h: (b, h, 0, 0)),
        ],
        out_specs=pl.BlockSpec((None, s, V_DIM), lambda b, h: (b, 0, h)),
        out_shape=jax.ShapeDtypeStruct((bsz, s, hh * V_DIM), BF16),
        scratch_shapes=[pltpu.VMEM((s // tq, V_DIM + ONES_ROWS, tq), BF16),
                        pltpu.VMEM((tq, tq), F32), pltpu.VMEM((tq, tq), F32),
                        pltpu.VMEM((tq, tq), F32),
                        pltpu.VMEM((1, tq), F32), pltpu.VMEM((V_DIM + ONES_ROWS, tq), F32)],
        compiler_params=_params("parallel", "parallel"),
        name="mla_flash",
    )(qt, k, vt)


def _post_mlp_kernel(x_ref, a_ref, mod_ref, wo_ref, bo_ref, g_ref, w1_ref, w2_ref, gf_ref,
                     out_ref, x1_sc, h_sc, acc_sc, *, final_norm, n_ff, row_chunk):
    j = pl.program_id(2)
    d = x_ref.shape[-1]
    tm = x_ref.shape[0]

    def step(first, last):
        mod = mod_ref[...]
        gt1, sh2, sc2 = mod[:, 2 * d:3 * d], mod[:, 3 * d:4 * d], mod[:, 4 * d:5 * d]
        gt2 = mod[:, 5 * d:6 * d]
        w1 = w1_ref[...].astype(BF16)
        w2 = w2_ref[...].astype(BF16)
        for r in range(tm // row_chunk):
            rows = slice(r * row_chunk, (r + 1) * row_chunk)
            if first:
                y = _dot(a_ref[rows, :], wo_ref[...].astype(BF16)) + bo_ref[...]
                x1 = x_ref[rows, :] + gt1 * y
                h = (_rmsnorm(x1, g_ref[...]) * (1.0 + sc2) + sh2).astype(BF16)
                h_sc[rows, :] = h
                if not last:
                    x1_sc[rows, :] = x1
            else:
                h = h_sc[rows, :]
            u = jnp.square(jnp.maximum(_dot(h, w1), 0.0)).astype(BF16)
            acc = _dot(u, w2)
            if not first:
                acc = acc_sc[rows, :] + acc
            if last:
                x2 = (x1 if first else x1_sc[rows, :]) + gt2 * acc
                if final_norm:
                    x2 = _rmsnorm(x2, gf_ref[...])
                out_ref[rows, :] = x2
            else:
                acc_sc[rows, :] = acc

    pl.when(j == 0)(functools.partial(step, True, n_ff == 1))
    if n_ff > 1:
        pl.when(j == n_ff - 1)(functools.partial(step, False, True))
    if n_ff > 2:
        pl.when(jnp.logical_and(j > 0, j < n_ff - 1))(functools.partial(step, False, False))


def _post_mlp(x, attn, mods, layer, w_o, b_o, g_mlp, w_ff1, w_ff2, g_final, final_norm):
    bsz, s, d = x.shape
    dff = w_ff1.shape[-1]
    tm, tf, row_chunk = TILES.mlp, TILES.mlp_ff, TILES.mlp_chunk
    const = lambda b, i, j: (0, 0)
    tok = lambda b, i, j: (b, i, 0)
    return pl.pallas_call(
        functools.partial(_post_mlp_kernel, final_norm=final_norm, n_ff=dff // tf,
                          row_chunk=row_chunk),
        grid=(bsz, s // tm, dff // tf),
        in_specs=[
            pl.BlockSpec((None, tm, d), tok),
            pl.BlockSpec((None, tm, attn.shape[-1]), tok),
            pl.BlockSpec((None, None, 1, mods.shape[-1]), lambda b, i, j: (layer, b, 0, 0)),
            pl.BlockSpec(w_o.shape, const),
            pl.BlockSpec((1, d), const),
            pl.BlockSpec((1, d), const),
            pl.BlockSpec((None, d, tf), lambda b, i, j: (layer, 0, j)),
            pl.BlockSpec((None, tf, d), lambda b, i, j: (layer, j, 0)),
            pl.BlockSpec((1, d), const),
        ],
        out_specs=pl.BlockSpec((None, tm, d), tok),
        out_shape=jax.ShapeDtypeStruct((bsz, s, d), F32),
        scratch_shapes=[pltpu.VMEM((tm, d), F32), pltpu.VMEM((tm, d), BF16),
                        pltpu.VMEM((tm, d), F32)],
        compiler_params=_params("parallel", "parallel", "arbitrary"),
        name="post_mlp_final" if final_norm else "post_mlp",
    )(x, attn, mods, w_o, b_o.reshape(1, d), g_mlp.reshape(1, d),
      w_ff1, w_ff2, g_final.reshape(1, d))


def _swa_proj_kernel(x_ref, mod_ref, g_ref, w_ref, b_ref, o_ref, *, scale):
    d = x_ref.shape[-1]
    mod = mod_ref[...]
    sh1, sc1 = mod[:, 0:d], mod[:, d:2 * d]
    h = (_rmsnorm(x_ref[...], g_ref[...]) * (1.0 + sc1) + sh1).astype(BF16)
    qkv = _dot(h, w_ref[...]) + b_ref[...]
    nq = SWA_HEADS * SWA_HEAD_DIM
    o_ref[:, 0:nq] = (qkv[:, 0:nq] * scale).astype(BF16)
    o_ref[:, nq:] = qkv[:, nq:].astype(BF16)


def _swa_window_order(t, axis):
    group = SWA_HEADS // SWA_KV_HEADS
    shape = t.shape
    t = t.reshape(shape[:axis] + (SWA_KV_HEADS, group, SWA_HEAD_DIM) + shape[axis + 1:])
    return jnp.swapaxes(t, axis, axis + 1).reshape(shape)


def _swa_proj(x, mods, layer, g, w_qkv, b_qkv):
    bsz, s, d = x.shape
    n = w_qkv.shape[1]
    tm = TILES.swa_proj
    const = lambda b, i: (0, 0)
    return pl.pallas_call(
        functools.partial(_swa_proj_kernel,
                          scale=float(SWA_HEAD_DIM ** -0.5 * math.log2(math.e))),
        grid=(bsz, s // tm),
        in_specs=[
            pl.BlockSpec((None, tm, d), lambda b, i: (b, i, 0)),
            pl.BlockSpec((None, None, 1, mods.shape[-1]), lambda b, i: (layer, b, 0, 0)),
            pl.BlockSpec((1, d), const),
            pl.BlockSpec(w_qkv.shape, const),
            pl.BlockSpec((1, n), const),
        ],
        out_specs=pl.BlockSpec((None, tm, n), lambda b, i: (b, i, 0)),
        out_shape=jax.ShapeDtypeStruct((bsz, s, n), BF16),
        compiler_params=pltpu.CompilerParams(
            dimension_semantics=("parallel", "parallel"), vmem_limit_bytes=VMEM_LIMIT,
            allow_input_fusion=[False, False, False, True, True]),
        name="swa_proj",
    )(x, mods, g.reshape(1, d), w_qkv.astype(BF16), b_qkv.reshape(1, n))


def _swa_attn_kernel(sink_ref, q_ref, kc_ref, vc_ref, kp_ref, vp_ref, o_ref, bias_sc, *, slopes):
    first_step = jnp.logical_and(pl.program_id(0) == 0, pl.program_id(1) == 0)
    w = WINDOW
    dh = SWA_HEAD_DIM
    n_kv = SWA_KV_HEADS
    group = SWA_HEADS // n_kv
    gw = n_kv * dh
    log2e = math.log2(math.e)

    @pl.when(first_step)
    def _():
        qpos = lax.broadcasted_iota(jnp.int32, (w, 2 * w), 0)
        kpos = lax.broadcasted_iota(jnp.int32, (w, 2 * w), 1)
        dist = w + qpos - kpos
        in_window = jnp.logical_and(dist >= 0, dist < w)
        dist_f = dist.astype(F32)
        for hd in range(SWA_HEADS):
            bias_sc[hd * w:(hd + 1) * w, :] = jnp.where(
                in_window, dist_f * (-slopes[hd] * log2e), NEG_INF)

    no_prev = jnp.where(pl.program_id(1) == 0, NEG_INF, 0.0)
    ones_cols = jnp.ones((2 * w, w), BF16)
    lane_group = lax.broadcasted_iota(jnp.int32, (w, n_kv * dh), 1) // dh
    low_half = lax.broadcasted_iota(jnp.int32, (w, w), 1) < dh
    half_rows = (n_kv // 2) * group * w
    sink = jnp.concatenate([jnp.full((w, w), sink_ref[hd] * log2e, F32)
                            for hd in range(SWA_HEADS)], axis=0)

    def band(cur_ref, prev_ref, n):
        if n == 0:
            return jnp.concatenate([prev_ref[...], cur_ref[0:w, :]], axis=0)
        return cur_ref[(n - 1) * w:(n + 1) * w, :]

    def stage_scores(n):
        lhs = []
        for g in range(n_kv):
            for a in range(group):
                qwin = q_ref[n * w:(n + 1) * w, a * gw:(a + 1) * gw]
                lhs.append(jnp.where(lane_group == g, qwin, jnp.zeros_like(qwin)))
        s = _dot_nt(jnp.concatenate(lhs, axis=0), band(kc_ref, kp_ref, n)) + bias_sc[...]
        if n == 0:
            s = jnp.concatenate([s[:, 0:w] + no_prev, s[:, w:2 * w]], axis=1)
        return s

    def stage_softmax(s):
        m = jnp.maximum(jnp.max(s, axis=-1, keepdims=True), sink)
        return jnp.exp2(s - jnp.tile(m, (1, 2))).astype(BF16), m

    def stage_out(n, p, m):
        vall = band(vc_ref, vp_ref, n)
        pv_a = _dot(p[0:half_rows], jnp.concatenate([vall[:, 0:w], ones_cols], axis=1))
        pv_b = _dot(p[half_rows:], jnp.concatenate([ones_cols, vall[:, w:2 * w]], axis=1))
        o_a, sum_a = pv_a[:, 0:w], pv_a[:, w:2 * w]
        o_b, sum_b = pv_b[:, w:2 * w], pv_b[:, 0:w]

        def pick(t, a):
            return jnp.where(low_half, t[a * w:(a + 1) * w], t[(group + a) * w:(group + a + 1) * w])

        def merged(o, psum, sink_rows, m_rows, a):
            sink_term = jnp.exp2(pick(sink_rows, a) - pick(m_rows, a))
            return pick(o, a) / (pick(psum, a) + sink_term)

        for a in range(group):
            lo = merged(o_a, sum_a, sink[0:half_rows], m[0:half_rows], a)
            hi = merged(o_b, sum_b, sink[half_rows:], m[half_rows:], a)
            o_ref[n * w:(n + 1) * w, a * gw:(a + 1) * gw] = jnp.concatenate(
                [lo, hi], axis=1).astype(o_ref.dtype)

    n_blocks = q_ref.shape[0] // w
    scores, probs = {}, {}
    for step in range(n_blocks + 2):
        if step < n_blocks:
            scores[step] = stage_scores(step)
        if 1 <= step <= n_blocks:
            probs[step - 1] = stage_softmax(scores.pop(step - 1))
        if step >= 2:
            stage_out(step - 2, *probs.pop(step - 2))


def _swa_attn(qkv, sinks):
    bsz, s, _ = qkv.shape
    nq = SWA_HEADS * SWA_HEAD_DIM
    nkv = SWA_KV_HEADS * SWA_HEAD_DIM
    tm = TILES.swa_attn
    per = tm // WINDOW
    slopes = tuple(float(v) for v in 2.0 ** (-8.0 * np.arange(1, SWA_HEADS + 1) / SWA_HEADS))
    kcol, vcol = nq // nkv, nq // nkv + 1
    prev = lambda b, i: jnp.maximum(i * per - 1, 0)
    return pl.pallas_call(
        functools.partial(_swa_attn_kernel, slopes=slopes),
        grid=(bsz, s // tm),
        in_specs=[
            pl.BlockSpec(memory_space=pltpu.SMEM),
            pl.BlockSpec((None, tm, nq), lambda b, i: (b, i, 0)),
            pl.BlockSpec((None, tm, nkv), lambda b, i: (b, i, kcol)),
            pl.BlockSpec((None, tm, nkv), lambda b, i: (b, i, vcol)),
            pl.BlockSpec((None, WINDOW, nkv), lambda b, i: (b, prev(b, i), kcol)),
            pl.BlockSpec((None, WINDOW, nkv), lambda b, i: (b, prev(b, i), vcol)),
        ],
        out_specs=pl.BlockSpec((None, tm, nq), lambda b, i: (b, i, 0)),
        out_shape=jax.ShapeDtypeStruct((bsz, s, nq), BF16),
        scratch_shapes=[pltpu.VMEM((SWA_HEADS * WINDOW, 2 * WINDOW), F32)],
        compiler_params=_params("arbitrary", "arbitrary"),
        name="swa_attn",
    )(sinks.astype(F32), qkv, qkv, qkv, qkv, qkv)


def kernel(x, c, positions, w_ada, b_ada, g_mix, g_mlp, mla_w_dq, mla_g_q, mla_w_uq, mla_w_dkv,
           mla_g_kv, mla_w_ukv, mla_w_o, swa_w_qkv, swa_b_qkv, swa_sinks, swa_w_o, swa_b_o,
           w_ff1, w_ff2, g_final):
    depth = w_ada.shape[0]
    bsz, s, d = x.shape
    token_tiles = (TILES.rope_tokens, TILES.mla_proj, TILES.mlp, TILES.swa_proj, TILES.swa_attn)
    assert all(s % t == 0 for t in token_tiles), "sequence length must be a multiple of the tiles"
    assert (bsz * s) % TILES.rope_tokens == 0 and w_ff1.shape[-1] % TILES.mlp_ff == 0
    assert mla_w_ukv.shape[-1] == MLA_HEADS * (QK_NOPE + V_DIM) and d % LANES == 0
    mods = _adaln(c, w_ada, b_ada).reshape(depth, bsz, 1, w_ada.shape[-1])
    cos_t, sin_t = _rope_tables(positions)
    zero_bias = jnp.zeros((d,), F32)
    for i in range(depth):
        j = i // 2
        last = i == depth - 1
        if i % 2 == 0:
            q, k, v = _mla_proj(x, mods, i, g_mix[i], cos_t, sin_t, mla_w_dq[j], mla_g_q[j],
                                mla_w_uq[j], mla_w_dkv[j], mla_g_kv[j], mla_w_ukv[j])
            attn = _mla_flash(q, k, v)
            w_o, b_o = mla_w_o[j], zero_bias
        else:
            nq = SWA_HEADS * SWA_HEAD_DIM
            w_qkv = swa_w_qkv[j].astype(BF16)
            w_qkv = jnp.concatenate([_swa_window_order(w_qkv[:, :nq], 1), w_qkv[:, nq:]], axis=1)
            b_qkv = jnp.concatenate([_swa_window_order(swa_b_qkv[j][:nq], 0), swa_b_qkv[j][nq:]])
            qkv = _swa_proj(x, mods, i, g_mix[i], w_qkv, b_qkv)
            attn = _swa_attn(qkv, swa_sinks[j])
            w_o, b_o = _swa_window_order(swa_w_o[j].astype(BF16), 0), swa_b_o[j]
        x = _post_mlp(x, attn, mods, i, w_o, b_o, g_mlp[i], w_ff1, w_ff2, g_final, last)
    return x
```
